```python
import jax, jax.numpy as jnp
from jax import lax
import numpy as np

D_MODEL = 2048
BATCH = 1
SEQ = 8192
DEPTH = 4

CHUNK = 64
N_BRANCH = 4
BRANCH_W = 512
RMS_EPS = 1e-6
GN_EPS = 1e-5
ROPE_THETA = 10000.0

RET_HEADS = 4
RET_DK = 128
RET_DV = 128
SSM_HEADS = 8
SSM_HEADDIM = 64
SSM_GROUPS = 2
SSM_STATE = 128
SSM_CONV = 4
SSM_HPG = SSM_HEADS // SSM_GROUPS
SSM_XBC = SSM_HEADS * SSM_HEADDIM + 2 * SSM_GROUPS * SSM_STATE
MLA_HEADS = 4
MLA_Q_RANK = 512
MLA_KV_RANK = 256
MLA_NOPE = 128
MLA_ROPE = 64
MLA_V = 128
MLA_QK = MLA_NOPE + MLA_ROPE
Q_BLOCK = 128
RWKV_HEADS = 8
RWKV_HEAD = 64
RWKV_W = RWKV_HEADS * RWKV_HEAD
RWKV_W_LORA = 64
RWKV_A_LORA = 64
RWKV_V_LORA = 32
RWKV_G_LORA = 128
RWKV_GN_EPS = 64e-5
RWKV_COLS = 3 * RWKV_W + RWKV_W_LORA + RWKV_A_LORA + RWKV_G_LORA
D_FF = -(-8 * D_MODEL // (3 * 256)) * 256

IN_SIZES = (
    RET_HEADS * RET_DK, RET_HEADS * RET_DK, RET_HEADS * RET_DV, RET_HEADS * RET_DV,
    SSM_HEADS * SSM_HEADDIM, SSM_XBC, SSM_HEADS,
    MLA_Q_RANK, MLA_KV_RANK, MLA_ROPE,
    RWKV_COLS,
    N_BRANCH * D_MODEL,
)
N_IN = sum(IN_SIZES)

kernel_name = "hybrid_ret_ssd_mla_rwkv7_gated_trunk"


def _split(x, sizes):
    idx = np.cumsum(np.array(sizes))[:-1].tolist()
    return jnp.split(x, idx, axis=-1)


def rms_norm(x, w, eps=RMS_EPS):
    xf = x.astype(jnp.float32)
    y = xf * lax.rsqrt(jnp.mean(xf * xf, axis=-1, keepdims=True) + eps)
    return (y * w.astype(jnp.float32)).astype(x.dtype)


def head_layer_norm(x, eps):
    xf = x.astype(jnp.float32)
    xc = xf - jnp.mean(xf, axis=-1, keepdims=True)
    return xc * lax.rsqrt(jnp.mean(xc * xc, axis=-1, keepdims=True) + eps)


def rope_tables(positions, dim):
    inv = 1.0 / (ROPE_THETA ** (jnp.arange(0, dim, 2, dtype=jnp.float32) / dim))
    ang = positions.astype(jnp.float32)[..., None] * inv
    return jnp.cos(ang), jnp.sin(ang)


def apply_rope(x, cos, sin):
    half = x.shape[-1] // 2
    x1 = x[..., :half].astype(jnp.float32)
    x2 = x[..., half:].astype(jnp.float32)
    c = cos[:, :, None, :]
    s = sin[:, :, None, :]
    return jnp.concatenate([x1 * c - x2 * s, x2 * c + x1 * s], axis=-1).astype(x.dtype)


def retention_mixer(q, k, v, g, cos, sin, gn_w):
    B, S, _ = q.shape
    NC = S // CHUNK
    H, DK, DV = RET_HEADS, RET_DK, RET_DV
    f32 = jnp.float32
    q = apply_rope(q.reshape(B, S, H, DK), cos, sin).astype(f32)
    k = apply_rope(k.reshape(B, S, H, DK), cos, sin).astype(f32) * (DK ** -0.5)
    v = v.reshape(B, S, H, DV).astype(f32)
    log_gamma = jnp.log1p(-jnp.exp2(-5.0 - jnp.arange(H, dtype=f32)))
    pos = jnp.arange(CHUNK, dtype=f32)
    intra_decay = jnp.exp(log_gamma[:, None, None] * jnp.abs(pos[:, None] - pos[None, :]))
    q_decay = jnp.exp(log_gamma[:, None] * (pos + 1.0))
    k_decay = jnp.exp(log_gamma[:, None] * (CHUNK - 1.0 - pos))
    chunk_decay = jnp.exp(log_gamma * CHUNK)[None, :, None, None]
    qc = q.reshape(B, NC, CHUNK, H, DK)
    kc = k.reshape(B, NC, CHUNK, H, DK)
    vc = v.reshape(B, NC, CHUNK, H, DV)
    scores = jnp.einsum('bclhd,bcmhd->bchlm', qc, kc) * intra_decay
    intra = jnp.einsum('bchlm,bcmhe->bclhe', scores, vc)
    kv = jnp.einsum('bcmhd,hm,bcmhe->bchde', kc, k_decay, vc)

    def step(state, kv_c):
        return chunk_decay * state + kv_c, state

    _, s_prev = lax.scan(step, jnp.zeros((B, H, DK, DV), f32), jnp.moveaxis(kv, 1, 0))
    s_prev = jnp.moveaxis(s_prev, 0, 1)
    inter = jnp.einsum('bclhd,bchde,hl->bclhe', qc, s_prev, q_decay)
    o = head_layer_norm((intra + inter).reshape(B, S, H, DV), GN_EPS) * gn_w.astype(f32).reshape(H, DV)
    return (jax.nn.silu(g.astype(f32)) * o.reshape(B, S, H * DV)).astype(g.dtype)


def causal_depthwise_conv(x, w, b):
    K, C = w.shape
    y = lax.conv_general_dilated(x, w.astype(x.dtype)[:, None, :], window_strides=(1,),
                                 padding=[(K - 1, 0)], dimension_numbers=('NWC', 'WIO', 'NWC'),
                                 feature_group_count=C)
    return y + b.astype(x.dtype)


def ssd_chunked(x, a_dt, bm, cm):
    B, S, G, HG, P = x.shape
    N = bm.shape[-1]
    NC = S // CHUNK
    f32 = jnp.float32
    xc = x.reshape(B, NC, CHUNK, G, HG, P)
    ac = a_dt.reshape(B, NC, CHUNK, G, HG)
    bc = bm.reshape(B, NC, CHUNK, G, N)
    cc = cm.reshape(B, NC, CHUNK, G, N)
    a_cs = jnp.cumsum(ac, axis=2)
    causal = jnp.tril(jnp.ones((CHUNK, CHUNK), dtype=bool))[:, :, None, None]
    seg = a_cs[:, :, :, None] - a_cs[:, :, None, :]
    decay_lm = jnp.exp(jnp.where(causal, seg, -jnp.inf))
    cb = jnp.einsum('bclgn,bcmgn->bclmg', cc, bc)
    y_diag = jnp.einsum('bclmgh,bcmghp->bclghp', cb[..., None] * decay_lm, xc)
    decay_to_end = jnp.exp(a_cs[:, :, -1:] - a_cs)
    states = jnp.einsum('bclgn,bclgh,bclghp->bcghpn', bc, decay_to_end, xc)
    chunk_decay = jnp.exp(a_cs[:, :, -1])[..., None, None]

    def step(state, inp):
        dec, st = inp
        return state * dec + st, state

    _, s_prev = lax.scan(step, jnp.zeros((B, G, HG, P, N), f32),
                         (jnp.moveaxis(chunk_decay, 1, 0).astype(f32), jnp.moveaxis(states, 1, 0).astype(f32)))
    s_prev = jnp.moveaxis(s_prev, 0, 1)
    y_off = jnp.einsum('bclgn,bcghpn,bclgh->bclghp', cc, s_prev, jnp.exp(a_cs))
    return (y_diag + y_off).reshape(B, S, G, HG, P)


def mamba2_mixer(z, xbc, dt, conv_w, conv_b, dt_bias, a_log, d_skip, norm_w):
    B, S, _ = z.shape
    G, HG, P, N = SSM_GROUPS, SSM_HPG, SSM_HEADDIM, SSM_STATE
    f32 = jnp.float32
    xbc = jax.nn.silu(causal_depthwise_conv(xbc, conv_w, conv_b))
    xs, bm, cm = _split(xbc, (SSM_HEADS * P, G * N, G * N))
    x = xs.reshape(B, S, G, HG, P).astype(f32)
    bm = bm.reshape(B, S, G, N).astype(f32)
    cm = cm.reshape(B, S, G, N).astype(f32)
    dt = jax.nn.softplus(dt.astype(f32) + dt_bias.astype(f32)).reshape(B, S, G, HG)
    a = -jnp.exp(a_log.astype(f32)).reshape(G, HG)
    y = ssd_chunked(x * dt[..., None], dt * a, bm, cm)
    y = y + x * d_skip.astype(f32).reshape(G, HG, 1)
    y = y.reshape(B, S, G, HG * P) * jax.nn.silu(z.astype(f32)).reshape(B, S, G, HG * P)
    y = rms_norm(y, norm_w.reshape(G, HG * P))
    return y.reshape(B, S, G * HG * P).astype(z.dtype)


def block_causal_attention(q, k, v):
    B, S, H, D = q.shape
    Dv = v.shape[-1]
    NB = S // Q_BLOCK
    scale = D ** -0.5
    qb = q.reshape(B, NB, Q_BLOCK, H, D).transpose(1, 0, 3, 2, 4)
    kt = k.transpose(0, 2, 1, 3)
    vt = v.transpose(0, 2, 1, 3)
    key_chunk = jnp.arange(S) // CHUNK

    def one_block(args):
        q_blk, blk = args
        q_chunk = (blk * Q_BLOCK + jnp.arange(Q_BLOCK)) // CHUNK
        s = jnp.einsum('bhqd,bhkd->bhqk', q_blk, kt).astype(jnp.float32) * scale
        s = jnp.where(key_chunk[None, :] <= q_chunk[:, None], s, -jnp.inf)
        p = jax.nn.softmax(s, axis=-1).astype(vt.dtype)
        return jnp.einsum('bhqk,bhkd->bhqd', p, vt)

    o = lax.map(one_block, (qb, jnp.arange(NB)))
    return o.transpose(1, 0, 3, 2, 4).reshape(B, S, H * Dv)


def mla_mixer(cq, ckv, k_rope, cos, sin, q_a_norm_w, w_qb, kv_a_norm_w, w_kvb, q_norm_w, k_norm_w):
    B, S, _ = cq.shape
    H = MLA_HEADS
    q = (rms_norm(cq, q_a_norm_w) @ w_qb).reshape(B, S, H, MLA_QK)
    kv = (rms_norm(ckv, kv_a_norm_w) @ w_kvb).reshape(B, S, H, MLA_NOPE + MLA_V)
    k_nope, v = kv[..., :MLA_NOPE], kv[..., MLA_NOPE:]
    k = jnp.concatenate([k_nope, jnp.broadcast_to(k_rope[:, :, None, :], (B, S, H, MLA_ROPE)).astype(k_nope.dtype)], axis=-1)
    q = rms_norm(q, q_norm_w)
    k = rms_norm(k, k_norm_w)
    q = jnp.concatenate([q[..., :MLA_NOPE], apply_rope(q[..., MLA_NOPE:], cos, sin)], axis=-1)
    k = jnp.concatenate([k[..., :MLA_NOPE], apply_rope(k[..., MLA_NOPE:], cos, sin)], axis=-1)
    return block_causal_attention(q, k, v)


def rwkv7_mixer(p, mu, w0, w2, a0, a2, g2, k_k, k_a, r_k, ln_w, ln_b, v_first, v_res):
    B, S, _ = p.shape
    H, N = RWKV_HEADS, RWKV_HEAD
    f32 = jnp.float32
    p = p.astype(f32)
    p_prev = jnp.pad(p[:, :-1], ((0, 0), (1, 0), (0, 0)))
    pm = p + (p_prev - p) * mu.astype(f32)
    r, k, v, wl, al, gl = _split(pm, (RWKV_W, RWKV_W, RWKV_W, RWKV_W_LORA, RWKV_A_LORA, RWKV_G_LORA))
    w = -jax.nn.softplus(-(w0.astype(f32) + jnp.tanh(wl) @ w2.astype(f32))) - 0.5
    decay = jnp.exp(-jnp.exp(w))
    a = jax.nn.sigmoid(a0.astype(f32) + al @ a2.astype(f32))
    g = jax.nn.sigmoid(gl) @ g2.astype(f32)
    if v_res is None:
        v_first = v
    else:
        v0, v1, v2 = v_res
        v = v + (v_first - v) * jax.nn.sigmoid(v0.astype(f32) + (v @ v1.astype(f32)) @ v2.astype(f32))
    rh = r.reshape(B, S, H, N)
    kh = k.reshape(B, S, H, N)
    vh = v.reshape(B, S, H, N)
    ah = a.reshape(B, S, H, N)
    dh = decay.reshape(B, S, H, N)
    kk = kh * k_k.astype(f32).reshape(H, N)
    kk = kk / jnp.maximum(jnp.sqrt(jnp.sum(kk * kk, axis=-1, keepdims=True)), 1e-12)
    kh = kh * (1.0 + (ah - 1.0) * k_a.astype(f32).reshape(H, N))

    def step(state, inp):
        r_t, w_t, k_t, v_t, a_t, b_t = inp
        sa = jnp.einsum('bhij,bhj->bhi', state, a_t)
        state = state * w_t[:, :, None, :] + sa[..., None] * b_t[:, :, None, :] + v_t[..., None] * k_t[:, :, None, :]
        return state, jnp.einsum('bhij,bhj->bhi', state, r_t)

    xs = (jnp.moveaxis(rh, 1, 0), jnp.moveaxis(dh, 1, 0), jnp.moveaxis(kh, 1, 0),
          jnp.moveaxis(vh, 1, 0), jnp.moveaxis(-kk, 1, 0), jnp.moveaxis(kk * ah, 1, 0))
    _, ys = lax.scan(step, jnp.zeros((B, H, N, N), f32), xs)
    y = jnp.moveaxis(ys, 0, 1)
    y = head_layer_norm(y, RWKV_GN_EPS) * ln_w.astype(f32).reshape(H, N) + ln_b.astype(f32).reshape(H, N)
    y = y + jnp.sum(rh * kh * r_k.astype(f32), axis=-1, keepdims=True) * vh
    return y.reshape(B, S, H * N) * g, v_first


def setup_inputs(seed: int = 0) -> dict:
    key = jax.random.key(seed)
    k = jax.random.split(key, 40)
    f32 = jnp.float32

    def nrm(i, shape, scale):
        return jax.random.normal(k[i], shape, f32) * scale

    def gain(i, shape):
        return 1.0 + nrm(i, shape, 0.02)

    x = jax.random.normal(k[0], (BATCH, SEQ, D_MODEL), f32)
    offset = jax.random.randint(k[1], (BATCH, 1), 0, 4096, dtype=jnp.int32)
    positions = (offset + jnp.arange(SEQ, dtype=jnp.int32)[None, :]).astype(jnp.int32)
    u = jax.random.uniform(k[6], (DEPTH, SSM_HEADS), f32)
    dt0 = jnp.exp(u * (jnp.log(0.1) - jnp.log(0.001)) + jnp.log(0.001))
    return {
        "x": x,
        "positions": positions,
        "norm1_w": gain(2, (DEPTH, D_MODEL)),
        "w_in": nrm(3, (DEPTH, D_MODEL, N_IN), D_MODEL ** -0.5),
        "ret_gn_w": gain(4, (DEPTH, RET_HEADS * RET_DV)),
        "ssm_conv_w": nrm(5, (DEPTH, SSM_CONV, SSM_XBC), SSM_CONV ** -0.5),
        "ssm_conv_b": nrm(7, (DEPTH, SSM_XBC), 0.02),
        "ssm_dt_bias": dt0 + jnp.log(-jnp.expm1(-dt0)),
        "ssm_a_log": jnp.log(jax.random.uniform(k[8], (DEPTH, SSM_HEADS), f32, 1.0, 16.0)),
        "ssm_d": 1.0 + nrm(9, (DEPTH, SSM_HEADS), 0.1),
        "ssm_norm_w": gain(10, (DEPTH, SSM_HEADS * SSM_HEADDIM)),
        "mla_q_a_norm_w": gain(11, (DEPTH, MLA_Q_RANK)),
        "mla_w_qb": nrm(12, (DEPTH, MLA_Q_RANK, MLA_HEADS * MLA_QK), MLA_Q_RANK ** -0.5),
        "mla_kv_a_norm_w": gain(13, (DEPTH, MLA_KV_RANK)),
        "mla_w_kvb": nrm(14, (DEPTH, MLA_KV_RANK, MLA_HEADS * (MLA_NOPE + MLA_V)), MLA_KV_RANK ** -0.5),
        "mla_q_norm_w": gain(15, (DEPTH, MLA_QK)),
        "mla_k_norm_w": gain(16, (DEPTH, MLA_QK)),
        "rwkv_mu": jax.random.uniform(k[17], (DEPTH, RWKV_COLS), f32),
        "rwkv_w0": jax.random.uniform(k[18], (DEPTH, RWKV_W), f32, -6.0, -1.0),
        "rwkv_w2": nrm(19, (DEPTH, RWKV_W_LORA, RWKV_W), 0.1),
        "rwkv_a0": nrm(20, (DEPTH, RWKV_W), 0.1),
        "rwkv_a2": nrm(21, (DEPTH, RWKV_A_LORA, RWKV_W), RWKV_A_LORA ** -0.5),
        "rwkv_g2": nrm(22, (DEPTH, RWKV_G_LORA, RWKV_W), RWKV_G_LORA ** -0.5),
        "rwkv_v0": nrm(23, (DEPTH - 1, RWKV_W), 0.1),
        "rwkv_v1": nrm(24, (DEPTH - 1, RWKV_W, RWKV_V_LORA), RWKV_W ** -0.5),
        "rwkv_v2": nrm(25, (DEPTH - 1, RWKV_V_LORA, RWKV_W), 0.1),
        "rwkv_k_k": 0.85 + nrm(26, (DEPTH, RWKV_W), 0.02),
        "rwkv_k_a": gain(27, (DEPTH, RWKV_W)),
        "rwkv_r_k": nrm(28, (DEPTH, RWKV_HEADS, RWKV_HEAD), 0.1),
        "rwkv_ln_w": gain(29, (DEPTH, RWKV_W)),
        "rwkv_ln_b": nrm(30, (DEPTH, RWKV_W), 0.02),
        "w_branch": nrm(31, (DEPTH, N_BRANCH, BRANCH_W, D_MODEL), BRANCH_W ** -0.5),
        "w_out": nrm(32, (DEPTH, D_MODEL, D_MODEL), D_MODEL ** -0.5),
        "norm2_w": gain(33, (DEPTH, D_MODEL)),
        "ffn_w_gu": nrm(34, (DEPTH, D_MODEL, 2 * D_FF), D_MODEL ** -0.5),
        "ffn_w_down": nrm(35, (DEPTH, D_FF, D_MODEL), D_FF ** -0.5),
    }


def reference(x, positions, norm1_w, w_in, ret_gn_w, ssm_conv_w, ssm_conv_b, ssm_dt_bias, ssm_a_log, ssm_d,
              ssm_norm_w, mla_q_a_norm_w, mla_w_qb, mla_kv_a_norm_w, mla_w_kvb, mla_q_norm_w, mla_k_norm_w,
              rwkv_mu, rwkv_w0, rwkv_w2, rwkv_a0, rwkv_a2, rwkv_g2, rwkv_v0, rwkv_v1, rwkv_v2, rwkv_k_k, rwkv_k_a,
              rwkv_r_k, rwkv_ln_w, rwkv_ln_b, w_branch, w_out, norm2_w, ffn_w_gu, ffn_w_down):
    B, S, D = x.shape
    cos_ret, sin_ret = rope_tables(positions, RET_DK)
    cos_mla, sin_mla = rope_tables(positions, MLA_ROPE)
    v_first = None
    for l in range(DEPTH):
        h = rms_norm(x, norm1_w[l])
        pin = h @ w_in[l]
        (rq, rk, rv, rg, sz, sxbc, sdt, cq, ckv, krope, rw, gate_pre) = _split(pin, IN_SIZES)
        o_a = retention_mixer(rq, rk, rv, rg, cos_ret, sin_ret, ret_gn_w[l])
        o_b = mamba2_mixer(sz, sxbc, sdt, ssm_conv_w[l], ssm_conv_b[l], ssm_dt_bias[l], ssm_a_log[l],
                           ssm_d[l], ssm_norm_w[l])
        o_c = mla_mixer(cq, ckv, krope, cos_mla, sin_mla, mla_q_a_norm_w[l], mla_w_qb[l], mla_kv_a_norm_w[l],
                        mla_w_kvb[l], mla_q_norm_w[l], mla_k_norm_w[l])
        v_res = None if l == 0 else (rwkv_v0[l - 1], rwkv_v1[l - 1], rwkv_v2[l - 1])
        o_d, v_first = rwkv7_mixer(rw, rwkv_mu[l], rwkv_w0[l], rwkv_w2[l], rwkv_a0[l], rwkv_a2[l], rwkv_g2[l],
                                   rwkv_k_k[l], rwkv_k_a[l], rwkv_r_k[l], rwkv_ln_w[l], rwkv_ln_b[l], v_first, v_res)
        o = jnp.stack([o_a.astype(x.dtype), o_b.astype(x.dtype), o_c.astype(x.dtype), o_d.astype(x.dtype)], axis=2)
        br = jnp.einsum('bsnc,ncd->bsnd', o, w_branch[l])
        gates = jax.nn.sigmoid(gate_pre.reshape(B, S, N_BRANCH, D))
        merged = jnp.sum(gates * br, axis=2)
        x = x + (merged @ w_out[l]).astype(x.dtype)
        h2 = rms_norm(x, norm2_w[l])
        gt, up = jnp.split(h2 @ ffn_w_gu[l], 2, axis=-1)
        x = x + ((jax.nn.silu(gt) * up) @ ffn_w_down[l]).astype(x.dtype)
    return x
```

```python
import functools
import math

import numpy as np
import jax
import jax.numpy as jnp
from jax import lax
from jax.experimental import pallas as pl
from jax.experimental.pallas import tpu as pltpu

F32 = jnp.float32
BF16 = jnp.bfloat16

D_MODEL = 2048
DEPTH = 4
CHUNK = 64
N_BRANCH = 4
BRANCH_W = 512
RMS_EPS = 1e-6
GN_EPS = 1e-5
ROPE_THETA = 10000.0
RET_HEADS, RET_DK, RET_DV = 4, 128, 128
SSM_HEADS, SSM_HEADDIM, SSM_GROUPS, SSM_STATE, SSM_CONV = 8, 64, 2, 128, 4
SSM_XBC = SSM_HEADS * SSM_HEADDIM + 2 * SSM_GROUPS * SSM_STATE
MLA_HEADS, MLA_Q_RANK, MLA_KV_RANK, MLA_NOPE, MLA_ROPE, MLA_V = 4, 512, 256, 128, 64, 128
MLA_QK = MLA_NOPE + MLA_ROPE
RWKV_HEADS, RWKV_HEAD = 8, 64
RWKV_W = RWKV_HEADS * RWKV_HEAD
RWKV_W_LORA, RWKV_A_LORA, RWKV_V_LORA, RWKV_G_LORA = 64, 64, 32, 128
RWKV_GN_EPS = 64e-5
D_FF = 5632

LANES = 128
SUBLANES = 8
VMEM_LIMIT_BYTES = 56 * 1024 * 1024

_O_RQ, _O_RK, _O_RV, _O_RG = 0, 512, 1024, 1536
_O_SZ, _O_SXBC, _O_SDT = 2048, 2560, 3584
_O_CQ, _O_CKV, _O_KROPE = 3592, 4104, 4360
_O_RW = 4424
_O_GATE = 6216
N_IN = 14408
P_GATE = 0
P_RQ, P_RK, P_RV, P_RG = 8192, 8704, 9216, 9728
P_SXBC, P_SZ = 10240, 11264
P_CQ = 11776
P_WR, P_WK, P_WV = 12288, 12800, 13312
P_CKV = 13824
P_DT, P_KROPE, P_WL, P_AL, P_GL = 14080, 14208, 14336, 14464, 14592
N_PACK = 15360

ROW_BLOCK = 256
RWKV_L = 64


def _cparams(sem):
    return pltpu.CompilerParams(dimension_semantics=sem, vmem_limit_bytes=VMEM_LIMIT_BYTES)


def _dot(a, b):
    return jnp.dot(a, b, preferred_element_type=F32)


def _dot_nt(a, b):
    return lax.dot_general(a, b, (((1,), (1,)), ((), ())), preferred_element_type=F32)


def _dot_tn(a, b):
    return lax.dot_general(a, b, (((0,), (0,)), ((), ())), preferred_element_type=F32)


def _split_bf16(x):
    hi = x.astype(BF16)
    lo = (x - hi.astype(F32)).astype(BF16)
    return hi, lo


def _mm3(a, b, dot=_dot):
    ah, al = _split_bf16(a)
    bh, bl = _split_bf16(b)
    return dot(ah, bh) + dot(ah, bl) + dot(al, bh)


def _mm1(a, b, dot=_dot):
    return dot(a.astype(BF16), b.astype(BF16))


def _sigmoid(x):
    return 1.0 / (1.0 + jnp.exp(-x))


def _silu(x):
    return x * _sigmoid(x)


def _softplus(x):
    return jnp.maximum(x, 0.0) + jnp.log(1.0 + jnp.exp(-jnp.abs(x)))


def _shift_rows(x, carry8, s):
    xr = pltpu.roll(x, s, 0)
    pr = pltpu.roll(carry8, s, 0)
    row = lax.broadcasted_iota(jnp.int32, carry8.shape, 0)
    top = jnp.where(row < s, pr, xr[:SUBLANES])
    return jnp.concatenate([top, xr[SUBLANES:]], axis=0)


def _inproj_body(x_ref, nw_ref, w_ref, o_ref, h_ref):
    @pl.when(pl.program_id(1) == 0)
    def _():
        x = x_ref[...]
        ms = jnp.mean(x * x, axis=-1, keepdims=True)
        h_ref[...] = (x * lax.rsqrt(ms + RMS_EPS) * nw_ref[...]).astype(BF16)

    o_ref[...] = _dot(h_ref[...], w_ref[...])


def _inproj(x, norm_w, w_pack, layer, tm=1024, tn=1024):
    S, D = x.shape
    n = w_pack.shape[-1]
    return pl.pallas_call(
        _inproj_body,
        grid=(S // tm, n // tn),
        in_specs=[
            pl.BlockSpec((tm, D), lambda i, j: (i, 0)),
            pl.BlockSpec((None, 1, D), lambda i, j: (layer, 0, 0)),
            pl.BlockSpec((None, D, tn), lambda i, j: (layer, 0, j)),
        ],
        out_specs=pl.BlockSpec((tm, tn), lambda i, j: (i, j)),
        out_shape=jax.ShapeDtypeStruct((S, n), F32),
        scratch_shapes=[pltpu.VMEM((tm, D), BF16)],
        compiler_params=_cparams(("arbitrary", "arbitrary")),
        name="inproj",
    )(x, norm_w, w_pack)


def _ret_body(q_ref, k_ref, v_ref, g_ref, cos_ref, sin_ref, m_ref, qd_ref, kd_ref, gnw_ref,
              o_ref, s_ref, *, block_decay):
    @pl.when(pl.program_id(0) == 0)
    def _():
        s_ref[...] = jnp.zeros_like(s_ref)

    cos = cos_ref[...]
    sin = sin_ref[...]
    for h in range(RET_HEADS):
        sl = slice(h * RET_DK, (h + 1) * RET_DK)
        q = q_ref[:, sl]
        k = k_ref[:, sl]
        vb = v_ref[:, sl].astype(BF16)
        q = q * cos + pltpu.roll(q, RET_DK // 2, 1) * sin
        k = (k * cos + pltpu.roll(k, RET_DK // 2, 1) * sin) * (RET_DK ** -0.5)
        sc = _dot_nt(q.astype(BF16), k.astype(BF16)) * m_ref[h]
        st = s_ref[h]
        o = _dot(sc.astype(BF16), vb) + _dot((q * qd_ref[h]).astype(BF16), st.astype(BF16))
        s_ref[h] = block_decay[h] * st + _dot_tn((k * kd_ref[h]).astype(BF16), vb)
        mu = jnp.mean(o, axis=-1, keepdims=True)
        oc = o - mu
        var = jnp.mean(oc * oc, axis=-1, keepdims=True)
        o = oc * lax.rsqrt(var + GN_EPS) * gnw_ref[:, sl]
        o_ref[:, sl] = _silu(g_ref[:, sl]) * o


def _ret_tables(tb):
    lg = np.log1p(-np.exp2(-5.0 - np.arange(RET_HEADS, dtype=np.float64)))
    pos = np.arange(tb, dtype=np.float64)
    dist = np.abs(pos[:, None] - pos[None, :])
    visible = (pos[None, :] // CHUNK) <= (pos[:, None] // CHUNK)
    mask = np.where(visible[None], np.exp(lg[:, None, None] * dist[None]), 0.0)
    qd = np.exp(lg[:, None] * (pos[None, :] + 1.0))[:, :, None]
    kd = np.exp(lg[:, None] * (tb - 1.0 - pos[None, :]))[:, :, None]
    bd = tuple(float(v) for v in np.exp(lg * tb))
    return (jnp.asarray(mask, F32), jnp.asarray(qd, F32), jnp.asarray(kd, F32), bd)


def _retention(pin, cos2, sin2, gn_w, layer, tb=ROW_BLOCK):
    S = pin.shape[0]
    mask, qd, kd, bd = _ret_tables(tb)
    w = RET_HEADS * RET_DK
    col = lambda off: (lambda i: (i, off // w))
    full3 = lambda i: (0, 0, 0)
    return pl.pallas_call(
        functools.partial(_ret_body, block_decay=bd),
        grid=(S // tb,),
        in_specs=[
            pl.BlockSpec((tb, w), col(P_RQ)),
            pl.BlockSpec((tb, w), col(P_RK)),
            pl.BlockSpec((tb, w), col(P_RV)),
            pl.BlockSpec((tb, w), col(P_RG)),
            pl.BlockSpec((tb, RET_DK), lambda i: (i, 0)),
            pl.BlockSpec((tb, RET_DK), lambda i: (i, 0)),
            pl.BlockSpec((RET_HEADS, tb, tb), full3),
            pl.BlockSpec((RET_HEADS, tb, 1), full3),
            pl.BlockSpec((RET_HEADS, tb, 1), full3),
            pl.BlockSpec((None, 1, w), lambda i: (layer, 0, 0)),
        ],
        out_specs=pl.BlockSpec((tb, w), lambda i: (i, 0)),
        out_shape=jax.ShapeDtypeStruct((S, w), F32),
        scratch_shapes=[pltpu.VMEM((RET_HEADS, RET_DK, RET_DV), F32)],
        compiler_params=_cparams(("arbitrary",)),
        name="retention",
    )(pin, pin, pin, pin, cos2, sin2, mask, qd, kd, gn_w)


def _ssd_body(xbc_ref, z_ref, dt_ref, adt_ref, adtT_ref, cw_ref, cb_ref, d_ref, nw_ref,
              o_ref, carry_ref, st_ref):
    tb = xbc_ref.shape[0]
    P, N = SSM_HEADDIM, SSM_STATE
    nx = SSM_HEADS * P

    @pl.when(pl.program_id(0) == 0)
    def _():
        carry_ref[...] = jnp.zeros_like(carry_ref)
        st_ref[...] = jnp.zeros_like(st_ref)

    x = xbc_ref[...]
    carry = carry_ref[...]
    acc = x * cw_ref[SSM_CONV - 1:SSM_CONV, :] + cb_ref[...]
    for s in range(1, SSM_CONV):
        acc = acc + _shift_rows(x, carry, s) * cw_ref[SSM_CONV - 1 - s:SSM_CONV - s, :]
    carry_ref[...] = x[tb - SUBLANES:, :]
    xbc = _silu(acc)

    row = lax.broadcasted_iota(jnp.int32, (tb, tb), 0)
    colm = lax.broadcasted_iota(jnp.int32, (tb, tb), 1)
    causal = colm <= row
    tril = causal.astype(F32)
    adt = adt_ref[...]
    a_col = jnp.dot(tril, adt, preferred_element_type=F32, precision=lax.Precision.HIGHEST)
    a_row = lax.dot_general(adtT_ref[...], tril, (((1,), (1,)), ((), ())),
                            preferred_element_type=F32, precision=lax.Precision.HIGHEST)
    a_end = a_col[tb - 1:tb, :]
    dt = dt_ref[...]
    lane = lax.broadcasted_iota(jnp.int32, (tb, 2 * P), 1)
    first = lane < P

    for g in range(SSM_GROUPS):
        bm = xbc[:, nx + g * N: nx + (g + 1) * N]
        cm = xbc[:, nx + SSM_GROUPS * N + g * N: nx + SSM_GROUPS * N + (g + 1) * N]
        bmb = bm.astype(BF16)
        cb = _dot_nt(cm.astype(BF16), bmb)
        ys = []
        for pr in range(SSM_HEADS // SSM_GROUPS // 2):
            h0 = g * (SSM_HEADS // SSM_GROUPS) + 2 * pr
            pidx = h0 // 2
            xp = xbc[:, h0 * P:(h0 + 2) * P]
            dtp = jnp.where(first, dt[:, h0:h0 + 1], dt[:, h0 + 1:h0 + 2])
            xdt = (xp * dtp).astype(BF16)
            st = st_ref[pidx]
            stb = st.astype(BF16)
            y2 = []
            snew = []
            for e in range(2):
                h = h0 + e
                ac = a_col[:, h:h + 1]
                seg = ac - a_row[h:h + 1, :]
                dec = jnp.exp(jnp.where(causal, seg, -jnp.inf))
                y = _dot((cb * dec).astype(BF16), xdt)
                y = y + _dot((cm * jnp.exp(ac)).astype(BF16), stb)
                y2.append(y)
                wend = jnp.exp(a_end[:, h:h + 1] - ac)
                snew.append(jnp.exp(a_end[:, h:h + 1]) * st + _dot_tn((bm * wend).astype(BF16), xdt))
            lane_s = lax.broadcasted_iota(jnp.int32, (N, 2 * P), 1)
            st_ref[pidx] = jnp.where(lane_s < P, snew[0], snew[1])
            dsk = jnp.where(first, d_ref[:, h0:h0 + 1], d_ref[:, h0 + 1:h0 + 2])
            ys.append(jnp.where(first, y2[0], y2[1]) + xp * dsk)
        yg = jnp.concatenate(ys, axis=1)
        gsl = slice(g * 4 * P, (g + 1) * 4 * P)
        yg = yg * _silu(z_ref[:, gsl])
        ms = jnp.mean(yg * yg, axis=-1, keepdims=True)
        o_ref[:, gsl] = yg * lax.rsqrt(ms + RMS_EPS) * nw_ref[:, gsl]


def _ssd(pin, dt, adt, adtT, conv_w, conv_b, d_skip, norm_w, layer, tb=ROW_BLOCK):
    S = pin.shape[0]
    nx = SSM_HEADS * SSM_HEADDIM
    H = SSM_HEADS
    return pl.pallas_call(
        _ssd_body,
        grid=(S // tb,),
        in_specs=[
            pl.BlockSpec((tb, SSM_XBC), lambda i: (i, P_SXBC // SSM_XBC)),
            pl.BlockSpec((tb, nx), lambda i: (i, P_SZ // nx)),
            pl.BlockSpec((tb, LANES), lambda i: (i, 0)),
            pl.BlockSpec((tb, LANES), lambda i: (i, 0)),
            pl.BlockSpec((H, tb), lambda i: (0, i)),
            pl.BlockSpec((None, SSM_CONV, SSM_XBC), lambda i: (layer, 0, 0)),
            pl.BlockSpec((None, 1, SSM_XBC), lambda i: (layer, 0, 0)),
            pl.BlockSpec((None, 1, H), lambda i: (layer, 0, 0)),
            pl.BlockSpec((None, 1, nx), lambda i: (layer, 0, 0)),
        ],
        out_specs=pl.BlockSpec((tb, nx), lambda i: (i, 0)),
        out_shape=jax.ShapeDtypeStruct((S, nx), F32),
        scratch_shapes=[pltpu.VMEM((SUBLANES, SSM_XBC), F32),
                        pltpu.VMEM((H // 2, SSM_STATE, 2 * SSM_HEADDIM), F32)],
        compiler_params=_cparams(("arbitrary",)),
        name="ssd",
    )(pin, pin, dt, adt, adtT, conv_w, conv_b, d_skip, norm_w)


MLA_PAD = 256


def _rope_pad(x, cosp, sinp):
    half = MLA_ROPE // 2
    return x * cosp + (pltpu.roll(x, half, 1) + pltpu.roll(x, LANES - half, 1)) * sinp


def _mla_prep_body(cq_ref, ckv_ref, kr_ref, qaw_ref, wqb_ref, kvaw_ref, wkvb_ref, qnw_ref,
                   knw_n_ref, knw_r_ref, cos_ref, sin_ref, q_out, k_out, v_out):
    cosp = cos_ref[...]
    sinp = sin_ref[...]
    cq = cq_ref[...]
    ms = jnp.mean(cq * cq, axis=-1, keepdims=True)
    cqn = (cq * lax.rsqrt(ms + RMS_EPS) * qaw_ref[...]).astype(BF16)
    q = _dot(cqn, wqb_ref[...])
    ckv = ckv_ref[...]
    ms = jnp.mean(ckv * ckv, axis=-1, keepdims=True)
    ckvn = (ckv * lax.rsqrt(ms + RMS_EPS) * kvaw_ref[...]).astype(BF16)
    kv = _dot(ckvn, wkvb_ref[...])
    kr = kr_ref[...]
    ssr = jnp.sum(kr * kr, axis=-1, keepdims=True)
    scale = MLA_QK ** -0.5
    for h in range(MLA_HEADS):
        qh = q[:, h * MLA_PAD:(h + 1) * MLA_PAD]
        msq = jnp.sum(qh * qh, axis=-1, keepdims=True) * (1.0 / MLA_QK)
        qh = qh * lax.rsqrt(msq + RMS_EPS) * qnw_ref[...]
        qr = _rope_pad(qh[:, MLA_NOPE:], cosp, sinp)
        q_out[h] = (jnp.concatenate([qh[:, :MLA_NOPE], qr], axis=1) * scale).astype(BF16)
        kn = kv[:, h * MLA_PAD: h * MLA_PAD + MLA_NOPE]
        vv = kv[:, h * MLA_PAD + MLA_NOPE:(h + 1) * MLA_PAD]
        msk = (jnp.sum(kn * kn, axis=-1, keepdims=True) + ssr) * (1.0 / MLA_QK)
        rinv = lax.rsqrt(msk + RMS_EPS)
        krh = _rope_pad(kr * rinv * knw_r_ref[...], cosp, sinp)
        k_out[h] = jnp.concatenate([kn * rinv * knw_n_ref[...], krh], axis=1).astype(BF16)
        v_out[h] = vv.astype(BF16)


def _mla_prep(pin, qaw, wqb, kvaw, wkvb, qnw, knw_n, knw_r, cosp, sinp, layer, tb=ROW_BLOCK):
    S = pin.shape[0]
    H = MLA_HEADS
    lsel = lambda i: (layer, 0, 0)
    return pl.pallas_call(
        _mla_prep_body,
        grid=(S // tb,),
        in_specs=[
            pl.BlockSpec((tb, MLA_Q_RANK), lambda i: (i, P_CQ // MLA_Q_RANK)),
            pl.BlockSpec((tb, MLA_KV_RANK), lambda i: (i, P_CKV // MLA_KV_RANK)),
            pl.BlockSpec((tb, LANES), lambda i: (i, P_KROPE // LANES)),
            pl.BlockSpec((None, 1, MLA_Q_RANK), lsel),
            pl.BlockSpec((None, MLA_Q_RANK, H * MLA_PAD), lsel),
            pl.BlockSpec((None, 1, MLA_KV_RANK), lsel),
            pl.BlockSpec((None, MLA_KV_RANK, H * MLA_PAD), lsel),
            pl.BlockSpec((None, 1, MLA_PAD), lsel),
            pl.BlockSpec((None, 1, MLA_NOPE), lsel),
            pl.BlockSpec((None, 1, LANES), lsel),
            pl.BlockSpec((tb, LANES), lambda i: (i, 0)),
            pl.BlockSpec((tb, LANES), lambda i: (i, 0)),
        ],
        out_specs=[
            pl.BlockSpec((H, tb, MLA_PAD), lambda i: (0, i, 0)),
            pl.BlockSpec((H, tb, MLA_PAD), lambda i: (0, i, 0)),
            pl.BlockSpec((H, tb, MLA_V), lambda i: (0, i, 0)),
        ],
        out_shape=[
            jax.ShapeDtypeStruct((H, S, MLA_PAD), BF16),
            jax.ShapeDtypeStruct((H, S, MLA_PAD), BF16),
            jax.ShapeDtypeStruct((H, S, MLA_V), BF16),
        ],
        compiler_params=_cparams(("arbitrary",)),
        name="mla_prep",
    )(pin, pin, pin, qaw, wqb, kvaw, wkvb, qnw, knw_n, knw_r, cosp, sinp)


def _flash_body(q_ref, k_ref, v_ref, o_ref, m_ref, l_ref, acc_ref):
    i = pl.program_id(1)
    j = pl.program_id(2)
    tq = q_ref.shape[0]
    tk = k_ref.shape[0]

    @pl.when(j == 0)
    def _():
        m_ref[...] = jnp.full_like(m_ref, -jnp.inf)
        l_ref[...] = jnp.zeros_like(l_ref)
        acc_ref[...] = jnp.zeros_like(acc_ref)

    @pl.when(j <= i)
    def _():
        s = _dot_nt(q_ref[...], k_ref[...])
        rq = lax.broadcasted_iota(jnp.int32, (tq, tk), 0) // CHUNK
        ck = lax.broadcasted_iota(jnp.int32, (tq, tk), 1) // CHUNK
        s = jnp.where((ck <= rq) | (j < i), s, -jnp.inf)
        m_prev = m_ref[...]
        m_new = jnp.maximum(m_prev, jnp.max(s, axis=-1, keepdims=True))
        alpha = jnp.exp(m_prev - m_new)
        p = jnp.exp(s - m_new)
        l_ref[...] = alpha * l_ref[...] + jnp.sum(p, axis=-1, keepdims=True)
        acc_ref[...] = alpha * acc_ref[...] + _dot(p.astype(BF16), v_ref[...])
        m_ref[...] = m_new

    @pl.when(j == i)
    def _():
        o_ref[...] = acc_ref[...] / l_ref[...]


def _flash(q, k, v, tq=512):
    H, S, _ = q.shape
    nb = S // tq
    return pl.pallas_call(
        _flash_body,
        grid=(H, nb, nb),
        in_specs=[
            pl.BlockSpec((None, tq, MLA_PAD), lambda h, i, j: (h, i, 0)),
            pl.BlockSpec((None, tq, MLA_PAD), lambda h, i, j: (h, jnp.minimum(i, j), 0)),
            pl.BlockSpec((None, tq, MLA_V), lambda h, i, j: (h, jnp.minimum(i, j), 0)),
        ],
        out_specs=pl.BlockSpec((tq, MLA_V), lambda h, i, j: (i, h)),
        out_shape=jax.ShapeDtypeStruct((S, H * MLA_V), F32),
        scratch_shapes=[pltpu.VMEM((tq, 1), F32), pltpu.VMEM((tq, 1), F32),
                        pltpu.VMEM((tq, MLA_V), F32)],
        compiler_params=_cparams(("arbitrary", "arbitrary", "arbitrary")),
        name="mla_flash",
    )(q, k, v)


def _seg_sum(x, ones_bd):
    return _mm3(x, ones_bd)


def _rwkv_prep_body(*refs, mix_v):
    if mix_v:
        (r_ref, k_ref, v_ref, wl_ref, al_ref, gl_ref, mur, muk, muv, muwl, mual, mugl, w0, a0, w2, a2, g2,
         kk_ref, ka_ref, vf_ref, v0, v1, v2,
         ro, lwo, ko, vo, ao, bo, go, c_r, c_k, c_v, c_wl, c_al, c_gl) = refs
    else:
        (r_ref, k_ref, v_ref, wl_ref, al_ref, gl_ref, mur, muk, muv, muwl, mual, mugl, w0, a0, w2, a2, g2,
         kk_ref, ka_ref,
         ro, lwo, ko, vo, ao, bo, go, c_r, c_k, c_v, c_wl, c_al, c_gl) = refs
    tb = r_ref.shape[0]
    carries = (c_r, c_k, c_v, c_wl, c_al, c_gl)

    @pl.when(pl.program_id(0) == 0)
    def _():
        for c in carries:
            c[...] = jnp.zeros_like(c)

    def mixed(x_ref, c_ref, mu_ref):
        x = x_ref[...]
        prev = _shift_rows(x, c_ref[...], 1)
        c_ref[...] = x[tb - SUBLANES:, :]
        return x + (prev - x) * mu_ref[...]

    r = mixed(r_ref, c_r, mur)
    k = mixed(k_ref, c_k, muk)
    v = mixed(v_ref, c_v, muv)
    wl = mixed(wl_ref, c_wl, muwl)
    al = mixed(al_ref, c_al, mual)
    gl = mixed(gl_ref, c_gl, mugl)

    w_raw = w0[...] + _mm3(jnp.tanh(wl), w2[...])
    lwo[...] = -jnp.exp(-_softplus(-w_raw) - 0.5)
    a_sig = _sigmoid(a0[...] + _mm3(al, a2[...]))
    go[...] = _mm1(_sigmoid(gl), g2[...])
    if mix_v:
        lora = _mm1(_mm1(v, v1[...]), v2[...])
        v = v + (vf_ref[...] - v) * _sigmoid(v0[...] + lora)
    row = lax.broadcasted_iota(jnp.int32, (LANES, LANES), 0) // RWKV_HEAD
    colm = lax.broadcasted_iota(jnp.int32, (LANES, LANES), 1) // RWKV_HEAD
    ones_bd = (row == colm).astype(F32)
    kk = k * kk_ref[...]
    parts = []
    for p in range(RWKV_W // LANES):
        sl = slice(p * LANES, (p + 1) * LANES)
        kp = kk[:, sl]
        n2 = _seg_sum(kp * kp, ones_bd)
        parts.append(kp / jnp.maximum(jnp.sqrt(n2), 1e-12))
    kk = jnp.concatenate(parts, axis=1)
    ro[...] = r
    ko[...] = k * (1.0 + (a_sig - 1.0) * ka_ref[...])
    vo[...] = v
    ao[...] = -kk
    bo[...] = kk * a_sig


def _rwkv_prep(pin, vec, mats, vfirst, vmix, layer, tb=ROW_BLOCK):
    S = pin.shape[0]
    W = RWKV_W
    lsel = lambda i: (layer, 0, 0)
    lsel1 = lambda i: (layer - 1, 0, 0)
    in_specs = [
        pl.BlockSpec((tb, W), lambda i: (i, P_WR // W)),
        pl.BlockSpec((tb, W), lambda i: (i, P_WK // W)),
        pl.BlockSpec((tb, W), lambda i: (i, P_WV // W)),
        pl.BlockSpec((tb, LANES), lambda i: (i, P_WL // LANES)),
        pl.BlockSpec((tb, LANES), lambda i: (i, P_AL // LANES)),
        pl.BlockSpec((tb, LANES), lambda i: (i, P_GL // LANES)),
    ]
    args = [pin] * 6
    for name, width in (("mu_r", W), ("mu_k", W), ("mu_v", W), ("mu_wl", LANES), ("mu_al", LANES),
                        ("mu_gl", LANES), ("w0", W), ("a0", W)):
        in_specs.append(pl.BlockSpec((None, 1, width), lsel))
        args.append(vec[name])
    for name in ("w2", "a2", "g2"):
        in_specs.append(pl.BlockSpec((None, LANES, W), lsel))
        args.append(mats[name])
    for name in ("k_k", "k_a"):
        in_specs.append(pl.BlockSpec((None, 1, W), lsel))
        args.append(vec[name])
    mix_v = vmix is not None
    if mix_v:
        v0, v1, v2 = vmix
        in_specs += [pl.BlockSpec((tb, W), lambda i: (i, 0)),
                     pl.BlockSpec((None, 1, W), lsel1),
                     pl.BlockSpec((None, W, LANES), lsel1),
                     pl.BlockSpec((None, LANES, W), lsel1)]
        args += [vfirst, v0, v1, v2]
    out_spec = pl.BlockSpec((tb, W), lambda i: (i, 0))
    outs = pl.pallas_call(
        functools.partial(_rwkv_prep_body, mix_v=mix_v),
        grid=(S // tb,),
        in_specs=in_specs,
        out_specs=[out_spec] * 7,
        out_shape=[jax.ShapeDtypeStruct((S, W), F32)] * 7,
        scratch_shapes=[pltpu.VMEM((SUBLANES, W), F32)] * 3 + [pltpu.VMEM((SUBLANES, LANES), F32)] * 3,
        compiler_params=_cparams(("arbitrary",)),
        name="rwkv_prep",
    )(*args)
    return outs


def _rwkv_core_body(r_ref, lw_ref, k_ref, v_ref, a_ref, b_ref, g_ref, lnw_ref, lnb_ref, rk_ref,
                    o_ref, s_ref):
    L = RWKV_L
    N = RWKV_HEAD
    L2 = 2 * L

    @pl.when(pl.program_id(0) == 0)
    def _():
        s_ref[...] = jnp.zeros_like(s_ref)

    row = lax.broadcasted_iota(jnp.int32, (L, L), 0)
    colm = lax.broadcasted_iota(jnp.int32, (L, L), 1)
    tril = (colm <= row).astype(F32)
    cum_all = jnp.dot(tril, lw_ref[...], preferred_element_type=F32, precision=lax.Precision.HIGHEST)

    lane = lax.broadcasted_iota(jnp.int32, (L, LANES), 1)
    m0 = lane < N
    r2 = lax.broadcasted_iota(jnp.int32, (L2, L2), 0)
    c2 = lax.broadcasted_iota(jnp.int32, (L2, L2), 1)
    same = (r2 // L) == (c2 // L)
    strict = same & (c2 < r2)
    incl = same & (c2 <= r2)
    eye = (r2 == c2).astype(F32)
    ones_bd = ((r2 // N) == (c2 // N)).astype(F32)

    def stack(x):
        return jnp.concatenate([jnp.where(m0, x, 0.0), jnp.where(m0, 0.0, x)], axis=0)

    for p in range(RWKV_W // LANES):
        sl = slice(p * LANES, (p + 1) * LANES)
        lw = lw_ref[:, sl]
        cum = cum_all[:, sl]
        cum_end = cum[L - 1:L, :]
        r = r_ref[:, sl]
        k = k_ref[:, sl]
        v = v_ref[:, sl]
        a = a_ref[:, sl]
        b = b_ref[:, sl]
        e_neg = jnp.exp(-cum)
        e_end = jnp.exp(cum_end - cum)
        xa = stack(a * jnp.exp(cum - lw))
        xr = stack(r * jnp.exp(cum))
        xb = stack(b * e_neg)
        xk = stack(k * e_neg)
        xbh = stack(b * e_end)
        xkh = stack(k * e_end)
        vs = stack(v)

        pmat = _mm3(jnp.concatenate([xa, xr], axis=0), jnp.concatenate([xb, xk], axis=0), _dot_nt)
        a_ab = jnp.where(strict, pmat[:L2, :L2], 0.0)
        a_ak = jnp.where(strict, pmat[:L2, L2:], 0.0)
        a_rb = jnp.where(incl, pmat[L2:, :L2], 0.0)
        a_rk = jnp.where(incl, pmat[L2:, L2:], 0.0)

        t = eye + a_ab
        pw = a_ab
        for _ in range(int(math.log2(L)) - 1):
            pw = _mm3(pw, pw)
            t = t + _mm3(t, pw)

        w1 = _mm3(a_ak, vs)
        mu = _mm3(t, jnp.concatenate([xa, w1], axis=1))
        ry = _mm3(a_rb, mu)
        m_r = xr + ry[:, :LANES]
        y1 = ry[:, LANES:] + _mm3(a_rk, vs)
        gh = _mm3(xbh, mu, _dot_tn)
        g_mat = eye * jnp.exp(cum_end) + gh[:, :LANES]
        h_mat = gh[:, LANES:] + _mm3(xkh, vs, _dot_tn)

        st = s_ref[p]
        yst = _mm3(m_r, st) + y1
        s_ref[p] = _mm3(g_mat, st) + h_mat
        y = yst[:L] + yst[L:]

        mean = _seg_sum(y, ones_bd) * (1.0 / N)
        yc = y - mean
        var = _seg_sum(yc * yc, ones_bd) * (1.0 / N)
        yn = yc * lax.rsqrt(var + RWKV_GN_EPS) * lnw_ref[:, sl] + lnb_ref[:, sl]
        yn = yn + _seg_sum(r * k * rk_ref[:, sl], ones_bd) * v
        o_ref[:, sl] = yn * g_ref[:, sl]


def _rwkv_core(r, lw, k, v, a, b, g, ln_w, ln_b, r_k, layer):
    S, W = r.shape
    L = RWKV_L
    blk = pl.BlockSpec((L, W), lambda i: (i, 0))
    lsel = pl.BlockSpec((None, 1, W), lambda i: (layer, 0, 0))
    return pl.pallas_call(
        _rwkv_core_body,
        grid=(S // L,),
        in_specs=[blk] * 7 + [lsel] * 3,
        out_specs=blk,
        out_shape=jax.ShapeDtypeStruct((S, W), F32),
        scratch_shapes=[pltpu.VMEM((W // LANES, LANES, LANES), F32)],
        compiler_params=_cparams(("arbitrary",)),
        name="rwkv_core",
    )(r, lw, k, v, a, b, g, ln_w, ln_b, r_k)


def _merge_body(oa_ref, ob_ref, oc_ref, od_ref, gate_ref, wb_ref, wo_ref, x_ref, out_ref, acc_ref):
    n = pl.program_id(1)
    o_refs = (oa_ref, ob_ref, oc_ref, od_ref)
    for idx in range(N_BRANCH):
        @pl.when(n == idx)
        def _(idx=idx):
            br = _dot(o_refs[idx][...].astype(BF16), wb_ref[...])
            contrib = _sigmoid(gate_ref[...]) * br
            if idx == 0:
                acc_ref[...] = contrib
            else:
                acc_ref[...] = acc_ref[...] + contrib

    @pl.when(n == N_BRANCH - 1)
    def _():
        out_ref[...] = x_ref[...] + _dot(acc_ref[...].astype(BF16), wo_ref[...])


def _merge(o_a, o_b, o_c, o_d, pin, w_branch, w_out, x, layer, tm=256):
    S, D = x.shape
    ospec = pl.BlockSpec((tm, BRANCH_W), lambda i, n: (i, 0))
    return pl.pallas_call(
        _merge_body,
        grid=(S // tm, N_BRANCH),
        in_specs=[
            ospec, ospec, ospec, ospec,
            pl.BlockSpec((tm, D), lambda i, n: (i, n)),
            pl.BlockSpec((None, None, BRANCH_W, D), lambda i, n: (layer, n, 0, 0)),
            pl.BlockSpec((None, D, D), lambda i, n: (layer, 0, 0)),
            pl.BlockSpec((tm, D), lambda i, n: (i, 0)),
        ],
        out_specs=pl.BlockSpec((tm, D), lambda i, n: (i, 0)),
        out_shape=jax.ShapeDtypeStruct((S, D), F32),
        scratch_shapes=[pltpu.VMEM((tm, D), F32)],
        compiler_params=_cparams(("arbitrary", "arbitrary")),
        name="merge",
    )(o_a, o_b, o_c, o_d, pin, w_branch, w_out, x)


def _ffn_body(x_ref, nw_ref, wg_ref, wu_ref, wd_ref, out_ref, h_ref, acc_ref):
    f = pl.program_id(1)

    @pl.when(f == 0)
    def _():
        x = x_ref[...]
        ms = jnp.mean(x * x, axis=-1, keepdims=True)
        h_ref[...] = (x * lax.rsqrt(ms + RMS_EPS) * nw_ref[...]).astype(BF16)

    h = h_ref[...]
    act = (_silu(_dot(h, wg_ref[...])) * _dot(h, wu_ref[...])).astype(BF16)
    part = _dot(act, wd_ref[...])

    @pl.when(f == 0)
    def _():
        acc_ref[...] = part

    @pl.when(f > 0)
    def _():
        acc_ref[...] = acc_ref[...] + part

    @pl.when(f == pl.num_programs(1) - 1)
    def _():
        out_ref[...] = x_ref[...] + acc_ref[...]


def _ffn(x, norm_w, w_gu, w_down, layer, tm=512, tf=512):
    S, D = x.shape
    nf = D_FF // tf
    return pl.pallas_call(
        _ffn_body,
        grid=(S // tm, nf),
        in_specs=[
            pl.BlockSpec((tm, D), lambda i, f: (i, 0)),
            pl.BlockSpec((None, 1, D), lambda i, f: (layer, 0, 0)),
            pl.BlockSpec((None, D, tf), lambda i, f: (layer, 0, f)),
            pl.BlockSpec((None, D, tf), lambda i, f: (layer, 0, f + nf)),
            pl.BlockSpec((None, tf, D), lambda i, f: (layer, f, 0)),
        ],
        out_specs=pl.BlockSpec((tm, D), lambda i, f: (i, 0)),
        out_shape=jax.ShapeDtypeStruct((S, D), F32),
        scratch_shapes=[pltpu.VMEM((tm, D), BF16), pltpu.VMEM((tm, D), F32)],
        compiler_params=_cparams(("arbitrary", "arbitrary")),
        name="ffn",
    )(x, norm_w, w_gu, w_gu, w_down)


def _pad_last(a, width):
    pad = width - a.shape[-1]
    if pad == 0:
        return a
    return jnp.pad(a, [(0, 0)] * (a.ndim - 1) + [(0, pad)])


def _pad_axis(a, axis, width):
    pad = width - a.shape[axis]
    if pad == 0:
        return a
    cfg = [(0, 0)] * a.ndim
    cfg[axis] = (0, pad)
    return jnp.pad(a, cfg)


def _pack_w_in(w_in):
    c = lambda off, n: w_in[..., off:off + n]
    r0 = _O_RW
    pieces = [
        c(_O_GATE, N_BRANCH * D_MODEL),
        c(_O_RQ, 512), c(_O_RK, 512), c(_O_RV, 512), c(_O_RG, 512),
        c(_O_SXBC, SSM_XBC), c(_O_SZ, 512),
        c(_O_CQ, MLA_Q_RANK),
        c(r0, RWKV_W), c(r0 + RWKV_W, RWKV_W), c(r0 + 2 * RWKV_W, RWKV_W),
        c(_O_CKV, MLA_KV_RANK),
        _pad_last(c(_O_SDT, SSM_HEADS), LANES),
        _pad_last(c(_O_KROPE, MLA_ROPE), LANES),
        _pad_last(c(r0 + 3 * RWKV_W, RWKV_W_LORA), LANES),
        _pad_last(c(r0 + 3 * RWKV_W + RWKV_W_LORA, RWKV_A_LORA), LANES),
        c(r0 + 3 * RWKV_W + RWKV_W_LORA + RWKV_A_LORA, RWKV_G_LORA),
    ]
    packed = jnp.concatenate([p.astype(BF16) for p in pieces], axis=-1)
    return _pad_last(packed, N_PACK)


def _rope_tables(positions, dim):
    inv = 1.0 / (ROPE_THETA ** (jnp.arange(0, dim, 2, dtype=F32) / dim))
    ang = positions.astype(F32)[:, None] * inv
    return jnp.cos(ang), jnp.sin(ang)


def kernel(x, positions, norm1_w, w_in, ret_gn_w, ssm_conv_w, ssm_conv_b, ssm_dt_bias, ssm_a_log, ssm_d, ssm_norm_w, mla_q_a_norm_w, mla_w_qb, mla_kv_a_norm_w, mla_w_kvb, mla_q_norm_w, mla_k_norm_w, rwkv_mu, rwkv_w0, rwkv_w2, rwkv_a0, rwkv_a2, rwkv_g2, rwkv_v0, rwkv_v1, rwkv_v2, rwkv_k_k, rwkv_k_a, rwkv_r_k, rwkv_ln_w, rwkv_ln_b, w_branch, w_out, norm2_w, ffn_w_gu, ffn_w_down):
    B, S, D = x.shape
    assert B == 1 and D == D_MODEL and S % 1024 == 0
    xs = x[0]
    pos = positions[0]

    c_r, s_r = _rope_tables(pos, RET_DK)
    cos_ret = jnp.concatenate([c_r, c_r], axis=-1)
    sin_ret = jnp.concatenate([-s_r, s_r], axis=-1)
    c_m, s_m = _rope_tables(pos, MLA_ROPE)
    cos_mla = _pad_last(jnp.concatenate([c_m, c_m], axis=-1), LANES)
    sin_mla = _pad_last(jnp.concatenate([-s_m, s_m], axis=-1), LANES)

    row = lambda a: a[:, None, :]
    w_pack = _pack_w_in(w_in)
    norm1 = row(norm1_w)
    norm2 = row(norm2_w)
    gn_w = row(ret_gn_w)
    conv_b = row(ssm_conv_b)
    d_skip = row(ssm_d)
    ssm_nw = row(ssm_norm_w)
    qaw = row(mla_q_a_norm_w)
    kvaw = row(mla_kv_a_norm_w)
    wqb = _pad_last(mla_w_qb.reshape(DEPTH, MLA_Q_RANK, MLA_HEADS, MLA_QK), MLA_PAD)
    wqb = wqb.reshape(DEPTH, MLA_Q_RANK, MLA_HEADS * MLA_PAD).astype(BF16)
    wkvb = mla_w_kvb.astype(BF16)
    qnw = row(_pad_last(mla_q_norm_w, MLA_PAD))
    knw_n = row(mla_k_norm_w[:, :MLA_NOPE])
    knw_r = row(_pad_last(mla_k_norm_w[:, MLA_NOPE:], LANES))
    W = RWKV_W
    mu = rwkv_mu
    vec = {
        "mu_r": row(mu[:, :W]), "mu_k": row(mu[:, W:2 * W]), "mu_v": row(mu[:, 2 * W:3 * W]),
        "mu_wl": row(_pad_last(mu[:, 3 * W:3 * W + RWKV_W_LORA], LANES)),
        "mu_al": row(_pad_last(mu[:, 3 * W + RWKV_W_LORA:3 * W + RWKV_W_LORA + RWKV_A_LORA], LANES)),
        "mu_gl": row(mu[:, 3 * W + RWKV_W_LORA + RWKV_A_LORA:]),
        "w0": row(rwkv_w0), "a0": row(rwkv_a0), "k_k": row(rwkv_k_k), "k_a": row(rwkv_k_a),
    }
    mats = {"w2": _pad_axis(rwkv_w2, 1, LANES), "a2": _pad_axis(rwkv_a2, 1, LANES), "g2": rwkv_g2}
    v0 = row(rwkv_v0)
    v1 = _pad_last(rwkv_v1, LANES)
    v2 = _pad_axis(rwkv_v2, 1, LANES)
    ln_w = row(rwkv_ln_w)
    ln_b = row(rwkv_ln_b)
    r_k = row(rwkv_r_k.reshape(DEPTH, W))
    wb = w_branch.astype(BF16)
    wo = w_out.astype(BF16)
    wgu = ffn_w_gu.astype(BF16)
    wdn = ffn_w_down.astype(BF16)
    a_neg = -jnp.exp(ssm_a_log.astype(F32))

    v_first = None
    for l in range(DEPTH):
        pin = _inproj(xs, norm1, w_pack, l)
        o_a = _retention(pin, cos_ret, sin_ret, gn_w, l)
        dt = jax.nn.softplus(pin[:, P_DT:P_DT + SSM_HEADS] + ssm_dt_bias[l][None, :])
        adt = dt * a_neg[l][None, :]
        o_b = _ssd(pin, _pad_last(dt, LANES), _pad_last(adt, LANES), adt.T, ssm_conv_w, conv_b, d_skip,
                   ssm_nw, l)
        q, k, v = _mla_prep(pin, qaw, wqb, kvaw, wkvb, qnw, knw_n, knw_r, cos_mla, sin_mla, l)
        o_c = _flash(q, k, v)
        vmix = None if l == 0 else (v0, v1, v2)
        r_, lw_, k_, v_, a_, b_, g_ = _rwkv_prep(pin, vec, mats, v_first, vmix, l)
        if l == 0:
            v_first = v_
        o_d = _rwkv_core(r_, lw_, k_, v_, a_, b_, g_, ln_w, ln_b, r_k, l)
        xs = _merge(o_a, o_b, o_c, o_d, pin, wb, wo, xs, l)
        xs = _ffn(xs, norm2, wgu, wdn, l)
    return xs[None]
```

```python
import functools
import math

import numpy as np
import jax
import jax.numpy as jnp
from jax import lax
from jax.experimental import pallas as pl
from jax.experimental.pallas import tpu as pltpu

F32 = jnp.float32
BF16 = jnp.bfloat16

D_MODEL = 2048
DEPTH = 4
CHUNK = 64
N_BRANCH = 4
BRANCH_W = 512
RMS_EPS = 1e-6
GN_EPS = 1e-5
ROPE_THETA = 10000.0
RET_HEADS, RET_DK, RET_DV = 4, 128, 128
SSM_HEADS, SSM_HEADDIM, SSM_GROUPS, SSM_STATE, SSM_CONV = 8, 64, 2, 128, 4
SSM_XBC = SSM_HEADS * SSM_HEADDIM + 2 * SSM_GROUPS * SSM_STATE
MLA_HEADS, MLA_Q_RANK, MLA_KV_RANK, MLA_NOPE, MLA_ROPE, MLA_V = 4, 512, 256, 128, 64, 128
MLA_QK = MLA_NOPE + MLA_ROPE
RWKV_HEADS, RWKV_HEAD = 8, 64
RWKV_W = RWKV_HEADS * RWKV_HEAD
RWKV_W_LORA, RWKV_A_LORA, RWKV_V_LORA, RWKV_G_LORA = 64, 64, 32, 128
RWKV_GN_EPS = 64e-5
D_FF = 5632

LANES = 128
SUBLANES = 8
VMEM_LIMIT_BYTES = 56 * 1024 * 1024

_O_RQ, _O_RK, _O_RV, _O_RG = 0, 512, 1024, 1536
_O_SZ, _O_SXBC, _O_SDT = 2048, 2560, 3584
_O_CQ, _O_CKV, _O_KROPE = 3592, 4104, 4360
_O_RW = 4424
_O_GATE = 6216
P_GATE = 0
P_RQ, P_RK, P_RV, P_RG = 8192, 8704, 9216, 9728
P_SXBC, P_SZ = 10240, 11264
P_CQ = 11776
P_WR, P_WK, P_WV = 12288, 12800, 13312
P_CKV = 13824
N_MAIN = 14336
T_DT, T_KROPE, T_WL, T_AL, T_GL = 0, 128, 256, 384, 512
N_TAIL = 640

ROW_BLOCK = 256
RWKV_L = 64
LOG2E = 1.4426950408889634


def _cparams(sem):
    return pltpu.CompilerParams(dimension_semantics=sem, vmem_limit_bytes=VMEM_LIMIT_BYTES)


def _dot(a, b):
    return jnp.dot(a, b, preferred_element_type=F32)


def _dot_nt(a, b):
    return lax.dot_general(a, b, (((1,), (1,)), ((), ())), preferred_element_type=F32)


def _dot_tn(a, b):
    return lax.dot_general(a, b, (((0,), (0,)), ((), ())), preferred_element_type=F32)


def _split_bf16(x):
    hi = x.astype(BF16)
    lo = (x - hi.astype(F32)).astype(BF16)
    return hi, lo


def _mm3(a, b, dot=_dot):
    ah, al = _split_bf16(a)
    bh, bl = _split_bf16(b)
    return dot(ah, bh) + dot(ah, bl) + dot(al, bh)


def _mm1(a, b, dot=_dot):
    return dot(a.astype(BF16), b.astype(BF16))


def _mm_exact_lhs(a01, x, dot=_dot):
    xh, xl = _split_bf16(x)
    a = a01.astype(BF16)
    return dot(a, xh) + dot(a, xl)


def _seg_sum(x, ones_bd):
    xh, xl = _split_bf16(x)
    return _dot(xh, ones_bd) + _dot(xl, ones_bd)


def _sigmoid(x):
    return 1.0 / (1.0 + jnp.exp(-x))


def _silu(x):
    return x * _sigmoid(x)


def _softplus(x):
    return jnp.maximum(x, 0.0) + jnp.log(1.0 + jnp.exp(-jnp.abs(x)))


def _shift_rows(x, carry8, s):
    xr = pltpu.roll(x, s, 0)
    pr = pltpu.roll(carry8, s, 0)
    row = lax.broadcasted_iota(jnp.int32, carry8.shape, 0)
    top = jnp.where(row < s, pr, xr[:SUBLANES])
    return jnp.concatenate([top, xr[SUBLANES:]], axis=0)


def _inproj_body(x_ref, nw_ref, w_ref, wt_ref, o_ref, ot_ref, h_ref):
    @pl.when(pl.program_id(1) == 0)
    def _():
        x = x_ref[...]
        ms = jnp.mean(x * x, axis=-1, keepdims=True)
        h = (x * lax.rsqrt(ms + RMS_EPS) * nw_ref[...]).astype(BF16)
        h_ref[...] = h
        ot_ref[...] = _dot(h, wt_ref[...])

    o_ref[...] = _dot(h_ref[...], w_ref[...]).astype(BF16)


def _inproj(x, norm_w, w_main, w_tail, layer, tm=1024, tn=1024):
    S, D = x.shape
    return pl.pallas_call(
        _inproj_body,
        grid=(S // tm, N_MAIN // tn),
        in_specs=[
            pl.BlockSpec((tm, D), lambda i, j: (i, 0)),
            pl.BlockSpec((None, 1, D), lambda i, j: (layer, 0, 0)),
            pl.BlockSpec((None, D, tn), lambda i, j: (layer, 0, j)),
            pl.BlockSpec((None, D, N_TAIL), lambda i, j: (layer, 0, 0)),
        ],
        out_specs=[pl.BlockSpec((tm, tn), lambda i, j: (i, j)),
                   pl.BlockSpec((tm, N_TAIL), lambda i, j: (i, 0))],
        out_shape=[jax.ShapeDtypeStruct((S, N_MAIN), BF16), jax.ShapeDtypeStruct((S, N_TAIL), F32)],
        scratch_shapes=[pltpu.VMEM((tm, D), BF16)],
        compiler_params=_cparams(("arbitrary", "arbitrary")),
        name="inproj",
    )(x, norm_w, w_main, w_tail)


def _ret_body(q_ref, k_ref, v_ref, g_ref, cos_ref, sin_ref, m_ref, qd_ref, kd_ref, gnw_ref,
              o_ref, s_ref, *, block_decay):
    @pl.when(pl.program_id(0) == 0)
    def _():
        s_ref[...] = jnp.zeros_like(s_ref)

    cos = cos_ref[...]
    sin = sin_ref[...]
    for h in range(RET_HEADS):
        sl = slice(h * RET_DK, (h + 1) * RET_DK)
        q = q_ref[:, sl].astype(F32)
        k = k_ref[:, sl].astype(F32)
        vb = v_ref[:, sl]
        q = q * cos + pltpu.roll(q, RET_DK // 2, 1) * sin
        k = (k * cos + pltpu.roll(k, RET_DK // 2, 1) * sin) * (RET_DK ** -0.5)
        sc = _dot_nt(q.astype(BF16), k.astype(BF16)) * m_ref[h]
        st = s_ref[h]
        o = _dot(sc.astype(BF16), vb) + _dot((q * qd_ref[h]).astype(BF16), st.astype(BF16))
        s_ref[h] = block_decay[h] * st + _dot_tn((k * kd_ref[h]).astype(BF16), vb)
        mu = jnp.mean(o, axis=-1, keepdims=True)
        oc = o - mu
        var = jnp.mean(oc * oc, axis=-1, keepdims=True)
        o = oc * lax.rsqrt(var + GN_EPS) * gnw_ref[:, sl]
        o_ref[:, sl] = (_silu(g_ref[:, sl].astype(F32)) * o).astype(BF16)


def _ret_tables(tb):
    lg = np.log1p(-np.exp2(-5.0 - np.arange(RET_HEADS, dtype=np.float64)))
    pos = np.arange(tb, dtype=np.float64)
    dist = np.abs(pos[:, None] - pos[None, :])
    visible = (pos[None, :] // CHUNK) <= (pos[:, None] // CHUNK)
    mask = np.where(visible[None], np.exp(lg[:, None, None] * dist[None]), 0.0)
    qd = np.exp(lg[:, None] * (pos[None, :] + 1.0))[:, :, None]
    kd = np.exp(lg[:, None] * (tb - 1.0 - pos[None, :]))[:, :, None]
    bd = tuple(float(v) for v in np.exp(lg * tb))
    return (jnp.asarray(mask, F32), jnp.asarray(qd, F32), jnp.asarray(kd, F32), bd)


def _retention(pin, cos2, sin2, gn_w, layer, tb=ROW_BLOCK):
    S = pin.shape[0]
    mask, qd, kd, bd = _ret_tables(tb)
    w = RET_HEADS * RET_DK
    col = lambda off: (lambda i: (i, off // w))
    full3 = lambda i: (0, 0, 0)
    return pl.pallas_call(
        functools.partial(_ret_body, block_decay=bd),
        grid=(S // tb,),
        in_specs=[
            pl.BlockSpec((tb, w), col(P_RQ)),
            pl.BlockSpec((tb, w), col(P_RK)),
            pl.BlockSpec((tb, w), col(P_RV)),
            pl.BlockSpec((tb, w), col(P_RG)),
            pl.BlockSpec((tb, RET_DK), lambda i: (i, 0)),
            pl.BlockSpec((tb, RET_DK), lambda i: (i, 0)),
            pl.BlockSpec((RET_HEADS, tb, tb), full3),
            pl.BlockSpec((RET_HEADS, tb, 1), full3),
            pl.BlockSpec((RET_HEADS, tb, 1), full3),
            pl.BlockSpec((None, 1, w), lambda i: (layer, 0, 0)),
        ],
        out_specs=pl.BlockSpec((tb, w), lambda i: (i, 0)),
        out_shape=jax.ShapeDtypeStruct((S, w), BF16),
        scratch_shapes=[pltpu.VMEM((RET_HEADS, RET_DK, RET_DV), F32)],
        compiler_params=_cparams(("arbitrary",)),
        name="retention",
    )(pin, pin, pin, pin, cos2, sin2, mask, qd, kd, gn_w)


def _ssd_body(xbc_ref, z_ref, dt_ref, dtb_ref, aneg_ref, cw_ref, cb_ref, d_ref, nw_ref,
              o_ref, carry_ref, st_ref):
    tb = xbc_ref.shape[0]
    P, N = SSM_HEADDIM, SSM_STATE
    nx = SSM_HEADS * P

    @pl.when(pl.program_id(0) == 0)
    def _():
        carry_ref[...] = jnp.zeros_like(carry_ref)
        st_ref[...] = jnp.zeros_like(st_ref)

    x = xbc_ref[...].astype(F32)
    carry = carry_ref[...]
    acc = x * cw_ref[SSM_CONV - 1:SSM_CONV, :] + cb_ref[...]
    for s in range(1, SSM_CONV):
        acc = acc + _shift_rows(x, carry, s) * cw_ref[SSM_CONV - 1 - s:SSM_CONV - s, :]
    carry_ref[...] = x[tb - SUBLANES:, :]
    xbc = _silu(acc)

    dt = _softplus(dt_ref[...] + dtb_ref[...])
    adt = dt * aneg_ref[...]
    row = lax.broadcasted_iota(jnp.int32, (tb, tb), 0)
    colm = lax.broadcasted_iota(jnp.int32, (tb, tb), 1)
    causal = colm <= row
    tril = causal.astype(F32)
    a_col = jnp.dot(tril, adt, preferred_element_type=F32, precision=lax.Precision.HIGHEST)
    triu = (row <= colm).astype(F32)
    a_row = lax.dot_general(adt, triu, (((0,), (0,)), ((), ())),
                            preferred_element_type=F32, precision=lax.Precision.HIGHEST)
    a_end = a_col[tb - 1:tb, :]
    lane = lax.broadcasted_iota(jnp.int32, (tb, 2 * P), 1)
    first = lane < P

    for g in range(SSM_GROUPS):
        bm = xbc[:, nx + g * N: nx + (g + 1) * N]
        cm = xbc[:, nx + SSM_GROUPS * N + g * N: nx + SSM_GROUPS * N + (g + 1) * N]
        bmb = bm.astype(BF16)
        cb = _dot_nt(cm.astype(BF16), bmb)
        ys = []
        for pr in range(SSM_HEADS // SSM_GROUPS // 2):
            h0 = g * (SSM_HEADS // SSM_GROUPS) + 2 * pr
            pidx = h0 // 2
            xp = xbc[:, h0 * P:(h0 + 2) * P]
            dtp = jnp.where(first, dt[:, h0:h0 + 1], dt[:, h0 + 1:h0 + 2])
            xdt = (xp * dtp).astype(BF16)
            st = st_ref[pidx]
            stb = st.astype(BF16)
            y2 = []
            snew = []
            for e in range(2):
                h = h0 + e
                ac = a_col[:, h:h + 1]
                seg = ac - a_row[h:h + 1, :]
                dec = jnp.exp(jnp.where(causal, seg, -jnp.inf))
                y = _dot((cb * dec).astype(BF16), xdt)
                y = y + _dot((cm * jnp.exp(ac)).astype(BF16), stb)
                y2.append(y)
                wend = jnp.exp(a_end[:, h:h + 1] - ac)
                snew.append(jnp.exp(a_end[:, h:h + 1]) * st + _dot_tn((bm * wend).astype(BF16), xdt))
            lane_s = lax.broadcasted_iota(jnp.int32, (N, 2 * P), 1)
            st_ref[pidx] = jnp.where(lane_s < P, snew[0], snew[1])
            dsk = jnp.where(first, d_ref[:, h0:h0 + 1], d_ref[:, h0 + 1:h0 + 2])
            ys.append(jnp.where(first, y2[0], y2[1]) + xp * dsk)
        yg = jnp.concatenate(ys, axis=1)
        gsl = slice(g * 4 * P, (g + 1) * 4 * P)
        yg = yg * _silu(z_ref[:, gsl].astype(F32))
        ms = jnp.mean(yg * yg, axis=-1, keepdims=True)
        o_ref[:, gsl] = (yg * lax.rsqrt(ms + RMS_EPS) * nw_ref[:, gsl]).astype(BF16)


def _ssd(pin, tail, dt_bias, a_neg, conv_w, conv_b, d_skip, norm_w, layer, tb=ROW_BLOCK):
    S = pin.shape[0]
    nx = SSM_HEADS * SSM_HEADDIM
    H = SSM_HEADS
    lsel = lambda i: (layer, 0, 0)
    return pl.pallas_call(
        _ssd_body,
        grid=(S // tb,),
        in_specs=[
            pl.BlockSpec((tb, SSM_XBC), lambda i: (i, P_SXBC // SSM_XBC)),
            pl.BlockSpec((tb, nx), lambda i: (i, P_SZ // nx)),
            pl.BlockSpec((tb, LANES), lambda i: (i, T_DT // LANES)),
            pl.BlockSpec((None, 1, LANES), lsel),
            pl.BlockSpec((None, 1, LANES), lsel),
            pl.BlockSpec((None, SSM_CONV, SSM_XBC), lsel),
            pl.BlockSpec((None, 1, SSM_XBC), lsel),
            pl.BlockSpec((None, 1, H), lsel),
            pl.BlockSpec((None, 1, nx), lsel),
        ],
        out_specs=pl.BlockSpec((tb, nx), lambda i: (i, 0)),
        out_shape=jax.ShapeDtypeStruct((S, nx), BF16),
        scratch_shapes=[pltpu.VMEM((SUBLANES, SSM_XBC), F32),
                        pltpu.VMEM((H // 2, SSM_STATE, 2 * SSM_HEADDIM), F32)],
        compiler_params=_cparams(("arbitrary",)),
        name="ssd",
    )(pin, pin, tail, dt_bias, a_neg, conv_w, conv_b, d_skip, norm_w)


MLA_PAD = 256


def _rope_pad(x, cosp, sinp):
    half = MLA_ROPE // 2
    return x * cosp + (pltpu.roll(x, half, 1) + pltpu.roll(x, LANES - half, 1)) * sinp


def _mla_prep_body(cq_ref, ckv_ref, kr_ref, qaw_ref, wqb_ref, kvaw_ref, wkvb_ref, qnw_ref,
                   knw_n_ref, knw_r_ref, cos_ref, sin_ref, q_out, k_out, v_out):
    cosp = cos_ref[...]
    sinp = sin_ref[...]
    cq = cq_ref[...].astype(F32)
    ms = jnp.mean(cq * cq, axis=-1, keepdims=True)
    cqn = (cq * lax.rsqrt(ms + RMS_EPS) * qaw_ref[...]).astype(BF16)
    q = _dot(cqn, wqb_ref[...])
    ckv = ckv_ref[...].astype(F32)
    ms = jnp.mean(ckv * ckv, axis=-1, keepdims=True)
    ckvn = (ckv * lax.rsqrt(ms + RMS_EPS) * kvaw_ref[...]).astype(BF16)
    kv = _dot(ckvn, wkvb_ref[...])
    kr = kr_ref[...]
    ssr = jnp.sum(kr * kr, axis=-1, keepdims=True)
    scale = (MLA_QK ** -0.5) * LOG2E
    for h in range(MLA_HEADS):
        qh = q[:, h * MLA_PAD:(h + 1) * MLA_PAD]
        msq = jnp.sum(qh * qh, axis=-1, keepdims=True) * (1.0 / MLA_QK)
        qh = qh * lax.rsqrt(msq + RMS_EPS) * qnw_ref[...]
        qr = _rope_pad(qh[:, MLA_NOPE:], cosp, sinp)
        q_out[h] = (jnp.concatenate([qh[:, :MLA_NOPE], qr], axis=1) * scale).T.astype(BF16)
        kn = kv[:, h * MLA_PAD: h * MLA_PAD + MLA_NOPE]
        vv = kv[:, h * MLA_PAD + MLA_NOPE:(h + 1) * MLA_PAD]
        msk = (jnp.sum(kn * kn, axis=-1, keepdims=True) + ssr) * (1.0 / MLA_QK)
        rinv = lax.rsqrt(msk + RMS_EPS)
        krh = _rope_pad(kr * rinv * knw_r_ref[...], cosp, sinp)
        k_out[h] = jnp.concatenate([kn * rinv * knw_n_ref[...], krh], axis=1).astype(BF16)
        v_out[h] = vv.T.astype(BF16)


def _mla_prep(pin, tail, qaw, wqb, kvaw, wkvb, qnw, knw_n, knw_r, cosp, sinp, layer, tb=ROW_BLOCK):
    S = pin.shape[0]
    H = MLA_HEADS
    lsel = lambda i: (layer, 0, 0)
    return pl.pallas_call(
        _mla_prep_body,
        grid=(S // tb,),
        in_specs=[
            pl.BlockSpec((tb, MLA_Q_RANK), lambda i: (i, P_CQ // MLA_Q_RANK)),
            pl.BlockSpec((tb, MLA_KV_RANK), lambda i: (i, P_CKV // MLA_KV_RANK)),
            pl.BlockSpec((tb, LANES), lambda i: (i, T_KROPE // LANES)),
            pl.BlockSpec((None, 1, MLA_Q_RANK), lsel),
            pl.BlockSpec((None, MLA_Q_RANK, H * MLA_PAD), lsel),
            pl.BlockSpec((None, 1, MLA_KV_RANK), lsel),
            pl.BlockSpec((None, MLA_KV_RANK, H * MLA_PAD), lsel),
            pl.BlockSpec((None, 1, MLA_PAD), lsel),
            pl.BlockSpec((None, 1, MLA_NOPE), lsel),
            pl.BlockSpec((None, 1, LANES), lsel),
            pl.BlockSpec((tb, LANES), lambda i: (i, 0)),
            pl.BlockSpec((tb, LANES), lambda i: (i, 0)),
        ],
        out_specs=[
            pl.BlockSpec((H, MLA_PAD, tb), lambda i: (0, 0, i)),
            pl.BlockSpec((H, tb, MLA_PAD), lambda i: (0, i, 0)),
            pl.BlockSpec((H, MLA_V, tb), lambda i: (0, 0, i)),
        ],
        out_shape=[
            jax.ShapeDtypeStruct((H, MLA_PAD, S), BF16),
            jax.ShapeDtypeStruct((H, S, MLA_PAD), BF16),
            jax.ShapeDtypeStruct((H, MLA_V, S), BF16),
        ],
        compiler_params=_cparams(("arbitrary",)),
        name="mla_prep",
    )(pin, pin, tail, qaw, wqb, kvaw, wkvb, qnw, knw_n, knw_r, cosp, sinp)


def _flash_body(it_ref, jt_ref, qt_ref, k_ref, vt_ref, o_ref, m_ref, l_ref, acc_ref):
    t = pl.program_id(0)
    i = it_ref[t]
    j = jt_ref[t]
    tq = qt_ref.shape[2]
    tk = k_ref.shape[1]
    H = MLA_HEADS

    @pl.when(j == 0)
    def _():
        m_ref[...] = jnp.full_like(m_ref, -jnp.inf)
        l_ref[...] = jnp.zeros_like(l_ref)
        acc_ref[...] = jnp.zeros_like(acc_ref)

    def step(diagonal):
        if diagonal:
            kc = lax.broadcasted_iota(jnp.int32, (tk, tq), 0) // CHUNK
            qc = lax.broadcasted_iota(jnp.int32, (tk, tq), 1) // CHUNK
            visible = kc <= qc

        def logits(h):
            s = _dot(k_ref[h], qt_ref[h])
            return jnp.where(visible, s, -jnp.inf) if diagonal else s

        def softmax(h, s):
            m_prev = m_ref[h]
            m_new = jnp.maximum(m_prev, jnp.max(s, axis=0, keepdims=True))
            alpha = jnp.exp2(m_prev - m_new)
            p = jnp.exp2(s - m_new)
            l_ref[h] = alpha * l_ref[h] + jnp.sum(p, axis=0, keepdims=True)
            m_ref[h] = m_new
            return alpha, p.astype(BF16)

        def accumulate(h, alpha, p):
            acc_ref[h] = alpha * acc_ref[h] + _dot(vt_ref[h], p)

        s_next = logits(0)
        pending = None
        for h in range(H):
            s_cur = s_next
            if h + 1 < H:
                s_next = logits(h + 1)
            if pending is not None:
                accumulate(*pending)
            pending = (h,) + softmax(h, s_cur)
        accumulate(*pending)

    @pl.when(j < i)
    def _():
        step(False)

    @pl.when(j == i)
    def _():
        step(True)
        for h in range(H):
            o_ref[:, h * MLA_V:(h + 1) * MLA_V] = (acc_ref[h] / l_ref[h]).T.astype(BF16)


def _flash(qt, k, vt, tq=512):
    H, S, _ = k.shape
    nb = S // tq
    pairs = [(i, j) for i in range(nb) for j in range(i + 1)]
    it = jnp.asarray([p[0] for p in pairs], jnp.int32)
    jt = jnp.asarray([p[1] for p in pairs], jnp.int32)
    grid_spec = pltpu.PrefetchScalarGridSpec(
        num_scalar_prefetch=2,
        grid=(len(pairs),),
        in_specs=[
            pl.BlockSpec((H, MLA_PAD, tq), lambda t, it, jt: (0, 0, it[t])),
            pl.BlockSpec((H, tq, MLA_PAD), lambda t, it, jt: (0, jt[t], 0)),
            pl.BlockSpec((H, MLA_V, tq), lambda t, it, jt: (0, 0, jt[t])),
        ],
        out_specs=pl.BlockSpec((tq, H * MLA_V), lambda t, it, jt: (it[t], 0)),
        scratch_shapes=[pltpu.VMEM((H, 1, tq), F32), pltpu.VMEM((H, 1, tq), F32),
                        pltpu.VMEM((H, MLA_V, tq), F32)],
    )
    return pl.pallas_call(
        _flash_body,
        grid_spec=grid_spec,
        out_shape=jax.ShapeDtypeStruct((S, H * MLA_V), BF16),
        compiler_params=_cparams(("arbitrary",)),
        name="mla_flash",
    )(it, jt, qt, k, vt)


def _rwkv_prep_body(*refs, mix_v):
    if mix_v:
        (r_ref, k_ref, v_ref, wl_ref, al_ref, gl_ref, mur, muk, muv, muwl, mual, mugl, w0, a0, w2, a2, g2,
         kk_ref, ka_ref, vf_ref, v0, v1, v2,
         ro, lwo, ko, vo, ao, bo, go, c_r, c_k, c_v, c_wl, c_al, c_gl) = refs
    else:
        (r_ref, k_ref, v_ref, wl_ref, al_ref, gl_ref, mur, muk, muv, muwl, mual, mugl, w0, a0, w2, a2, g2,
         kk_ref, ka_ref,
         ro, lwo, ko, vo, ao, bo, go, c_r, c_k, c_v, c_wl, c_al, c_gl) = refs
    tb = r_ref.shape[0]
    carries = (c_r, c_k, c_v, c_wl, c_al, c_gl)

    @pl.when(pl.program_id(0) == 0)
    def _():
        for c in carries:
            c[...] = jnp.zeros_like(c)

    def mixed(x_ref, c_ref, mu_ref):
        x = x_ref[...].astype(F32)
        prev = _shift_rows(x, c_ref[...], 1)
        c_ref[...] = x[tb - SUBLANES:, :]
        return x + (prev - x) * mu_ref[...]

    r = mixed(r_ref, c_r, mur)
    k = mixed(k_ref, c_k, muk)
    v = mixed(v_ref, c_v, muv)
    wl = mixed(wl_ref, c_wl, muwl)
    al = mixed(al_ref, c_al, mual)
    gl = mixed(gl_ref, c_gl, mugl)

    w_raw = w0[...] + _mm3(jnp.tanh(wl), w2[...])
    lwo[...] = -jnp.exp(-_softplus(-w_raw) - 0.5)
    a_sig = _sigmoid(a0[...] + _mm3(al, a2[...]))
    go[...] = _mm1(_sigmoid(gl), g2[...])
    if mix_v:
        lora = _mm1(_mm1(v, v1[...]), v2[...])
        v = v + (vf_ref[...] - v) * _sigmoid(v0[...] + lora)
    row = lax.broadcasted_iota(jnp.int32, (LANES, LANES), 0) // RWKV_HEAD
    colm = lax.broadcasted_iota(jnp.int32, (LANES, LANES), 1) // RWKV_HEAD
    ones_bd = (row == colm).astype(BF16)
    kk = k * kk_ref[...]
    parts = []
    for p in range(RWKV_W // LANES):
        sl = slice(p * LANES, (p + 1) * LANES)
        kp = kk[:, sl]
        n2 = _seg_sum(kp * kp, ones_bd)
        parts.append(kp / jnp.maximum(jnp.sqrt(n2), 1e-12))
    kk = jnp.concatenate(parts, axis=1)
    ro[...] = r
    ko[...] = k * (1.0 + (a_sig - 1.0) * ka_ref[...])
    vo[...] = v
    ao[...] = -kk
    bo[...] = kk * a_sig


def _rwkv_prep(pin, tail, vec, mats, vfirst, vmix, layer, tb=ROW_BLOCK):
    S = pin.shape[0]
    W = RWKV_W
    lsel = lambda i: (layer, 0, 0)
    lsel1 = lambda i: (layer - 1, 0, 0)
    in_specs = [
        pl.BlockSpec((tb, W), lambda i: (i, P_WR // W)),
        pl.BlockSpec((tb, W), lambda i: (i, P_WK // W)),
        pl.BlockSpec((tb, W), lambda i: (i, P_WV // W)),
        pl.BlockSpec((tb, LANES), lambda i: (i, T_WL // LANES)),
        pl.BlockSpec((tb, LANES), lambda i: (i, T_AL // LANES)),
        pl.BlockSpec((tb, LANES), lambda i: (i, T_GL // LANES)),
    ]
    args = [pin] * 3 + [tail] * 3
    for name, width in (("mu_r", W), ("mu_k", W), ("mu_v", W), ("mu_wl", LANES), ("mu_al", LANES),
                        ("mu_gl", LANES), ("w0", W), ("a0", W)):
        in_specs.append(pl.BlockSpec((None, 1, width), lsel))
        args.append(vec[name])
    for name in ("w2", "a2", "g2"):
        in_specs.append(pl.BlockSpec((None, LANES, W), lsel))
        args.append(mats[name])
    for name in ("k_k", "k_a"):
        in_specs.append(pl.BlockSpec((None, 1, W), lsel))
        args.append(vec[name])
    mix_v = vmix is not None
    if mix_v:
        v0, v1, v2 = vmix
        in_specs += [pl.BlockSpec((tb, W), lambda i: (i, 0)),
                     pl.BlockSpec((None, 1, W), lsel1),
                     pl.BlockSpec((None, W, LANES), lsel1),
                     pl.BlockSpec((None, LANES, W), lsel1)]
        args += [vfirst, v0, v1, v2]
    out_spec = pl.BlockSpec((tb, W), lambda i: (i, 0))
    outs = pl.pallas_call(
        functools.partial(_rwkv_prep_body, mix_v=mix_v),
        grid=(S // tb,),
        in_specs=in_specs,
        out_specs=[out_spec] * 7,
        out_shape=[jax.ShapeDtypeStruct((S, W), F32)] * 7,
        scratch_shapes=[pltpu.VMEM((SUBLANES, W), F32)] * 3 + [pltpu.VMEM((SUBLANES, LANES), F32)] * 3,
        compiler_params=_cparams(("arbitrary",)),
        name="rwkv_prep",
    )(*args)
    return outs


def _rwkv_core_body(r_ref, lw_ref, k_ref, v_ref, a_ref, b_ref, g_ref, lnw_ref, lnb_ref, rk_ref,
                    o_ref, s_ref):
    L = RWKV_L
    N = RWKV_HEAD
    L2 = 2 * L
    tb = r_ref.shape[0]
    npair = RWKV_W // LANES

    @pl.when(pl.program_id(0) == 0)
    def _():
        s_ref[...] = jnp.zeros_like(s_ref)

    rb = lax.broadcasted_iota(jnp.int32, (tb, tb), 0)
    cb = lax.broadcasted_iota(jnp.int32, (tb, tb), 1)
    tril_bd = ((rb // L) == (cb // L)) & (cb <= rb)
    cum_blk = _mm_exact_lhs(tril_bd, lw_ref[...])

    lane = lax.broadcasted_iota(jnp.int32, (L, LANES), 1)
    m0 = lane < N
    r2 = lax.broadcasted_iota(jnp.int32, (L2, L2), 0)
    c2 = lax.broadcasted_iota(jnp.int32, (L2, L2), 1)
    same = (r2 // L) == (c2 // L)
    strict = same & (c2 < r2)
    incl = same & (c2 <= r2)
    eye = (r2 == c2).astype(F32)
    ones_bd = ((r2 // N) == (c2 // N)).astype(BF16)

    def stack(x):
        return jnp.concatenate([jnp.where(m0, x, 0.0), jnp.where(m0, 0.0, x)], axis=0).astype(BF16)

    probs = [(c, p) for c in range(tb // L) for p in range(npair)]
    rows_of = lambda c: slice(c * L, (c + 1) * L)
    lanes_of = lambda p: slice(p * LANES, (p + 1) * LANES)

    opnd = {}
    for (c, p) in probs:
        rows, sl = rows_of(c), lanes_of(p)
        lw = lw_ref[rows, sl]
        cum = cum_blk[rows, sl]
        cum_end = cum[L - 1:L, :]
        r = r_ref[rows, sl]
        k = k_ref[rows, sl]
        v = v_ref[rows, sl]
        a = a_ref[rows, sl]
        b = b_ref[rows, sl]
        e_neg = jnp.exp(-cum)
        e_end = jnp.exp(cum_end - cum)
        re = r * jnp.exp(cum)
        xr_f = jnp.concatenate([jnp.where(m0, re, 0.0), jnp.where(m0, 0.0, re)], axis=0)
        opnd[c, p] = dict(
            xr_f=xr_f, xa=stack(a * jnp.exp(cum - lw)), xr=xr_f.astype(BF16), xb=stack(b * e_neg),
            xk=stack(k * e_neg), xbh=stack(b * e_end), xkh=stack(k * e_end), vs=stack(v),
            decay_end=jnp.exp(cum_end), rk=r * k * rk_ref[:, sl], v=v)

    amat = {}
    for cp in probs:
        o = opnd[cp]
        pmat = _dot_nt(jnp.concatenate([o["xa"], o["xr"]], axis=0),
                       jnp.concatenate([o["xb"], o["xk"]], axis=0))
        amat[cp] = dict(
            ab=jnp.where(strict, pmat[:L2, :L2], 0.0),
            ak=jnp.where(strict, pmat[:L2, L2:], 0.0).astype(BF16),
            rb=jnp.where(incl, pmat[L2:, :L2], 0.0).astype(BF16),
            rk=jnp.where(incl, pmat[L2:, L2:], 0.0).astype(BF16))

    tinv = {cp: eye + amat[cp]["ab"] for cp in probs}
    pw = {cp: amat[cp]["ab"] for cp in probs}
    for _ in range(int(math.log2(L)) - 1):
        for cp in probs:
            pwb = pw[cp].astype(BF16)
            pw[cp] = _dot(pwb, pwb)
        for cp in probs:
            tinv[cp] = tinv[cp] + _mm1(tinv[cp], pw[cp])

    w1 = {cp: _dot(amat[cp]["ak"], opnd[cp]["vs"]) for cp in probs}
    mub = {}
    for cp in probs:
        rhs = jnp.concatenate([opnd[cp]["xa"], w1[cp].astype(BF16)], axis=1)
        mub[cp] = _dot(tinv[cp].astype(BF16), rhs).astype(BF16)
    coef = {}
    for cp in probs:
        o = opnd[cp]
        ry = _dot(amat[cp]["rb"], mub[cp])
        gh = _dot_tn(o["xbh"], mub[cp])
        coef[cp] = dict(
            m_r=(o["xr_f"] + ry[:, :LANES]).astype(BF16),
            y1=ry[:, LANES:] + _dot(amat[cp]["rk"], o["vs"]),
            g=(eye * o["decay_end"] + gh[:, :LANES]).astype(BF16),
            h=gh[:, LANES:] + _dot_tn(o["xkh"], o["vs"]))

    states = [s_ref[p] for p in range(npair)]
    ys = {}
    for (c, p) in probs:
        st = states[p].astype(BF16)
        cf = coef[c, p]
        yst = _dot(cf["m_r"], st) + cf["y1"]
        states[p] = _dot(cf["g"], st) + cf["h"]
        ys[c, p] = yst[:L] + yst[L:]
    for p in range(npair):
        s_ref[p] = states[p]

    for (c, p) in probs:
        rows, sl = rows_of(c), lanes_of(p)
        y = ys[c, p]
        mean = _seg_sum(y, ones_bd) * (1.0 / N)
        yc = y - mean
        var = _seg_sum(yc * yc, ones_bd) * (1.0 / N)
        yn = yc * lax.rsqrt(var + RWKV_GN_EPS) * lnw_ref[:, sl] + lnb_ref[:, sl]
        yn = yn + _seg_sum(opnd[c, p]["rk"], ones_bd) * opnd[c, p]["v"]
        o_ref[rows, sl] = (yn * g_ref[rows, sl]).astype(BF16)


def _rwkv_core(r, lw, k, v, a, b, g, ln_w, ln_b, r_k, layer, tb=ROW_BLOCK):
    S, W = r.shape
    blk = pl.BlockSpec((tb, W), lambda i: (i, 0))
    lsel = pl.BlockSpec((None, 1, W), lambda i: (layer, 0, 0))
    return pl.pallas_call(
        _rwkv_core_body,
        grid=(S // tb,),
        in_specs=[blk] * 7 + [lsel] * 3,
        out_specs=blk,
        out_shape=jax.ShapeDtypeStruct((S, W), BF16),
        scratch_shapes=[pltpu.VMEM((W // LANES, LANES, LANES), F32)],
        compiler_params=_cparams(("arbitrary",)),
        name="rwkv_core",
    )(r, lw, k, v, a, b, g, ln_w, ln_b, r_k)


def _merge_body(oa_ref, ob_ref, oc_ref, od_ref, g0_ref, g1_ref, g2_ref, g3_ref, wb_ref, out_ref):
    o_refs = (oa_ref, ob_ref, oc_ref, od_ref)
    g_refs = (g0_ref, g1_ref, g2_ref, g3_ref)
    acc = None
    for n in range(N_BRANCH):
        contrib = _sigmoid(g_refs[n][...].astype(F32)) * _dot(o_refs[n][...], wb_ref[n])
        acc = contrib if acc is None else acc + contrib
    out_ref[...] = acc.astype(BF16)


def _merge(o_a, o_b, o_c, o_d, pin, w_branch, layer, tm=512):
    S = pin.shape[0]
    D = D_MODEL
    ospec = pl.BlockSpec((tm, BRANCH_W), lambda i: (i, 0))
    gspec = lambda n: pl.BlockSpec((tm, D), lambda i: (i, n))
    return pl.pallas_call(
        _merge_body,
        grid=(S // tm,),
        in_specs=[ospec, ospec, ospec, ospec, gspec(0), gspec(1), gspec(2), gspec(3),
                  pl.BlockSpec((None, N_BRANCH, BRANCH_W, D), lambda i: (layer, 0, 0, 0))],
        out_specs=pl.BlockSpec((tm, D), lambda i: (i, 0)),
        out_shape=jax.ShapeDtypeStruct((S, D), BF16),
        compiler_params=_cparams(("arbitrary",)),
        name="merge",
    )(o_a, o_b, o_c, o_d, pin, pin, pin, pin, w_branch)


def _outproj_body(m_ref, w_ref, x_ref, o_ref):
    o_ref[...] = x_ref[...] + _dot(m_ref[...], w_ref[...])


def _outproj(merged, w_out, x, layer, tm=1024, tn=1024):
    S, D = x.shape
    return pl.pallas_call(
        _outproj_body,
        grid=(S // tm, D // tn),
        in_specs=[
            pl.BlockSpec((tm, D), lambda i, j: (i, 0)),
            pl.BlockSpec((None, D, tn), lambda i, j: (layer, 0, j)),
            pl.BlockSpec((tm, tn), lambda i, j: (i, j)),
        ],
        out_specs=pl.BlockSpec((tm, tn), lambda i, j: (i, j)),
        out_shape=jax.ShapeDtypeStruct((S, D), F32),
        compiler_params=_cparams(("arbitrary", "arbitrary")),
        name="outproj",
    )(merged, w_out, x)


def _ffn_body(x_ref, nw_ref, wg_ref, wu_ref, wd_ref, out_ref, h_ref, acc_ref):
    f = pl.program_id(1)

    @pl.when(f == 0)
    def _():
        x = x_ref[...]
        ms = jnp.mean(x * x, axis=-1, keepdims=True)
        h_ref[...] = (x * lax.rsqrt(ms + RMS_EPS) * nw_ref[...]).astype(BF16)
        acc_ref[...] = jnp.zeros_like(acc_ref)

    h = h_ref[...]
    act = (_silu(_dot(h, wg_ref[...])) * _dot(h, wu_ref[...])).astype(BF16)
    acc_ref[...] += _dot(act, wd_ref[...])

    @pl.when(f == pl.num_programs(1) - 1)
    def _():
        out_ref[...] = x_ref[...] + acc_ref[...]


def _ffn(x, norm_w, w_gu, w_down, layer, tm=512, tf=512):
    S, D = x.shape
    nf = D_FF // tf
    return pl.pallas_call(
        _ffn_body,
        grid=(S // tm, nf),
        in_specs=[
            pl.BlockSpec((tm, D), lambda i, f: (i, 0)),
            pl.BlockSpec((None, 1, D), lambda i, f: (layer, 0, 0)),
            pl.BlockSpec((None, D, tf), lambda i, f: (layer, 0, f)),
            pl.BlockSpec((None, D, tf), lambda i, f: (layer, 0, f + nf)),
            pl.BlockSpec((None, tf, D), lambda i, f: (layer, f, 0)),
        ],
        out_specs=pl.BlockSpec((tm, D), lambda i, f: (i, 0)),
        out_shape=jax.ShapeDtypeStruct((S, D), F32),
        scratch_shapes=[pltpu.VMEM((tm, D), BF16), pltpu.VMEM((tm, D), F32)],
        compiler_params=_cparams(("arbitrary", "arbitrary")),
        name="ffn",
    )(x, norm_w, w_gu, w_gu, w_down)


def _pad_last(a, width):
    pad = width - a.shape[-1]
    if pad == 0:
        return a
    return jnp.pad(a, [(0, 0)] * (a.ndim - 1) + [(0, pad)])


def _pad_axis(a, axis, width):
    pad = width - a.shape[axis]
    if pad == 0:
        return a
    cfg = [(0, 0)] * a.ndim
    cfg[axis] = (0, pad)
    return jnp.pad(a, cfg)


def _pack_w_in(w_in):
    c = lambda off, n: w_in[..., off:off + n]
    r0 = _O_RW
    main = [
        c(_O_GATE, N_BRANCH * D_MODEL),
        c(_O_RQ, 512), c(_O_RK, 512), c(_O_RV, 512), c(_O_RG, 512),
        c(_O_SXBC, SSM_XBC), c(_O_SZ, 512),
        c(_O_CQ, MLA_Q_RANK),
        c(r0, RWKV_W), c(r0 + RWKV_W, RWKV_W), c(r0 + 2 * RWKV_W, RWKV_W),
        c(_O_CKV, MLA_KV_RANK),
    ]
    tail = [
        _pad_last(c(_O_SDT, SSM_HEADS), LANES),
        _pad_last(c(_O_KROPE, MLA_ROPE), LANES),
        _pad_last(c(r0 + 3 * RWKV_W, RWKV_W_LORA), LANES),
        _pad_last(c(r0 + 3 * RWKV_W + RWKV_W_LORA, RWKV_A_LORA), LANES),
        c(r0 + 3 * RWKV_W + RWKV_W_LORA + RWKV_A_LORA, RWKV_G_LORA),
    ]
    w_main = _pad_last(jnp.concatenate([p.astype(BF16) for p in main], axis=-1), N_MAIN)
    w_tail = jnp.concatenate([p.astype(BF16) for p in tail], axis=-1)
    return w_main, w_tail


def _rope_tables(positions, dim):
    inv = 1.0 / (ROPE_THETA ** (jnp.arange(0, dim, 2, dtype=F32) / dim))
    ang = positions.astype(F32)[:, None] * inv
    return jnp.cos(ang), jnp.sin(ang)


def kernel(x, positions, norm1_w, w_in, ret_gn_w, ssm_conv_w, ssm_conv_b, ssm_dt_bias, ssm_a_log, ssm_d, ssm_norm_w, mla_q_a_norm_w, mla_w_qb, mla_kv_a_norm_w, mla_w_kvb, mla_q_norm_w, mla_k_norm_w, rwkv_mu, rwkv_w0, rwkv_w2, rwkv_a0, rwkv_a2, rwkv_g2, rwkv_v0, rwkv_v1, rwkv_v2, rwkv_k_k, rwkv_k_a, rwkv_r_k, rwkv_ln_w, rwkv_ln_b, w_branch, w_out, norm2_w, ffn_w_gu, ffn_w_down):
    B, S, D = x.shape
    assert B == 1 and D == D_MODEL and S % 1024 == 0
    xs = x[0]
    pos = positions[0]

    c_r, s_r = _rope_tables(pos, RET_DK)
    cos_ret = jnp.concatenate([c_r, c_r], axis=-1)
    sin_ret = jnp.concatenate([-s_r, s_r], axis=-1)
    c_m, s_m = _rope_tables(pos, MLA_ROPE)
    cos_mla = _pad_last(jnp.concatenate([c_m, c_m], axis=-1), LANES)
    sin_mla = _pad_last(jnp.concatenate([-s_m, s_m], axis=-1), LANES)

    row = lambda a: a[:, None, :]
    w_main, w_tail = _pack_w_in(w_in)
    norm1 = row(norm1_w)
    norm2 = row(norm2_w)
    gn_w = row(ret_gn_w)
    conv_b = row(ssm_conv_b)
    d_skip = row(ssm_d)
    ssm_nw = row(ssm_norm_w)
    dt_bias = row(_pad_last(ssm_dt_bias, LANES))
    a_neg = row(_pad_last(-jnp.exp(ssm_a_log.astype(F32)), LANES))
    qaw = row(mla_q_a_norm_w)
    kvaw = row(mla_kv_a_norm_w)
    wqb = _pad_last(mla_w_qb.reshape(DEPTH, MLA_Q_RANK, MLA_HEADS, MLA_QK), MLA_PAD)
    wqb = wqb.reshape(DEPTH, MLA_Q_RANK, MLA_HEADS * MLA_PAD).astype(BF16)
    wkvb = mla_w_kvb.astype(BF16)
    qnw = row(_pad_last(mla_q_norm_w, MLA_PAD))
    knw_n = row(mla_k_norm_w[:, :MLA_NOPE])
    knw_r = row(_pad_last(mla_k_norm_w[:, MLA_NOPE:], LANES))
    W = RWKV_W
    mu = rwkv_mu
    vec = {
        "mu_r": row(mu[:, :W]), "mu_k": row(mu[:, W:2 * W]), "mu_v": row(mu[:, 2 * W:3 * W]),
        "mu_wl": row(_pad_last(mu[:, 3 * W:3 * W + RWKV_W_LORA], LANES)),
        "mu_al": row(_pad_last(mu[:, 3 * W + RWKV_W_LORA:3 * W + RWKV_W_LORA + RWKV_A_LORA], LANES)),
        "mu_gl": row(mu[:, 3 * W + RWKV_W_LORA + RWKV_A_LORA:]),
        "w0": row(rwkv_w0), "a0": row(rwkv_a0), "k_k": row(rwkv_k_k), "k_a": row(rwkv_k_a),
    }
    mats = {"w2": _pad_axis(rwkv_w2, 1, LANES), "a2": _pad_axis(rwkv_a2, 1, LANES), "g2": rwkv_g2}
    v0 = row(rwkv_v0)
    v1 = _pad_last(rwkv_v1, LANES)
    v2 = _pad_axis(rwkv_v2, 1, LANES)
    ln_w = row(rwkv_ln_w)
    ln_b = row(rwkv_ln_b)
    r_k = row(rwkv_r_k.reshape(DEPTH, W))
    wb = w_branch.astype(BF16)
    wo = w_out.astype(BF16)
    wgu = ffn_w_gu.astype(BF16)
    wdn = ffn_w_down.astype(BF16)

    v_first = None
    for l in range(DEPTH):
        pin, tail = _inproj(xs, norm1, w_main, w_tail, l)
        o_a = _retention(pin, cos_ret, sin_ret, gn_w, l)
        o_b = _ssd(pin, tail, dt_bias, a_neg, ssm_conv_w, conv_b, d_skip, ssm_nw, l)
        q, k, v = _mla_prep(pin, tail, qaw, wqb, kvaw, wkvb, qnw, knw_n, knw_r, cos_mla, sin_mla, l)
        o_c = _flash(q, k, v)
        vmix = None if l == 0 else (v0, v1, v2)
        r_, lw_, k_, v_, a_, b_, g_ = _rwkv_prep(pin, tail, vec, mats, v_first, vmix, l)
        if l == 0:
            v_first = v_
        o_d = _rwkv_core(r_, lw_, k_, v_, a_, b_, g_, ln_w, ln_b, r_k, l)
        merged = _merge(o_a, o_b, o_c, o_d, pin, wb, l)
        xs = _outproj(merged, wo, xs, l)
        xs = _ffn(xs, norm2, wgu, wdn, l)
    return xs[None]
```

```python
import functools
import math

import numpy as np
import jax
import jax.numpy as jnp
from jax import lax
from jax.experimental import pallas as pl
from jax.experimental.pallas import tpu as pltpu

F32 = jnp.float32
BF16 = jnp.bfloat16

D_MODEL = 2048
DEPTH = 4
CHUNK = 64
N_BRANCH = 4
BRANCH_W = 512
RMS_EPS = 1e-6
GN_EPS = 1e-5
ROPE_THETA = 10000.0
RET_HEADS, RET_DK, RET_DV = 4, 128, 128
SSM_HEADS, SSM_HEADDIM, SSM_GROUPS, SSM_STATE, SSM_CONV = 8, 64, 2, 128, 4
SSM_XBC = SSM_HEADS * SSM_HEADDIM + 2 * SSM_GROUPS * SSM_STATE
MLA_HEADS, MLA_Q_RANK, MLA_KV_RANK, MLA_NOPE, MLA_ROPE, MLA_V = 4, 512, 256, 128, 64, 128
MLA_QK = MLA_NOPE + MLA_ROPE
RWKV_HEADS, RWKV_HEAD = 8, 64
RWKV_W = RWKV_HEADS * RWKV_HEAD
RWKV_W_LORA, RWKV_A_LORA, RWKV_V_LORA, RWKV_G_LORA = 64, 64, 32, 128
RWKV_GN_EPS = 64e-5
D_FF = 5632

LANES = 128
SUBLANES = 8
VMEM_LIMIT_BYTES = 56 * 1024 * 1024

_O_RQ, _O_RK, _O_RV, _O_RG = 0, 512, 1024, 1536
_O_SZ, _O_SXBC, _O_SDT = 2048, 2560, 3584
_O_CQ, _O_CKV, _O_KROPE = 3592, 4104, 4360
_O_RW = 4424
_O_GATE = 6216
P_GATE = 0
P_RQ, P_RK, P_RV, P_RG = 8192, 8704, 9216, 9728
P_SXBC, P_SZ = 10240, 11264
P_CQ = 11776
P_WR, P_WK, P_WV = 12288, 12800, 13312
P_CKV = 13824
N_MAIN = 14336
T_DT, T_KROPE, T_WL, T_AL, T_GL = 0, 128, 256, 384, 512
N_TAIL = 640

ROW_BLOCK = 256
RWKV_L = 64
LOG2E = 1.4426950408889634


def _cparams(sem):
    return pltpu.CompilerParams(dimension_semantics=sem, vmem_limit_bytes=VMEM_LIMIT_BYTES)


def _dot(a, b):
    return jnp.dot(a, b, preferred_element_type=F32)


def _dot_nt(a, b):
    return lax.dot_general(a, b, (((1,), (1,)), ((), ())), preferred_element_type=F32)


def _dot_tn(a, b):
    return lax.dot_general(a, b, (((0,), (0,)), ((), ())), preferred_element_type=F32)


def _split_bf16(x):
    hi = x.astype(BF16)
    lo = (x - hi.astype(F32)).astype(BF16)
    return hi, lo


def _mm3(a, b, dot=_dot):
    ah, al = _split_bf16(a)
    bh, bl = _split_bf16(b)
    return dot(ah, bh) + dot(ah, bl) + dot(al, bh)


def _mm1(a, b, dot=_dot):
    return dot(a.astype(BF16), b.astype(BF16))


def _mm_exact_lhs(a01, x, dot=_dot):
    xh, xl = _split_bf16(x)
    a = a01.astype(BF16)
    return dot(a, xh) + dot(a, xl)


def _seg_sum(x, ones_bd):
    xh, xl = _split_bf16(x)
    return _dot(xh, ones_bd) + _dot(xl, ones_bd)


def _sigmoid(x):
    return 0.5 * jnp.tanh(0.5 * x) + 0.5


def _silu(x):
    return x * _sigmoid(x)


def _softplus(x):
    return jnp.maximum(x, 0.0) + jnp.log(1.0 + jnp.exp(-jnp.abs(x)))


def _shift_rows(x, carry8, s):
    xr = pltpu.roll(x, s, 0)
    pr = pltpu.roll(carry8, s, 0)
    row = lax.broadcasted_iota(jnp.int32, carry8.shape, 0)
    top = jnp.where(row < s, pr, xr[:SUBLANES])
    return jnp.concatenate([top, xr[SUBLANES:]], axis=0)


def _inproj_body(x_ref, nw_ref, w_ref, wt_ref, o_ref, ot_ref, h_ref):
    @pl.when(pl.program_id(1) == 0)
    def _():
        x = x_ref[...]
        ms = jnp.mean(x * x, axis=-1, keepdims=True)
        h = (x * lax.rsqrt(ms + RMS_EPS) * nw_ref[...]).astype(BF16)
        h_ref[...] = h
        ot_ref[...] = _dot(h, wt_ref[...])

    o_ref[...] = _dot(h_ref[...], w_ref[...]).astype(BF16)


def _inproj(x, norm_w, w_main, w_tail, layer, tm=1024, tn=1024):
    S, D = x.shape
    return pl.pallas_call(
        _inproj_body,
        grid=(S // tm, N_MAIN // tn),
        in_specs=[
            pl.BlockSpec((tm, D), lambda i, j: (i, 0)),
            pl.BlockSpec((None, 1, D), lambda i, j: (layer, 0, 0)),
            pl.BlockSpec((None, D, tn), lambda i, j: (layer, 0, j)),
            pl.BlockSpec((None, D, N_TAIL), lambda i, j: (layer, 0, 0)),
        ],
        out_specs=[pl.BlockSpec((tm, tn), lambda i, j: (i, j)),
                   pl.BlockSpec((tm, N_TAIL), lambda i, j: (i, 0))],
        out_shape=[jax.ShapeDtypeStruct((S, N_MAIN), BF16), jax.ShapeDtypeStruct((S, N_TAIL), F32)],
        scratch_shapes=[pltpu.VMEM((tm, D), BF16)],
        compiler_params=_cparams(("arbitrary", "arbitrary")),
        name="inproj",
    )(x, norm_w, w_main, w_tail)


def _ret_body(q_ref, k_ref, v_ref, g_ref, cos_ref, sin_ref, m_ref, qd_ref, kd_ref, gnw_ref,
              o_ref, s_ref, *, block_decay):
    @pl.when(pl.program_id(0) == 0)
    def _():
        s_ref[...] = jnp.zeros_like(s_ref)

    cos = cos_ref[...]
    sin = sin_ref[...]
    heads = range(RET_HEADS)
    lanes = [slice(h * RET_DK, (h + 1) * RET_DK) for h in heads]
    q, k, vb = [], [], []
    for h in heads:
        qh = q_ref[:, lanes[h]].astype(F32)
        kh = k_ref[:, lanes[h]].astype(F32)
        q.append(qh * cos + pltpu.roll(qh, RET_DK // 2, 1) * sin)
        k.append((kh * cos + pltpu.roll(kh, RET_DK // 2, 1) * sin) * (RET_DK ** -0.5))
        vb.append(v_ref[:, lanes[h]])
    sc = [(_dot_nt(q[h].astype(BF16), k[h].astype(BF16)) * m_ref[h]).astype(BF16) for h in heads]
    st = [s_ref[h] for h in heads]
    o = [_dot(sc[h], vb[h]) + _dot((q[h] * qd_ref[h]).astype(BF16), st[h].astype(BF16)) for h in heads]
    for h in heads:
        s_ref[h] = block_decay[h] * st[h] + _dot_tn((k[h] * kd_ref[h]).astype(BF16), vb[h])
    oc = [o[h] - jnp.mean(o[h], axis=-1, keepdims=True) for h in heads]
    var = [jnp.mean(oc[h] * oc[h], axis=-1, keepdims=True) for h in heads]
    for h in heads:
        on = oc[h] * lax.rsqrt(var[h] + GN_EPS) * gnw_ref[:, lanes[h]]
        o_ref[:, lanes[h]] = (_silu(g_ref[:, lanes[h]].astype(F32)) * on).astype(BF16)


def _ret_tables(tb):
    lg = np.log1p(-np.exp2(-5.0 - np.arange(RET_HEADS, dtype=np.float64)))
    pos = np.arange(tb, dtype=np.float64)
    dist = np.abs(pos[:, None] - pos[None, :])
    visible = (pos[None, :] // CHUNK) <= (pos[:, None] // CHUNK)
    mask = np.where(visible[None], np.exp(lg[:, None, None] * dist[None]), 0.0)
    qd = np.exp(lg[:, None] * (pos[None, :] + 1.0))[:, :, None]
    kd = np.exp(lg[:, None] * (tb - 1.0 - pos[None, :]))[:, :, None]
    bd = tuple(float(v) for v in np.exp(lg * tb))
    return (jnp.asarray(mask, F32), jnp.asarray(qd, F32), jnp.asarray(kd, F32), bd)


def _retention(pin, cos2, sin2, gn_w, layer, tb=ROW_BLOCK):
    S = pin.shape[0]
    mask, qd, kd, bd = _ret_tables(tb)
    w = RET_HEADS * RET_DK
    col = lambda off: (lambda i: (i, off // w))
    full3 = lambda i: (0, 0, 0)
    return pl.pallas_call(
        functools.partial(_ret_body, block_decay=bd),
        grid=(S // tb,),
        in_specs=[
            pl.BlockSpec((tb, w), col(P_RQ)),
            pl.BlockSpec((tb, w), col(P_RK)),
            pl.BlockSpec((tb, w), col(P_RV)),
            pl.BlockSpec((tb, w), col(P_RG)),
            pl.BlockSpec((tb, RET_DK), lambda i: (i, 0)),
            pl.BlockSpec((tb, RET_DK), lambda i: (i, 0)),
            pl.BlockSpec((RET_HEADS, tb, tb), full3),
            pl.BlockSpec((RET_HEADS, tb, 1), full3),
            pl.BlockSpec((RET_HEADS, tb, 1), full3),
            pl.BlockSpec((None, 1, w), lambda i: (layer, 0, 0)),
        ],
        out_specs=pl.BlockSpec((tb, w), lambda i: (i, 0)),
        out_shape=jax.ShapeDtypeStruct((S, w), BF16),
        scratch_shapes=[pltpu.VMEM((RET_HEADS, RET_DK, RET_DV), F32)],
        compiler_params=_cparams(("arbitrary",)),
        name="retention",
    )(pin, pin, pin, pin, cos2, sin2, mask, qd, kd, gn_w)


def _ssd_body(xbc_ref, z_ref, dt_ref, dtb_ref, aneg_ref, cw_ref, cb_ref, d_ref, nw_ref,
              o_ref, carry_ref, st_ref):
    tb = xbc_ref.shape[0]
    P, N = SSM_HEADDIM, SSM_STATE
    nx = SSM_HEADS * P

    @pl.when(pl.program_id(0) == 0)
    def _():
        carry_ref[...] = jnp.zeros_like(carry_ref)
        st_ref[...] = jnp.zeros_like(st_ref)

    x = xbc_ref[...].astype(F32)
    carry = carry_ref[...]
    acc = x * cw_ref[SSM_CONV - 1:SSM_CONV, :] + cb_ref[...]
    for s in range(1, SSM_CONV):
        acc = acc + _shift_rows(x, carry, s) * cw_ref[SSM_CONV - 1 - s:SSM_CONV - s, :]
    carry_ref[...] = x[tb - SUBLANES:, :]
    xbc = _silu(acc)

    dt = _softplus(dt_ref[...] + dtb_ref[...])
    adt = dt * aneg_ref[...]
    row = lax.broadcasted_iota(jnp.int32, (tb, tb), 0)
    colm = lax.broadcasted_iota(jnp.int32, (tb, tb), 1)
    causal = colm <= row
    tril = causal.astype(F32)
    a_col = jnp.dot(tril, adt, preferred_element_type=F32, precision=lax.Precision.HIGHEST)
    triu = (row <= colm).astype(F32)
    a_row = lax.dot_general(adt, triu, (((0,), (0,)), ((), ())),
                            preferred_element_type=F32, precision=lax.Precision.HIGHEST)
    a_end = a_col[tb - 1:tb, :]
    lane = lax.broadcasted_iota(jnp.int32, (tb, 2 * P), 1)
    first = lane < P

    for g in range(SSM_GROUPS):
        bm = xbc[:, nx + g * N: nx + (g + 1) * N]
        cm = xbc[:, nx + SSM_GROUPS * N + g * N: nx + SSM_GROUPS * N + (g + 1) * N]
        bmb = bm.astype(BF16)
        cb = _dot_nt(cm.astype(BF16), bmb)
        ys = []
        for pr in range(SSM_HEADS // SSM_GROUPS // 2):
            h0 = g * (SSM_HEADS // SSM_GROUPS) + 2 * pr
            pidx = h0 // 2
            xp = xbc[:, h0 * P:(h0 + 2) * P]
            dtp = jnp.where(first, dt[:, h0:h0 + 1], dt[:, h0 + 1:h0 + 2])
            xdt = (xp * dtp).astype(BF16)
            st = st_ref[pidx]
            stb = st.astype(BF16)
            y2 = []
            snew = []
            for e in range(2):
                h = h0 + e
                ac = a_col[:, h:h + 1]
                seg = ac - a_row[h:h + 1, :]
                dec = jnp.exp(jnp.where(causal, seg, -jnp.inf))
                y = _dot((cb * dec).astype(BF16), xdt)
                y = y + _dot((cm * jnp.exp(ac)).astype(BF16), stb)
                y2.append(y)
                wend = jnp.exp(a_end[:, h:h + 1] - ac)
                snew.append(jnp.exp(a_end[:, h:h + 1]) * st + _dot_tn((bm * wend).astype(BF16), xdt))
            lane_s = lax.broadcasted_iota(jnp.int32, (N, 2 * P), 1)
            st_ref[pidx] = jnp.where(lane_s < P, snew[0], snew[1])
            dsk = jnp.where(first, d_ref[:, h0:h0 + 1], d_ref[:, h0 + 1:h0 + 2])
            ys.append(jnp.where(first, y2[0], y2[1]) + xp * dsk)
        yg = jnp.concatenate(ys, axis=1)
        gsl = slice(g * 4 * P, (g + 1) * 4 * P)
        yg = yg * _silu(z_ref[:, gsl].astype(F32))
        ms = jnp.mean(yg * yg, axis=-1, keepdims=True)
        o_ref[:, gsl] = (yg * lax.rsqrt(ms + RMS_EPS) * nw_ref[:, gsl]).astype(BF16)


def _ssd(pin, tail, dt_bias, a_neg, conv_w, conv_b, d_skip, norm_w, layer, tb=ROW_BLOCK):
    S = pin.shape[0]
    nx = SSM_HEADS * SSM_HEADDIM
    H = SSM_HEADS
    lsel = lambda i: (layer, 0, 0)
    return pl.pallas_call(
        _ssd_body,
        grid=(S // tb,),
        in_specs=[
            pl.BlockSpec((tb, SSM_XBC), lambda i: (i, P_SXBC // SSM_XBC)),
            pl.BlockSpec((tb, nx), lambda i: (i, P_SZ // nx)),
            pl.BlockSpec((tb, LANES), lambda i: (i, T_DT // LANES)),
            pl.BlockSpec((None, 1, LANES), lsel),
            pl.BlockSpec((None, 1, LANES), lsel),
            pl.BlockSpec((None, SSM_CONV, SSM_XBC), lsel),
            pl.BlockSpec((None, 1, SSM_XBC), lsel),
            pl.BlockSpec((None, 1, H), lsel),
            pl.BlockSpec((None, 1, nx), lsel),
        ],
        out_specs=pl.BlockSpec((tb, nx), lambda i: (i, 0)),
        out_shape=jax.ShapeDtypeStruct((S, nx), BF16),
        scratch_shapes=[pltpu.VMEM((SUBLANES, SSM_XBC), F32),
                        pltpu.VMEM((H // 2, SSM_STATE, 2 * SSM_HEADDIM), F32)],
        compiler_params=_cparams(("arbitrary",)),
        name="ssd",
    )(pin, pin, tail, dt_bias, a_neg, conv_w, conv_b, d_skip, norm_w)


MLA_PAD = 256


def _rope_pad(x, cosp, sinp):
    half = MLA_ROPE // 2
    return x * cosp + (pltpu.roll(x, half, 1) + pltpu.roll(x, LANES - half, 1)) * sinp


def _mla_prep_body(cq_ref, ckv_ref, kr_ref, qaw_ref, wqb_ref, kvaw_ref, wkvb_ref, qnw_ref,
                   knw_n_ref, knw_r_ref, cos_ref, sin_ref, q_out, k_out, v_out):
    cosp = cos_ref[...]
    sinp = sin_ref[...]
    cq = cq_ref[...].astype(F32)
    ms = jnp.mean(cq * cq, axis=-1, keepdims=True)
    cqn = (cq * lax.rsqrt(ms + RMS_EPS) * qaw_ref[...]).astype(BF16)
    q = _dot(cqn, wqb_ref[...])
    ckv = ckv_ref[...].astype(F32)
    ms = jnp.mean(ckv * ckv, axis=-1, keepdims=True)
    ckvn = (ckv * lax.rsqrt(ms + RMS_EPS) * kvaw_ref[...]).astype(BF16)
    kv = _dot(ckvn, wkvb_ref[...])
    kr = kr_ref[...]
    ssr = jnp.sum(kr * kr, axis=-1, keepdims=True)
    scale = (MLA_QK ** -0.5) * LOG2E
    for h in range(MLA_HEADS):
        qh = q[:, h * MLA_PAD:(h + 1) * MLA_PAD]
        msq = jnp.sum(qh * qh, axis=-1, keepdims=True) * (1.0 / MLA_QK)
        qh = qh * lax.rsqrt(msq + RMS_EPS) * qnw_ref[...]
        qr = _rope_pad(qh[:, MLA_NOPE:], cosp, sinp)
        q_out[h] = (jnp.concatenate([qh[:, :MLA_NOPE], qr], axis=1) * scale).T.astype(BF16)
        kn = kv[:, h * MLA_PAD: h * MLA_PAD + MLA_NOPE]
        vv = kv[:, h * MLA_PAD + MLA_NOPE:(h + 1) * MLA_PAD]
        msk = (jnp.sum(kn * kn, axis=-1, keepdims=True) + ssr) * (1.0 / MLA_QK)
        rinv = lax.rsqrt(msk + RMS_EPS)
        krh = _rope_pad(kr * rinv * knw_r_ref[...], cosp, sinp)
        k_out[h] = jnp.concatenate([kn * rinv * knw_n_ref[...], krh], axis=1).astype(BF16)
        v_out[h] = vv.T.astype(BF16)


def _mla_prep(pin, tail, qaw, wqb, kvaw, wkvb, qnw, knw_n, knw_r, cosp, sinp, layer, tb=ROW_BLOCK):
    S = pin.shape[0]
    H = MLA_HEADS
    lsel = lambda i: (layer, 0, 0)
    return pl.pallas_call(
        _mla_prep_body,
        grid=(S // tb,),
        in_specs=[
            pl.BlockSpec((tb, MLA_Q_RANK), lambda i: (i, P_CQ // MLA_Q_RANK)),
            pl.BlockSpec((tb, MLA_KV_RANK), lambda i: (i, P_CKV // MLA_KV_RANK)),
            pl.BlockSpec((tb, LANES), lambda i: (i, T_KROPE // LANES)),
            pl.BlockSpec((None, 1, MLA_Q_RANK), lsel),
            pl.BlockSpec((None, MLA_Q_RANK, H * MLA_PAD), lsel),
            pl.BlockSpec((None, 1, MLA_KV_RANK), lsel),
            pl.BlockSpec((None, MLA_KV_RANK, H * MLA_PAD), lsel),
            pl.BlockSpec((None, 1, MLA_PAD), lsel),
            pl.BlockSpec((None, 1, MLA_NOPE), lsel),
            pl.BlockSpec((None, 1, LANES), lsel),
            pl.BlockSpec((tb, LANES), lambda i: (i, 0)),
            pl.BlockSpec((tb, LANES), lambda i: (i, 0)),
        ],
        out_specs=[
            pl.BlockSpec((H, MLA_PAD, tb), lambda i: (0, 0, i)),
            pl.BlockSpec((H, tb, MLA_PAD), lambda i: (0, i, 0)),
            pl.BlockSpec((H, MLA_V, tb), lambda i: (0, 0, i)),
        ],
        out_shape=[
            jax.ShapeDtypeStruct((H, MLA_PAD, S), BF16),
            jax.ShapeDtypeStruct((H, S, MLA_PAD), BF16),
            jax.ShapeDtypeStruct((H, MLA_V, S), BF16),
        ],
        compiler_params=_cparams(("arbitrary",)),
        name="mla_prep",
    )(pin, pin, tail, qaw, wqb, kvaw, wkvb, qnw, knw_n, knw_r, cosp, sinp)


def _flash_body(it_ref, jt_ref, qt_ref, k_ref, vt_ref, o_ref, m_ref, l_ref, acc_ref):
    t = pl.program_id(0)
    i = it_ref[t]
    j = jt_ref[t]
    tq = qt_ref.shape[2]
    tk = k_ref.shape[1]
    H = MLA_HEADS

    @pl.when(j == 0)
    def _():
        m_ref[...] = jnp.full_like(m_ref, -jnp.inf)
        l_ref[...] = jnp.zeros_like(l_ref)
        acc_ref[...] = jnp.zeros_like(acc_ref)

    def step(diagonal):
        if diagonal:
            kc = lax.broadcasted_iota(jnp.int32, (tk, tq), 0) // CHUNK
            qc = lax.broadcasted_iota(jnp.int32, (tk, tq), 1) // CHUNK
            visible = kc <= qc

        def logits(h):
            s = _dot(k_ref[h], qt_ref[h])
            return jnp.where(visible, s, -jnp.inf) if diagonal else s

        def softmax(h, s):
            m_prev = m_ref[h]
            m_new = jnp.maximum(m_prev, jnp.max(s, axis=0, keepdims=True))
            alpha = jnp.exp2(m_prev - m_new)
            p = jnp.exp2(s - m_new)
            l_ref[h] = alpha * l_ref[h] + jnp.sum(p, axis=0, keepdims=True)
            m_ref[h] = m_new
            return alpha, p.astype(BF16)

        def accumulate(h, alpha, p):
            acc_ref[h] = alpha * acc_ref[h] + _dot(vt_ref[h], p)

        s_next = logits(0)
        pending = None
        for h in range(H):
            s_cur = s_next
            if h + 1 < H:
                s_next = logits(h + 1)
            if pending is not None:
                accumulate(*pending)
            pending = (h,) + softmax(h, s_cur)
        accumulate(*pending)

    @pl.when(j < i)
    def _():
        step(False)

    @pl.when(j == i)
    def _():
        step(True)
        for h in range(H):
            o_ref[:, h * MLA_V:(h + 1) * MLA_V] = (acc_ref[h] / l_ref[h]).T.astype(BF16)


def _flash(qt, k, vt, tq=512):
    H, S, _ = k.shape
    nb = S // tq
    pairs = [(i, j) for i in range(nb) for j in range(i + 1)]
    it = jnp.asarray([p[0] for p in pairs], jnp.int32)
    jt = jnp.asarray([p[1] for p in pairs], jnp.int32)
    grid_spec = pltpu.PrefetchScalarGridSpec(
        num_scalar_prefetch=2,
        grid=(len(pairs),),
        in_specs=[
            pl.BlockSpec((H, MLA_PAD, tq), lambda t, it, jt: (0, 0, it[t])),
            pl.BlockSpec((H, tq, MLA_PAD), lambda t, it, jt: (0, jt[t], 0)),
            pl.BlockSpec((H, MLA_V, tq), lambda t, it, jt: (0, 0, jt[t])),
        ],
        out_specs=pl.BlockSpec((tq, H * MLA_V), lambda t, it, jt: (it[t], 0)),
        scratch_shapes=[pltpu.VMEM((H, 1, tq), F32), pltpu.VMEM((H, 1, tq), F32),
                        pltpu.VMEM((H, MLA_V, tq), F32)],
    )
    return pl.pallas_call(
        _flash_body,
        grid_spec=grid_spec,
        out_shape=jax.ShapeDtypeStruct((S, H * MLA_V), BF16),
        compiler_params=_cparams(("arbitrary",)),
        name="mla_flash",
    )(it, jt, qt, k, vt)


def _rwkv_body(*refs, mix_v):
    if mix_v:
        (r_ref, k_ref, v_ref, wl_ref, al_ref, gl_ref, mur, muk, muv, muwl, mual, mugl, w0, a0, w2, a2, g2,
         kk_ref, ka_ref, lnw_ref, lnb_ref, rk_ref, vf_ref, v0, v1, v2,
         o_ref, c_r, c_k, c_v, c_wl, c_al, c_gl, s_ref) = refs
    else:
        (r_ref, k_ref, v_ref, wl_ref, al_ref, gl_ref, mur, muk, muv, muwl, mual, mugl, w0, a0, w2, a2, g2,
         kk_ref, ka_ref, lnw_ref, lnb_ref, rk_ref,
         o_ref, vfirst_ref, c_r, c_k, c_v, c_wl, c_al, c_gl, s_ref) = refs
    tb = r_ref.shape[0]
    carries = (c_r, c_k, c_v, c_wl, c_al, c_gl)

    @pl.when(pl.program_id(0) == 0)
    def _():
        for c in carries:
            c[...] = jnp.zeros_like(c)

    def mixed(x_ref, c_ref, mu_ref):
        x = x_ref[...].astype(F32)
        prev = _shift_rows(x, c_ref[...], 1)
        c_ref[...] = x[tb - SUBLANES:, :]
        return x + (prev - x) * mu_ref[...]

    r = mixed(r_ref, c_r, mur)
    k = mixed(k_ref, c_k, muk)
    v = mixed(v_ref, c_v, muv)
    wl = mixed(wl_ref, c_wl, muwl)
    al = mixed(al_ref, c_al, mual)
    gl = mixed(gl_ref, c_gl, mugl)

    w_raw = w0[...] + _mm3(jnp.tanh(wl), w2[...])
    lw = -jnp.exp(-_softplus(-w_raw) - 0.5)
    a_sig = _sigmoid(a0[...] + _mm3(al, a2[...]))
    g = _mm1(_sigmoid(gl), g2[...])
    if mix_v:
        lora = _mm1(_mm1(v, v1[...]), v2[...])
        v = v + (vf_ref[...] - v) * _sigmoid(v0[...] + lora)
    else:
        vfirst_ref[...] = v
    row = lax.broadcasted_iota(jnp.int32, (LANES, LANES), 0) // RWKV_HEAD
    colm = lax.broadcasted_iota(jnp.int32, (LANES, LANES), 1) // RWKV_HEAD
    ones_bd = (row == colm).astype(BF16)
    kk = k * kk_ref[...]
    parts = []
    for p in range(RWKV_W // LANES):
        sl = slice(p * LANES, (p + 1) * LANES)
        kp = kk[:, sl]
        n2 = _seg_sum(kp * kp, ones_bd)
        parts.append(kp / jnp.maximum(jnp.sqrt(n2), 1e-12))
    kk = jnp.concatenate(parts, axis=1)
    k = k * (1.0 + (a_sig - 1.0) * ka_ref[...])
    _rwkv_chunks(r, lw, k, v, -kk, kk * a_sig, g, lnw_ref, lnb_ref, rk_ref, o_ref, s_ref)


def _rwkv(pin, tail, vec, mats, vfirst, vmix, layer, tb=ROW_BLOCK):
    S = pin.shape[0]
    W = RWKV_W
    lsel = lambda i: (layer, 0, 0)
    lsel1 = lambda i: (layer - 1, 0, 0)
    in_specs = [
        pl.BlockSpec((tb, W), lambda i: (i, P_WR // W)),
        pl.BlockSpec((tb, W), lambda i: (i, P_WK // W)),
        pl.BlockSpec((tb, W), lambda i: (i, P_WV // W)),
        pl.BlockSpec((tb, LANES), lambda i: (i, T_WL // LANES)),
        pl.BlockSpec((tb, LANES), lambda i: (i, T_AL // LANES)),
        pl.BlockSpec((tb, LANES), lambda i: (i, T_GL // LANES)),
    ]
    args = [pin] * 3 + [tail] * 3
    for name, width in (("mu_r", W), ("mu_k", W), ("mu_v", W), ("mu_wl", LANES), ("mu_al", LANES),
                        ("mu_gl", LANES), ("w0", W), ("a0", W)):
        in_specs.append(pl.BlockSpec((None, 1, width), lsel))
        args.append(vec[name])
    for name in ("w2", "a2", "g2"):
        in_specs.append(pl.BlockSpec((None, LANES, W), lsel))
        args.append(mats[name])
    for name in ("k_k", "k_a", "ln_w", "ln_b", "r_k"):
        in_specs.append(pl.BlockSpec((None, 1, W), lsel))
        args.append(vec[name])
    mix_v = vmix is not None
    if mix_v:
        v0, v1, v2 = vmix
        in_specs += [pl.BlockSpec((tb, W), lambda i: (i, 0)),
                     pl.BlockSpec((None, 1, W), lsel1),
                     pl.BlockSpec((None, W, LANES), lsel1),
                     pl.BlockSpec((None, LANES, W), lsel1)]
        args += [vfirst, v0, v1, v2]
    out_spec = pl.BlockSpec((tb, W), lambda i: (i, 0))
    out_specs = [out_spec] if mix_v else [out_spec, out_spec]
    out_shape = [jax.ShapeDtypeStruct((S, W), BF16)]
    if not mix_v:
        out_shape.append(jax.ShapeDtypeStruct((S, W), F32))
    outs = pl.pallas_call(
        functools.partial(_rwkv_body, mix_v=mix_v),
        grid=(S // tb,),
        in_specs=in_specs,
        out_specs=out_specs,
        out_shape=out_shape,
        scratch_shapes=[pltpu.VMEM((SUBLANES, W), F32)] * 3 + [pltpu.VMEM((SUBLANES, LANES), F32)] * 3
        + [pltpu.VMEM((W // LANES, LANES, LANES), F32)],
        compiler_params=_cparams(("arbitrary",)),
        name="rwkv",
    )(*args)
    return (outs[0], None) if mix_v else (outs[0], outs[1])


def _rwkv_chunks(r_ref, lw_ref, k_ref, v_ref, a_ref, b_ref, g_ref, lnw_ref, lnb_ref, rk_ref, o_ref, s_ref):
    L = RWKV_L
    N = RWKV_HEAD
    L2 = 2 * L
    tb = r_ref.shape[0]
    npair = RWKV_W // LANES

    @pl.when(pl.program_id(0) == 0)
    def _():
        s_ref[...] = jnp.zeros_like(s_ref)

    rb = lax.broadcasted_iota(jnp.int32, (tb, tb), 0)
    cb = lax.broadcasted_iota(jnp.int32, (tb, tb), 1)
    tril_bd = ((rb // L) == (cb // L)) & (cb <= rb)
    cum_blk = _mm_exact_lhs(tril_bd, lw_ref[...])

    lane = lax.broadcasted_iota(jnp.int32, (L, LANES), 1)
    m0 = lane < N
    r2 = lax.broadcasted_iota(jnp.int32, (L2, L2), 0)
    c2 = lax.broadcasted_iota(jnp.int32, (L2, L2), 1)
    same = (r2 // L) == (c2 // L)
    strict = same & (c2 < r2)
    incl = same & (c2 <= r2)
    eye = (r2 == c2).astype(F32)
    ones_bd = ((r2 // N) == (c2 // N)).astype(BF16)

    def stack(x):
        return jnp.concatenate([jnp.where(m0, x, 0.0), jnp.where(m0, 0.0, x)], axis=0).astype(BF16)

    probs = [(c, p) for c in range(tb // L) for p in range(npair)]
    rows_of = lambda c: slice(c * L, (c + 1) * L)
    lanes_of = lambda p: slice(p * LANES, (p + 1) * LANES)

    opnd = {}
    for (c, p) in probs:
        rows, sl = rows_of(c), lanes_of(p)
        lw = lw_ref[rows, sl]
        cum = cum_blk[rows, sl]
        cum_end = cum[L - 1:L, :]
        r = r_ref[rows, sl]
        k = k_ref[rows, sl]
        v = v_ref[rows, sl]
        a = a_ref[rows, sl]
        b = b_ref[rows, sl]
        e_neg = jnp.exp(-cum)
        e_end = jnp.exp(cum_end - cum)
        re = r * jnp.exp(cum)
        xr_f = jnp.concatenate([jnp.where(m0, re, 0.0), jnp.where(m0, 0.0, re)], axis=0)
        opnd[c, p] = dict(
            xr_f=xr_f, xa=stack(a * jnp.exp(cum - lw)), xr=xr_f.astype(BF16), xb=stack(b * e_neg),
            xk=stack(k * e_neg), xbh=stack(b * e_end), xkh=stack(k * e_end), vs=stack(v),
            decay_end=jnp.exp(cum_end), rk=r * k * rk_ref[:, sl], v=v)

    amat = {}
    for cp in probs:
        o = opnd[cp]
        pmat = _dot_nt(jnp.concatenate([o["xa"], o["xr"]], axis=0),
                       jnp.concatenate([o["xb"], o["xk"]], axis=0))
        amat[cp] = dict(
            ab=jnp.where(strict, pmat[:L2, :L2], 0.0),
            ak=jnp.where(strict, pmat[:L2, L2:], 0.0).astype(BF16),
            rb=jnp.where(incl, pmat[L2:, :L2], 0.0).astype(BF16),
            rk=jnp.where(incl, pmat[L2:, L2:], 0.0).astype(BF16))

    tinv = {cp: eye + amat[cp]["ab"] for cp in probs}
    pw = {cp: amat[cp]["ab"] for cp in probs}
    for _ in range(int(math.log2(L)) - 1):
        for cp in probs:
            pwb = pw[cp].astype(BF16)
            pw[cp] = _dot(pwb, pwb)
        for cp in probs:
            tinv[cp] = tinv[cp] + _mm1(tinv[cp], pw[cp])

    w1 = {cp: _dot(amat[cp]["ak"], opnd[cp]["vs"]) for cp in probs}
    mub = {}
    for cp in probs:
        rhs = jnp.concatenate([opnd[cp]["xa"], w1[cp].astype(BF16)], axis=1)
        mub[cp] = _dot(tinv[cp].astype(BF16), rhs).astype(BF16)
    coef = {}
    for cp in probs:
        o = opnd[cp]
        ry = _dot(amat[cp]["rb"], mub[cp])
        gh = _dot_tn(o["xbh"], mub[cp])
        coef[cp] = dict(
            m_r=(o["xr_f"] + ry[:, :LANES]).astype(BF16),
            y1=ry[:, LANES:] + _dot(amat[cp]["rk"], o["vs"]),
            g=(eye * o["decay_end"] + gh[:, :LANES]).astype(BF16),
            h=gh[:, LANES:] + _dot_tn(o["xkh"], o["vs"]))

    states = [s_ref[p] for p in range(npair)]
    ys = {}
    for (c, p) in probs:
        st = states[p].astype(BF16)
        cf = coef[c, p]
        yst = _dot(cf["m_r"], st) + cf["y1"]
        states[p] = _dot(cf["g"], st) + cf["h"]
        ys[c, p] = yst[:L] + yst[L:]
    for p in range(npair):
        s_ref[p] = states[p]

    yc = {cp: ys[cp] - _seg_sum(ys[cp], ones_bd) * (1.0 / N) for cp in probs}
    bonus = {cp: _seg_sum(opnd[cp]["rk"], ones_bd) * opnd[cp]["v"] for cp in probs}
    var = {cp: _seg_sum(yc[cp] * yc[cp], ones_bd) * (1.0 / N) for cp in probs}
    for (c, p) in probs:
        rows, sl = rows_of(c), lanes_of(p)
        yn = yc[c, p] * lax.rsqrt(var[c, p] + RWKV_GN_EPS) * lnw_ref[:, sl] + lnb_ref[:, sl]
        o_ref[rows, sl] = ((yn + bonus[c, p]) * g_ref[rows, sl]).astype(BF16)


def _merge_body(oa_ref, ob_ref, oc_ref, od_ref, g0_ref, g1_ref, g2_ref, g3_ref, wb_ref, out_ref):
    o_refs = (oa_ref, ob_ref, oc_ref, od_ref)
    g_refs = (g0_ref, g1_ref, g2_ref, g3_ref)
    acc = None
    for n in range(N_BRANCH):
        contrib = _sigmoid(g_refs[n][...].astype(F32)) * _dot(o_refs[n][...], wb_ref[n])
        acc = contrib if acc is None else acc + contrib
    out_ref[...] = acc.astype(BF16)


def _merge(o_a, o_b, o_c, o_d, pin, w_branch, layer, tm=512):
    S = pin.shape[0]
    D = D_MODEL
    ospec = pl.BlockSpec((tm, BRANCH_W), lambda i: (i, 0))
    gspec = lambda n: pl.BlockSpec((tm, D), lambda i: (i, n))
    return pl.pallas_call(
        _merge_body,
        grid=(S // tm,),
        in_specs=[ospec, ospec, ospec, ospec, gspec(0), gspec(1), gspec(2), gspec(3),
                  pl.BlockSpec((None, N_BRANCH, BRANCH_W, D), lambda i: (layer, 0, 0, 0))],
        out_specs=pl.BlockSpec((tm, D), lambda i: (i, 0)),
        out_shape=jax.ShapeDtypeStruct((S, D), BF16),
        compiler_params=_cparams(("arbitrary",)),
        name="merge",
    )(o_a, o_b, o_c, o_d, pin, pin, pin, pin, w_branch)


def _outproj_body(m_ref, w_ref, x_ref, o_ref):
    o_ref[...] = x_ref[...] + _dot(m_ref[...], w_ref[...])


def _outproj(merged, w_out, x, layer, tm=1024, tn=1024):
    S, D = x.shape
    return pl.pallas_call(
        _outproj_body,
        grid=(S // tm, D // tn),
        in_specs=[
            pl.BlockSpec((tm, D), lambda i, j: (i, 0)),
            pl.BlockSpec((None, D, tn), lambda i, j: (layer, 0, j)),
            pl.BlockSpec((tm, tn), lambda i, j: (i, j)),
        ],
        out_specs=pl.BlockSpec((tm, tn), lambda i, j: (i, j)),
        out_shape=jax.ShapeDtypeStruct((S, D), F32),
        compiler_params=_cparams(("arbitrary", "arbitrary")),
        name="outproj",
    )(merged, w_out, x)


def _ffn_body(x_ref, nw_ref, wg_ref, wu_ref, wd_ref, out_ref, h_ref, acc_ref):
    f = pl.program_id(1)

    @pl.when(f == 0)
    def _():
        x = x_ref[...]
        ms = jnp.mean(x * x, axis=-1, keepdims=True)
        h_ref[...] = (x * lax.rsqrt(ms + RMS_EPS) * nw_ref[...]).astype(BF16)
        acc_ref[...] = jnp.zeros_like(acc_ref)

    h = h_ref[...]
    act = (_silu(_dot(h, wg_ref[...])) * _dot(h, wu_ref[...])).astype(BF16)
    acc_ref[...] += _dot(act, wd_ref[...])

    @pl.when(f == pl.num_programs(1) - 1)
    def _():
        out_ref[...] = x_ref[...] + acc_ref[...]


def _ffn(x, norm_w, w_gu, w_down, layer, tm=512, tf=512):
    S, D = x.shape
    nf = D_FF // tf
    return pl.pallas_call(
        _ffn_body,
        grid=(S // tm, nf),
        in_specs=[
            pl.BlockSpec((tm, D), lambda i, f: (i, 0)),
            pl.BlockSpec((None, 1, D), lambda i, f: (layer, 0, 0)),
            pl.BlockSpec((None, D, tf), lambda i, f: (layer, 0, f)),
            pl.BlockSpec((None, D, tf), lambda i, f: (layer, 0, f + nf)),
            pl.BlockSpec((None, tf, D), lambda i, f: (layer, f, 0)),
        ],
        out_specs=pl.BlockSpec((tm, D), lambda i, f: (i, 0)),
        out_shape=jax.ShapeDtypeStruct((S, D), F32),
        scratch_shapes=[pltpu.VMEM((tm, D), BF16), pltpu.VMEM((tm, D), F32)],
        compiler_params=_cparams(("arbitrary", "arbitrary")),
        name="ffn",
    )(x, norm_w, w_gu, w_gu, w_down)


def _pad_last(a, width):
    pad = width - a.shape[-1]
    if pad == 0:
        return a
    return jnp.pad(a, [(0, 0)] * (a.ndim - 1) + [(0, pad)])


def _pad_axis(a, axis, width):
    pad = width - a.shape[axis]
    if pad == 0:
        return a
    cfg = [(0, 0)] * a.ndim
    cfg[axis] = (0, pad)
    return jnp.pad(a, cfg)


_MAIN_PIECES = (
    (_O_GATE, P_GATE, N_BRANCH * D_MODEL),
    (_O_RQ, P_RQ, 512), (_O_RK, P_RK, 512), (_O_RV, P_RV, 512), (_O_RG, P_RG, 512),
    (_O_SXBC, P_SXBC, SSM_XBC), (_O_SZ, P_SZ, 512),
    (_O_CQ, P_CQ, MLA_Q_RANK),
    (_O_RW, P_WR, RWKV_W), (_O_RW + RWKV_W, P_WK, RWKV_W), (_O_RW + 2 * RWKV_W, P_WV, RWKV_W),
    (_O_CKV, P_CKV, MLA_KV_RANK),
)
_MAIN_USED = P_CKV + MLA_KV_RANK
_TAIL_PIECES = (
    (_O_SDT, T_DT, SSM_HEADS),
    (_O_KROPE, T_KROPE, MLA_ROPE),
    (_O_RW + 3 * RWKV_W, T_WL, RWKV_W_LORA),
    (_O_RW + 3 * RWKV_W + RWKV_W_LORA, T_AL, RWKV_A_LORA),
    (_O_RW + 3 * RWKV_W + RWKV_W_LORA + RWKV_A_LORA, T_GL, RWKV_G_LORA),
)
PACK_STRIP = 1024


def _pack_body(w_ref, main_ref, tail_ref):
    for src, dst, width in _MAIN_PIECES:
        for off in range(0, width, PACK_STRIP):
            n = min(PACK_STRIP, width - off)
            main_ref[:, dst + off:dst + off + n] = w_ref[:, src + off:src + off + n].astype(BF16)
    main_ref[:, _MAIN_USED:] = jnp.zeros((main_ref.shape[0], N_MAIN - _MAIN_USED), BF16)
    tail_ref[...] = jnp.zeros_like(tail_ref)
    for src, dst, width in _TAIL_PIECES:
        tail_ref[:, dst:dst + width] = w_ref[:, src:src + width].astype(BF16)


def _pack_w_in(w_in, rows=128):
    depth, d, n_in = w_in.shape
    return pl.pallas_call(
        _pack_body,
        grid=(depth, d // rows),
        in_specs=[pl.BlockSpec((None, rows, n_in), lambda l, i: (l, i, 0))],
        out_specs=[pl.BlockSpec((None, rows, N_MAIN), lambda l, i: (l, i, 0)),
                   pl.BlockSpec((None, rows, N_TAIL), lambda l, i: (l, i, 0))],
        out_shape=[jax.ShapeDtypeStruct((depth, d, N_MAIN), BF16),
                   jax.ShapeDtypeStruct((depth, d, N_TAIL), BF16)],
        compiler_params=_cparams(("arbitrary", "arbitrary")),
        name="pack_w_in",
    )(w_in)


def _rope_tables(positions, dim):
    inv = 1.0 / (ROPE_THETA ** (jnp.arange(0, dim, 2, dtype=F32) / dim))
    ang = positions.astype(F32)[:, None] * inv
    return jnp.cos(ang), jnp.sin(ang)


def kernel(x, positions, norm1_w, w_in, ret_gn_w, ssm_conv_w, ssm_conv_b, ssm_dt_bias, ssm_a_log, ssm_d, ssm_norm_w, mla_q_a_norm_w, mla_w_qb, mla_kv_a_norm_w, mla_w_kvb, mla_q_norm_w, mla_k_norm_w, rwkv_mu, rwkv_w0, rwkv_w2, rwkv_a0, rwkv_a2, rwkv_g2, rwkv_v0, rwkv_v1, rwkv_v2, rwkv_k_k, rwkv_k_a, rwkv_r_k, rwkv_ln_w, rwkv_ln_b, w_branch, w_out, norm2_w, ffn_w_gu, ffn_w_down):
    B, S, D = x.shape
    assert B == 1 and D == D_MODEL and S % 1024 == 0
    xs = x[0]
    pos = positions[0]

    c_r, s_r = _rope_tables(pos, RET_DK)
    cos_ret = jnp.concatenate([c_r, c_r], axis=-1)
    sin_ret = jnp.concatenate([-s_r, s_r], axis=-1)
    c_m, s_m = _rope_tables(pos, MLA_ROPE)
    cos_mla = _pad_last(jnp.concatenate([c_m, c_m], axis=-1), LANES)
    sin_mla = _pad_last(jnp.concatenate([-s_m, s_m], axis=-1), LANES)

    row = lambda a: a[:, None, :]
    w_main, w_tail = _pack_w_in(w_in)
    norm1 = row(norm1_w)
    norm2 = row(norm2_w)
    gn_w = row(ret_gn_w)
    conv_b = row(ssm_conv_b)
    d_skip = row(ssm_d)
    ssm_nw = row(ssm_norm_w)
    dt_bias = row(_pad_last(ssm_dt_bias, LANES))
    a_neg = row(_pad_last(-jnp.exp(ssm_a_log.astype(F32)), LANES))
    qaw = row(mla_q_a_norm_w)
    kvaw = row(mla_kv_a_norm_w)
    wqb = _pad_last(mla_w_qb.reshape(DEPTH, MLA_Q_RANK, MLA_HEADS, MLA_QK), MLA_PAD)
    wqb = wqb.reshape(DEPTH, MLA_Q_RANK, MLA_HEADS * MLA_PAD).astype(BF16)
    wkvb = mla_w_kvb.astype(BF16)
    qnw = row(_pad_last(mla_q_norm_w, MLA_PAD))
    knw_n = row(mla_k_norm_w[:, :MLA_NOPE])
    knw_r = row(_pad_last(mla_k_norm_w[:, MLA_NOPE:], LANES))
    W = RWKV_W
    mu = rwkv_mu
    vec = {
        "mu_r": row(mu[:, :W]), "mu_k": row(mu[:, W:2 * W]), "mu_v": row(mu[:, 2 * W:3 * W]),
        "mu_wl": row(_pad_last(mu[:, 3 * W:3 * W + RWKV_W_LORA], LANES)),
        "mu_al": row(_pad_last(mu[:, 3 * W + RWKV_W_LORA:3 * W + RWKV_W_LORA + RWKV_A_LORA], LANES)),
        "mu_gl": row(mu[:, 3 * W + RWKV_W_LORA + RWKV_A_LORA:]),
        "w0": row(rwkv_w0), "a0": row(rwkv_a0), "k_k": row(rwkv_k_k), "k_a": row(rwkv_k_a),
        "ln_w": row(rwkv_ln_w), "ln_b": row(rwkv_ln_b), "r_k": row(rwkv_r_k.reshape(DEPTH, W)),
    }
    mats = {"w2": _pad_axis(rwkv_w2, 1, LANES), "a2": _pad_axis(rwkv_a2, 1, LANES), "g2": rwkv_g2}
    v0 = row(rwkv_v0)
    v1 = _pad_last(rwkv_v1, LANES)
    v2 = _pad_axis(rwkv_v2, 1, LANES)
    wb = w_branch.astype(BF16)
    wo = w_out.astype(BF16)
    wgu = ffn_w_gu.astype(BF16)
    wdn = ffn_w_down.astype(BF16)

    v_first = None
    for l in range(DEPTH):
        pin, tail = _inproj(xs, norm1, w_main, w_tail, l)
        o_a = _retention(pin, cos_ret, sin_ret, gn_w, l)
        o_b = _ssd(pin, tail, dt_bias, a_neg, ssm_conv_w, conv_b, d_skip, ssm_nw, l)
        q, k, v = _mla_prep(pin, tail, qaw, wqb, kvaw, wkvb, qnw, knw_n, knw_r, cos_mla, sin_mla, l)
        o_c = _flash(q, k, v)
        vmix = None if l == 0 else (v0, v1, v2)
        o_d, vf = _rwkv(pin, tail, vec, mats, v_first, vmix, l)
        if l == 0:
            v_first = vf
        merged = _merge(o_a, o_b, o_c, o_d, pin, wb, l)
        xs = _outproj(merged, wo, xs, l)
        xs = _ffn(xs, norm2, wgu, wdn, l)
    return xs[None]
```

```python
import functools
import math

import numpy as np
import jax
import jax.numpy as jnp
from jax import lax
from jax.experimental import pallas as pl
from jax.experimental.pallas import tpu as pltpu

F32 = jnp.float32
BF16 = jnp.bfloat16

D_MODEL = 2048
DEPTH = 4
CHUNK = 64
N_BRANCH = 4
BRANCH_W = 512
RMS_EPS = 1e-6
GN_EPS = 1e-5
ROPE_THETA = 10000.0
RET_HEADS, RET_DK, RET_DV = 4, 128, 128
SSM_HEADS, SSM_HEADDIM, SSM_GROUPS, SSM_STATE, SSM_CONV = 8, 64, 2, 128, 4
SSM_XBC = SSM_HEADS * SSM_HEADDIM + 2 * SSM_GROUPS * SSM_STATE
MLA_HEADS, MLA_Q_RANK, MLA_KV_RANK, MLA_NOPE, MLA_ROPE, MLA_V = 4, 512, 256, 128, 64, 128
MLA_QK = MLA_NOPE + MLA_ROPE
RWKV_HEADS, RWKV_HEAD = 8, 64
RWKV_W = RWKV_HEADS * RWKV_HEAD
RWKV_W_LORA, RWKV_A_LORA, RWKV_V_LORA, RWKV_G_LORA = 64, 64, 32, 128
RWKV_GN_EPS = 64e-5
D_FF = 5632

LANES = 128
SUBLANES = 8
VMEM_LIMIT_BYTES = 56 * 1024 * 1024

_O_RQ, _O_RK, _O_RV, _O_RG = 0, 512, 1024, 1536
_O_SZ, _O_SXBC, _O_SDT = 2048, 2560, 3584
_O_CQ, _O_CKV, _O_KROPE = 3592, 4104, 4360
_O_RW = 4424
_O_GATE = 6216
P_GATE = 0
P_RQ, P_RK, P_RV, P_RG = 8192, 8704, 9216, 9728
P_SXBC, P_SZ = 10240, 11264
P_CQ = 11776
P_WR, P_WK, P_WV = 12288, 12800, 13312
P_CKV = 13824
N_MAIN = 14336
T_DT, T_KROPE, T_WL, T_AL, T_GL = 0, 128, 256, 384, 512
N_TAIL = 640

ROW_BLOCK = 256
RWKV_L = 64
LOG2E = 1.4426950408889634


def _cparams(sem):
    return pltpu.CompilerParams(dimension_semantics=sem, vmem_limit_bytes=VMEM_LIMIT_BYTES)


def _dot(a, b):
    return jnp.dot(a, b, preferred_element_type=F32)


def _dot_nt(a, b):
    return lax.dot_general(a, b, (((1,), (1,)), ((), ())), preferred_element_type=F32)


def _dot_tn(a, b):
    return lax.dot_general(a, b, (((0,), (0,)), ((), ())), preferred_element_type=F32)


def _split_bf16(x):
    hi = x.astype(BF16)
    lo = (x - hi.astype(F32)).astype(BF16)
    return hi, lo


def _mm3(a, b, dot=_dot):
    ah, al = _split_bf16(a)
    bh, bl = _split_bf16(b)
    return dot(ah, bh) + dot(ah, bl) + dot(al, bh)


def _mm1(a, b, dot=_dot):
    return dot(a.astype(BF16), b.astype(BF16))


def _mm_exact_lhs(a01, x, dot=_dot):
    xh, xl = _split_bf16(x)
    a = a01.astype(BF16)
    return dot(a, xh) + dot(a, xl)


def _seg_sum(x, ones_bd):
    xh, xl = _split_bf16(x)
    return _dot(xh, ones_bd) + _dot(xl, ones_bd)


def _sigmoid(x):
    return 0.5 * jnp.tanh(0.5 * x) + 0.5


def _silu(x):
    return x * _sigmoid(x)


def _softplus(x):
    return jnp.maximum(x, 0.0) + jnp.log(1.0 + jnp.exp(-jnp.abs(x)))


def _shift_rows(x, carry8, s):
    xr = pltpu.roll(x, s, 0)
    pr = pltpu.roll(carry8, s, 0)
    row = lax.broadcasted_iota(jnp.int32, carry8.shape, 0)
    top = jnp.where(row < s, pr, xr[:SUBLANES])
    return jnp.concatenate([top, xr[SUBLANES:]], axis=0)


def _inproj_body(x_ref, nw_ref, w_ref, wt_ref, o_ref, ot_ref, h_ref):
    @pl.when(pl.program_id(1) == 0)
    def _():
        x = x_ref[...]
        ms = jnp.mean(x * x, axis=-1, keepdims=True)
        h = (x * lax.rsqrt(ms + RMS_EPS) * nw_ref[...]).astype(BF16)
        h_ref[...] = h
        ot_ref[...] = _dot_nt(h, wt_ref[...])

    o_ref[...] = _dot_nt(h_ref[...], w_ref[...]).astype(BF16)


def _inproj(x, norm_w, w_main, w_tail, layer, tm=1024, tn=1024):
    S, D = x.shape
    return pl.pallas_call(
        _inproj_body,
        grid=(S // tm, N_MAIN // tn),
        in_specs=[
            pl.BlockSpec((tm, D), lambda i, j: (i, 0)),
            pl.BlockSpec((None, 1, D), lambda i, j: (layer, 0, 0)),
            pl.BlockSpec((None, tn, D), lambda i, j: (layer, j, 0)),
            pl.BlockSpec((None, N_TAIL, D), lambda i, j: (layer, 0, 0)),
        ],
        out_specs=[pl.BlockSpec((tm, tn), lambda i, j: (i, j)),
                   pl.BlockSpec((tm, N_TAIL), lambda i, j: (i, 0))],
        out_shape=[jax.ShapeDtypeStruct((S, N_MAIN), BF16), jax.ShapeDtypeStruct((S, N_TAIL), F32)],
        scratch_shapes=[pltpu.VMEM((tm, D), BF16)],
        compiler_params=_cparams(("arbitrary", "arbitrary")),
        name="inproj",
    )(x, norm_w, w_main, w_tail)


def _ret_body(q_ref, k_ref, v_ref, g_ref, cos_ref, sin_ref, m_ref, qd_ref, kd_ref, gnw_ref,
              o_ref, s_ref, *, block_decay):
    @pl.when(pl.program_id(0) == 0)
    def _():
        s_ref[...] = jnp.zeros_like(s_ref)

    cos = cos_ref[...]
    sin = sin_ref[...]
    heads = range(RET_HEADS)
    lanes = [slice(h * RET_DK, (h + 1) * RET_DK) for h in heads]
    q, k, vb = [], [], []
    for h in heads:
        qh = q_ref[:, lanes[h]].astype(F32)
        kh = k_ref[:, lanes[h]].astype(F32)
        q.append(qh * cos + pltpu.roll(qh, RET_DK // 2, 1) * sin)
        k.append((kh * cos + pltpu.roll(kh, RET_DK // 2, 1) * sin) * (RET_DK ** -0.5))
        vb.append(v_ref[:, lanes[h]])
    sc = [(_dot_nt(q[h].astype(BF16), k[h].astype(BF16)) * m_ref[h]).astype(BF16) for h in heads]
    st = [s_ref[h] for h in heads]
    o = [_dot(sc[h], vb[h]) + _dot((q[h] * qd_ref[h]).astype(BF16), st[h].astype(BF16)) for h in heads]
    for h in heads:
        s_ref[h] = block_decay[h] * st[h] + _dot_tn((k[h] * kd_ref[h]).astype(BF16), vb[h])
    oc = [o[h] - jnp.mean(o[h], axis=-1, keepdims=True) for h in heads]
    var = [jnp.mean(oc[h] * oc[h], axis=-1, keepdims=True) for h in heads]
    for h in heads:
        on = oc[h] * lax.rsqrt(var[h] + GN_EPS) * gnw_ref[:, lanes[h]]
        o_ref[:, lanes[h]] = (_silu(g_ref[:, lanes[h]].astype(F32)) * on).astype(BF16)


def _ret_tables(tb):
    lg = np.log1p(-np.exp2(-5.0 - np.arange(RET_HEADS, dtype=np.float64)))
    pos = np.arange(tb, dtype=np.float64)
    dist = np.abs(pos[:, None] - pos[None, :])
    visible = (pos[None, :] // CHUNK) <= (pos[:, None] // CHUNK)
    mask = np.where(visible[None], np.exp(lg[:, None, None] * dist[None]), 0.0)
    qd = np.exp(lg[:, None] * (pos[None, :] + 1.0))[:, :, None]
    kd = np.exp(lg[:, None] * (tb - 1.0 - pos[None, :]))[:, :, None]
    bd = tuple(float(v) for v in np.exp(lg * tb))
    return (jnp.asarray(mask, F32), jnp.asarray(qd, F32), jnp.asarray(kd, F32), bd)


def _retention(pin, cos2, sin2, gn_w, layer, tb=ROW_BLOCK):
    S = pin.shape[0]
    mask, qd, kd, bd = _ret_tables(tb)
    w = RET_HEADS * RET_DK
    col = lambda off: (lambda i: (i, off // w))
    full3 = lambda i: (0, 0, 0)
    return pl.pallas_call(
        functools.partial(_ret_body, block_decay=bd),
        grid=(S // tb,),
        in_specs=[
            pl.BlockSpec((tb, w), col(P_RQ)),
            pl.BlockSpec((tb, w), col(P_RK)),
            pl.BlockSpec((tb, w), col(P_RV)),
            pl.BlockSpec((tb, w), col(P_RG)),
            pl.BlockSpec((tb, RET_DK), lambda i: (i, 0)),
            pl.BlockSpec((tb, RET_DK), lambda i: (i, 0)),
            pl.BlockSpec((RET_HEADS, tb, tb), full3),
            pl.BlockSpec((RET_HEADS, tb, 1), full3),
            pl.BlockSpec((RET_HEADS, tb, 1), full3),
            pl.BlockSpec((None, 1, w), lambda i: (layer, 0, 0)),
        ],
        out_specs=pl.BlockSpec((tb, w), lambda i: (i, 0)),
        out_shape=jax.ShapeDtypeStruct((S, w), BF16),
        scratch_shapes=[pltpu.VMEM((RET_HEADS, RET_DK, RET_DV), F32)],
        compiler_params=_cparams(("arbitrary",)),
        name="retention",
    )(pin, pin, pin, pin, cos2, sin2, mask, qd, kd, gn_w)


def _ssd_body(xbc_ref, z_ref, dt_ref, dtb_ref, aneg_ref, cw_ref, cb_ref, d_ref, nw_ref,
              o_ref, carry_ref, st_ref):
    tb = xbc_ref.shape[0]
    P, N = SSM_HEADDIM, SSM_STATE
    nx = SSM_HEADS * P

    @pl.when(pl.program_id(0) == 0)
    def _():
        carry_ref[...] = jnp.zeros_like(carry_ref)
        st_ref[...] = jnp.zeros_like(st_ref)

    x = xbc_ref[...].astype(F32)
    carry = carry_ref[...]
    acc = x * cw_ref[SSM_CONV - 1:SSM_CONV, :] + cb_ref[...]
    for s in range(1, SSM_CONV):
        acc = acc + _shift_rows(x, carry, s) * cw_ref[SSM_CONV - 1 - s:SSM_CONV - s, :]
    carry_ref[...] = x[tb - SUBLANES:, :]
    xbc = _silu(acc)

    dt = _softplus(dt_ref[...] + dtb_ref[...])
    adt = dt * aneg_ref[...]
    row = lax.broadcasted_iota(jnp.int32, (tb, tb), 0)
    colm = lax.broadcasted_iota(jnp.int32, (tb, tb), 1)
    causal = colm <= row
    tril = causal.astype(F32)
    a_col = jnp.dot(tril, adt, preferred_element_type=F32, precision=lax.Precision.HIGHEST)
    triu = (row <= colm).astype(F32)
    a_row = lax.dot_general(adt, triu, (((0,), (0,)), ((), ())),
                            preferred_element_type=F32, precision=lax.Precision.HIGHEST)
    a_end = a_col[tb - 1:tb, :]
    lane = lax.broadcasted_iota(jnp.int32, (tb, 2 * P), 1)
    first = lane < P

    groups = range(SSM_GROUPS)
    heads = range(SSM_HEADS)
    hpg = SSM_HEADS // SSM_GROUPS
    npair = SSM_HEADS // 2
    bm = [xbc[:, nx + g * N: nx + (g + 1) * N] for g in groups]
    cm = [xbc[:, nx + SSM_GROUPS * N + g * N: nx + SSM_GROUPS * N + (g + 1) * N] for g in groups]
    cb = [_dot_nt(cm[g].astype(BF16), bm[g].astype(BF16)) for g in groups]
    xp = [xbc[:, 2 * q * P:(2 * q + 2) * P] for q in range(npair)]
    xdt = [(xp[q] * jnp.where(first, dt[:, 2 * q:2 * q + 1], dt[:, 2 * q + 1:2 * q + 2])).astype(BF16)
           for q in range(npair)]
    st = [st_ref[q] for q in range(npair)]
    ac = [a_col[:, h:h + 1] for h in heads]
    dec = [jnp.exp(jnp.where(causal, ac[h] - a_row[h:h + 1, :], -jnp.inf)) for h in heads]
    wts = [(cb[h // hpg] * dec[h]).astype(BF16) for h in heads]
    y = [_dot(wts[h], xdt[h // 2]) for h in heads]
    cdec = [(cm[h // hpg] * jnp.exp(ac[h])).astype(BF16) for h in heads]
    y = [y[h] + _dot(cdec[h], st[h // 2].astype(BF16)) for h in heads]
    bdec = [(bm[h // hpg] * jnp.exp(a_end[:, h:h + 1] - ac[h])).astype(BF16) for h in heads]
    snew = [jnp.exp(a_end[:, h:h + 1]) * st[h // 2] + _dot_tn(bdec[h], xdt[h // 2]) for h in heads]
    lane_s = lax.broadcasted_iota(jnp.int32, (N, 2 * P), 1)
    for q in range(npair):
        st_ref[q] = jnp.where(lane_s < P, snew[2 * q], snew[2 * q + 1])
    ypair = [jnp.where(first, y[2 * q], y[2 * q + 1])
             + xp[q] * jnp.where(first, d_ref[:, 2 * q:2 * q + 1], d_ref[:, 2 * q + 1:2 * q + 2])
             for q in range(npair)]
    for g in groups:
        yg = jnp.concatenate(ypair[g * hpg // 2:(g + 1) * hpg // 2], axis=1)
        gsl = slice(g * hpg * P, (g + 1) * hpg * P)
        yg = yg * _silu(z_ref[:, gsl].astype(F32))
        ms = jnp.mean(yg * yg, axis=-1, keepdims=True)
        o_ref[:, gsl] = (yg * lax.rsqrt(ms + RMS_EPS) * nw_ref[:, gsl]).astype(BF16)


def _ssd(pin, tail, dt_bias, a_neg, conv_w, conv_b, d_skip, norm_w, layer, tb=ROW_BLOCK):
    S = pin.shape[0]
    nx = SSM_HEADS * SSM_HEADDIM
    H = SSM_HEADS
    lsel = lambda i: (layer, 0, 0)
    return pl.pallas_call(
        _ssd_body,
        grid=(S // tb,),
        in_specs=[
            pl.BlockSpec((tb, SSM_XBC), lambda i: (i, P_SXBC // SSM_XBC)),
            pl.BlockSpec((tb, nx), lambda i: (i, P_SZ // nx)),
            pl.BlockSpec((tb, LANES), lambda i: (i, T_DT // LANES)),
            pl.BlockSpec((None, 1, LANES), lsel),
            pl.BlockSpec((None, 1, LANES), lsel),
            pl.BlockSpec((None, SSM_CONV, SSM_XBC), lsel),
            pl.BlockSpec((None, 1, SSM_XBC), lsel),
            pl.BlockSpec((None, 1, H), lsel),
            pl.BlockSpec((None, 1, nx), lsel),
        ],
        out_specs=pl.BlockSpec((tb, nx), lambda i: (i, 0)),
        out_shape=jax.ShapeDtypeStruct((S, nx), BF16),
        scratch_shapes=[pltpu.VMEM((SUBLANES, SSM_XBC), F32),
                        pltpu.VMEM((H // 2, SSM_STATE, 2 * SSM_HEADDIM), F32)],
        compiler_params=_cparams(("arbitrary",)),
        name="ssd",
    )(pin, pin, tail, dt_bias, a_neg, conv_w, conv_b, d_skip, norm_w)


MLA_PAD = 256


def _rope_pad(x, cosp, sinp):
    half = MLA_ROPE // 2
    return x * cosp + (pltpu.roll(x, half, 1) + pltpu.roll(x, LANES - half, 1)) * sinp


def _mla_prep_body(cq_ref, ckv_ref, kr_ref, qaw_ref, wqb_ref, kvaw_ref, wkvb_ref, qnw_ref,
                   knw_n_ref, knw_r_ref, cos_ref, sin_ref, q_out, k_out, v_out):
    cosp = cos_ref[...]
    sinp = sin_ref[...]
    cq = cq_ref[...].astype(F32)
    ms = jnp.mean(cq * cq, axis=-1, keepdims=True)
    cqn = (cq * lax.rsqrt(ms + RMS_EPS) * qaw_ref[...]).astype(BF16)
    q = _dot(cqn, wqb_ref[...])
    ckv = ckv_ref[...].astype(F32)
    ms = jnp.mean(ckv * ckv, axis=-1, keepdims=True)
    ckvn = (ckv * lax.rsqrt(ms + RMS_EPS) * kvaw_ref[...]).astype(BF16)
    kv = _dot(ckvn, wkvb_ref[...])
    kr = kr_ref[...]
    ssr = jnp.sum(kr * kr, axis=-1, keepdims=True)
    scale = (MLA_QK ** -0.5) * LOG2E
    for h in range(MLA_HEADS):
        qh = q[:, h * MLA_PAD:(h + 1) * MLA_PAD]
        msq = jnp.sum(qh * qh, axis=-1, keepdims=True) * (1.0 / MLA_QK)
        qh = qh * lax.rsqrt(msq + RMS_EPS) * qnw_ref[...]
        qr = _rope_pad(qh[:, MLA_NOPE:], cosp, sinp)
        q_out[h] = (jnp.concatenate([qh[:, :MLA_NOPE], qr], axis=1) * scale).T.astype(BF16)
        kn = kv[:, h * MLA_PAD: h * MLA_PAD + MLA_NOPE]
        vv = kv[:, h * MLA_PAD + MLA_NOPE:(h + 1) * MLA_PAD]
        msk = (jnp.sum(kn * kn, axis=-1, keepdims=True) + ssr) * (1.0 / MLA_QK)
        rinv = lax.rsqrt(msk + RMS_EPS)
        krh = _rope_pad(kr * rinv * knw_r_ref[...], cosp, sinp)
        k_out[h] = jnp.concatenate([kn * rinv * knw_n_ref[...], krh], axis=1).astype(BF16)
        v_out[h] = vv.T.astype(BF16)


def _mla_prep(pin, tail, qaw, wqb, kvaw, wkvb, qnw, knw_n, knw_r, cosp, sinp, layer, tb=ROW_BLOCK):
    S = pin.shape[0]
    H = MLA_HEADS
    lsel = lambda i: (layer, 0, 0)
    return pl.pallas_call(
        _mla_prep_body,
        grid=(S // tb,),
        in_specs=[
            pl.BlockSpec((tb, MLA_Q_RANK), lambda i: (i, P_CQ // MLA_Q_RANK)),
            pl.BlockSpec((tb, MLA_KV_RANK), lambda i: (i, P_CKV // MLA_KV_RANK)),
            pl.BlockSpec((tb, LANES), lambda i: (i, T_KROPE // LANES)),
            pl.BlockSpec((None, 1, MLA_Q_RANK), lsel),
            pl.BlockSpec((None, MLA_Q_RANK, H * MLA_PAD), lsel),
            pl.BlockSpec((None, 1, MLA_KV_RANK), lsel),
            pl.BlockSpec((None, MLA_KV_RANK, H * MLA_PAD), lsel),
            pl.BlockSpec((None, 1, MLA_PAD), lsel),
            pl.BlockSpec((None, 1, MLA_NOPE), lsel),
            pl.BlockSpec((None, 1, LANES), lsel),
            pl.BlockSpec((tb, LANES), lambda i: (i, 0)),
            pl.BlockSpec((tb, LANES), lambda i: (i, 0)),
        ],
        out_specs=[
            pl.BlockSpec((H, MLA_PAD, tb), lambda i: (0, 0, i)),
            pl.BlockSpec((H, tb, MLA_PAD), lambda i: (0, i, 0)),
            pl.BlockSpec((H, MLA_V, tb), lambda i: (0, 0, i)),
        ],
        out_shape=[
            jax.ShapeDtypeStruct((H, MLA_PAD, S), BF16),
            jax.ShapeDtypeStruct((H, S, MLA_PAD), BF16),
            jax.ShapeDtypeStruct((H, MLA_V, S), BF16),
        ],
        compiler_params=_cparams(("arbitrary",)),
        name="mla_prep",
    )(pin, pin, tail, qaw, wqb, kvaw, wkvb, qnw, knw_n, knw_r, cosp, sinp)


def _flash_body(it_ref, jt_ref, qt_ref, k_ref, vt_ref, o_ref, m_ref, l_ref, acc_ref):
    t = pl.program_id(0)
    i = it_ref[t]
    j = jt_ref[t]
    tq = qt_ref.shape[2]
    tk = k_ref.shape[1]
    ratio = tq // tk
    H = MLA_HEADS

    @pl.when(j == 0)
    def _():
        m_ref[...] = jnp.full_like(m_ref, -jnp.inf)
        l_ref[...] = jnp.zeros_like(l_ref)
        acc_ref[...] = jnp.zeros_like(acc_ref)

    def step(diagonal):
        if diagonal:
            kc = (j * tk + lax.broadcasted_iota(jnp.int32, (tk, tq), 0)) // CHUNK
            qc = (i * tq + lax.broadcasted_iota(jnp.int32, (tk, tq), 1)) // CHUNK
            visible = kc <= qc

        def logits(h):
            s = _dot(k_ref[h], qt_ref[h])
            return jnp.where(visible, s, -jnp.inf) if diagonal else s

        def softmax(h, s):
            m_prev = m_ref[h]
            m_new = jnp.maximum(m_prev, jnp.max(s, axis=0, keepdims=True))
            alpha = jnp.exp2(m_prev - m_new)
            p = jnp.exp2(s - m_new)
            l_ref[h] = alpha * l_ref[h] + jnp.sum(p, axis=0, keepdims=True)
            m_ref[h] = m_new
            return alpha, p.astype(BF16)

        def accumulate(h, alpha, p):
            acc_ref[h] = alpha * acc_ref[h] + _dot(vt_ref[h], p)

        s_next = logits(0)
        pending = None
        for h in range(H):
            s_cur = s_next
            if h + 1 < H:
                s_next = logits(h + 1)
            if pending is not None:
                accumulate(*pending)
            pending = (h,) + softmax(h, s_cur)
        accumulate(*pending)

    @pl.when(j < ratio * i)
    def _():
        step(False)

    @pl.when(j >= ratio * i)
    def _():
        step(True)

    @pl.when(j == ratio * (i + 1) - 1)
    def _():
        for h in range(H):
            o_ref[:, h * MLA_V:(h + 1) * MLA_V] = (acc_ref[h] / l_ref[h]).T.astype(BF16)


def _flash(qt, k, vt, tq=512, tk=512):
    H, S, _ = k.shape
    ratio = tq // tk
    pairs = [(i, j) for i in range(S // tq) for j in range(ratio * (i + 1))]
    it = jnp.asarray([p[0] for p in pairs], jnp.int32)
    jt = jnp.asarray([p[1] for p in pairs], jnp.int32)
    grid_spec = pltpu.PrefetchScalarGridSpec(
        num_scalar_prefetch=2,
        grid=(len(pairs),),
        in_specs=[
            pl.BlockSpec((H, MLA_PAD, tq), lambda t, it, jt: (0, 0, it[t])),
            pl.BlockSpec((H, tk, MLA_PAD), lambda t, it, jt: (0, jt[t], 0)),
            pl.BlockSpec((H, MLA_V, tk), lambda t, it, jt: (0, 0, jt[t])),
        ],
        out_specs=pl.BlockSpec((tq, H * MLA_V), lambda t, it, jt: (it[t], 0)),
        scratch_shapes=[pltpu.VMEM((H, 1, tq), F32), pltpu.VMEM((H, 1, tq), F32),
                        pltpu.VMEM((H, MLA_V, tq), F32)],
    )
    return pl.pallas_call(
        _flash_body,
        grid_spec=grid_spec,
        out_shape=jax.ShapeDtypeStruct((S, H * MLA_V), BF16),
        compiler_params=_cparams(("arbitrary",)),
        name="mla_flash",
    )(it, jt, qt, k, vt)


def _rwkv_body(*refs, mix_v):
    if mix_v:
        (r_ref, k_ref, v_ref, wl_ref, al_ref, gl_ref, mur, muk, muv, muwl, mual, mugl, w0, a0, w2, a2, g2,
         kk_ref, ka_ref, lnw_ref, lnb_ref, rk_ref, vf_ref, v0, v1, v2,
         o_ref, c_r, c_k, c_v, c_wl, c_al, c_gl, s_ref) = refs
    else:
        (r_ref, k_ref, v_ref, wl_ref, al_ref, gl_ref, mur, muk, muv, muwl, mual, mugl, w0, a0, w2, a2, g2,
         kk_ref, ka_ref, lnw_ref, lnb_ref, rk_ref,
         o_ref, vfirst_ref, c_r, c_k, c_v, c_wl, c_al, c_gl, s_ref) = refs
    tb = r_ref.shape[0]
    carries = (c_r, c_k, c_v, c_wl, c_al, c_gl)

    @pl.when(pl.program_id(0) == 0)
    def _():
        for c in carries:
            c[...] = jnp.zeros_like(c)

    def mixed(x_ref, c_ref, mu_ref):
        x = x_ref[...].astype(F32)
        prev = _shift_rows(x, c_ref[...], 1)
        c_ref[...] = x[tb - SUBLANES:, :]
        return x + (prev - x) * mu_ref[...]

    r = mixed(r_ref, c_r, mur)
    k = mixed(k_ref, c_k, muk)
    v = mixed(v_ref, c_v, muv)
    wl = mixed(wl_ref, c_wl, muwl)
    al = mixed(al_ref, c_al, mual)
    gl = mixed(gl_ref, c_gl, mugl)

    w_raw = w0[...] + _mm3(jnp.tanh(wl), w2[...])
    lw = -jnp.exp(-_softplus(-w_raw) - 0.5)
    a_sig = _sigmoid(a0[...] + _mm3(al, a2[...]))
    g = _mm1(_sigmoid(gl), g2[...])
    if mix_v:
        lora = _mm1(_mm1(v, v1[...]), v2[...])
        v = v + (vf_ref[...] - v) * _sigmoid(v0[...] + lora)
    else:
        vfirst_ref[...] = v
    row = lax.broadcasted_iota(jnp.int32, (LANES, LANES), 0) // RWKV_HEAD
    colm = lax.broadcasted_iota(jnp.int32, (LANES, LANES), 1) // RWKV_HEAD
    ones_bd = (row == colm).astype(BF16)
    kk = k * kk_ref[...]
    parts = []
    for p in range(RWKV_W // LANES):
        sl = slice(p * LANES, (p + 1) * LANES)
        kp = kk[:, sl]
        n2 = _seg_sum(kp * kp, ones_bd)
        parts.append(kp / jnp.maximum(jnp.sqrt(n2), 1e-12))
    kk = jnp.concatenate(parts, axis=1)
    k = k * (1.0 + (a_sig - 1.0) * ka_ref[...])
    _rwkv_chunks(r, lw, k, v, -kk, kk * a_sig, g, lnw_ref, lnb_ref, rk_ref, o_ref, s_ref)


def _rwkv(pin, tail, vec, mats, vfirst, vmix, layer, tb=ROW_BLOCK):
    S = pin.shape[0]
    W = RWKV_W
    lsel = lambda i: (layer, 0, 0)
    lsel1 = lambda i: (layer - 1, 0, 0)
    in_specs = [
        pl.BlockSpec((tb, W), lambda i: (i, P_WR // W)),
        pl.BlockSpec((tb, W), lambda i: (i, P_WK // W)),
        pl.BlockSpec((tb, W), lambda i: (i, P_WV // W)),
        pl.BlockSpec((tb, LANES), lambda i: (i, T_WL // LANES)),
        pl.BlockSpec((tb, LANES), lambda i: (i, T_AL // LANES)),
        pl.BlockSpec((tb, LANES), lambda i: (i, T_GL // LANES)),
    ]
    args = [pin] * 3 + [tail] * 3
    for name, width in (("mu_r", W), ("mu_k", W), ("mu_v", W), ("mu_wl", LANES), ("mu_al", LANES),
                        ("mu_gl", LANES), ("w0", W), ("a0", W)):
        in_specs.append(pl.BlockSpec((None, 1, width), lsel))
        args.append(vec[name])
    for name in ("w2", "a2", "g2"):
        in_specs.append(pl.BlockSpec((None, LANES, W), lsel))
        args.append(mats[name])
    for name in ("k_k", "k_a", "ln_w", "ln_b", "r_k"):
        in_specs.append(pl.BlockSpec((None, 1, W), lsel))
        args.append(vec[name])
    mix_v = vmix is not None
    if mix_v:
        v0, v1, v2 = vmix
        in_specs += [pl.BlockSpec((tb, W), lambda i: (i, 0)),
                     pl.BlockSpec((None, 1, W), lsel1),
                     pl.BlockSpec((None, W, LANES), lsel1),
                     pl.BlockSpec((None, LANES, W), lsel1)]
        args += [vfirst, v0, v1, v2]
    out_spec = pl.BlockSpec((tb, W), lambda i: (i, 0))
    out_specs = [out_spec] if mix_v else [out_spec, out_spec]
    out_shape = [jax.ShapeDtypeStruct((S, W), BF16)]
    if not mix_v:
        out_shape.append(jax.ShapeDtypeStruct((S, W), F32))
    outs = pl.pallas_call(
        functools.partial(_rwkv_body, mix_v=mix_v),
        grid=(S // tb,),
        in_specs=in_specs,
        out_specs=out_specs,
        out_shape=out_shape,
        scratch_shapes=[pltpu.VMEM((SUBLANES, W), F32)] * 3 + [pltpu.VMEM((SUBLANES, LANES), F32)] * 3
        + [pltpu.VMEM((W // LANES, LANES, LANES), F32)],
        compiler_params=_cparams(("arbitrary",)),
        name="rwkv",
    )(*args)
    return (outs[0], None) if mix_v else (outs[0], outs[1])


def _rwkv_chunks(r_ref, lw_ref, k_ref, v_ref, a_ref, b_ref, g_ref, lnw_ref, lnb_ref, rk_ref, o_ref, s_ref):
    L = RWKV_L
    N = RWKV_HEAD
    L2 = 2 * L
    tb = r_ref.shape[0]
    npair = RWKV_W // LANES

    @pl.when(pl.program_id(0) == 0)
    def _():
        s_ref[...] = jnp.zeros_like(s_ref)

    rb = lax.broadcasted_iota(jnp.int32, (tb, tb), 0)
    cb = lax.broadcasted_iota(jnp.int32, (tb, tb), 1)
    tril_bd = ((rb // L) == (cb // L)) & (cb <= rb)
    cum_blk = _mm_exact_lhs(tril_bd, lw_ref[...])

    lane = lax.broadcasted_iota(jnp.int32, (L, LANES), 1)
    m0 = lane < N
    r2 = lax.broadcasted_iota(jnp.int32, (L2, L2), 0)
    c2 = lax.broadcasted_iota(jnp.int32, (L2, L2), 1)
    same = (r2 // L) == (c2 // L)
    strict = same & (c2 < r2)
    incl = same & (c2 <= r2)
    eye = (r2 == c2).astype(F32)
    ones_bd = ((r2 // N) == (c2 // N)).astype(BF16)

    def stack(x):
        return jnp.concatenate([jnp.where(m0, x, 0.0), jnp.where(m0, 0.0, x)], axis=0).astype(BF16)

    probs = [(c, p) for c in range(tb // L) for p in range(npair)]
    rows_of = lambda c: slice(c * L, (c + 1) * L)
    lanes_of = lambda p: slice(p * LANES, (p + 1) * LANES)

    opnd = {}
    for (c, p) in probs:
        rows, sl = rows_of(c), lanes_of(p)
        lw = lw_ref[rows, sl]
        cum = cum_blk[rows, sl]
        cum_end = cum[L - 1:L, :]
        r = r_ref[rows, sl]
        k = k_ref[rows, sl]
        v = v_ref[rows, sl]
        a = a_ref[rows, sl]
        b = b_ref[rows, sl]
        e_neg = jnp.exp(-cum)
        e_end = jnp.exp(cum_end - cum)
        re = r * jnp.exp(cum)
        xr_f = jnp.concatenate([jnp.where(m0, re, 0.0), jnp.where(m0, 0.0, re)], axis=0)
        opnd[c, p] = dict(
            xr_f=xr_f, xa=stack(a * jnp.exp(cum - lw)), xr=xr_f.astype(BF16), xb=stack(b * e_neg),
            xk=stack(k * e_neg), xbh=stack(b * e_end), xkh=stack(k * e_end), vs=stack(v),
            decay_end=jnp.exp(cum_end), rk=r * k * rk_ref[:, sl], v=v)

    amat = {}
    for cp in probs:
        o = opnd[cp]
        pmat = _dot_nt(jnp.concatenate([o["xa"], o["xr"]], axis=0),
                       jnp.concatenate([o["xb"], o["xk"]], axis=0))
        amat[cp] = dict(
            ab=jnp.where(strict, pmat[:L2, :L2], 0.0),
            ak=jnp.where(strict, pmat[:L2, L2:], 0.0).astype(BF16),
            rb=jnp.where(incl, pmat[L2:, :L2], 0.0).astype(BF16),
            rk=jnp.where(incl, pmat[L2:, L2:], 0.0).astype(BF16))

    tinv = {cp: eye + amat[cp]["ab"] for cp in probs}
    pw = {cp: amat[cp]["ab"] for cp in probs}
    for _ in range(int(math.log2(L)) - 1):
        for cp in probs:
            pwb = pw[cp].astype(BF16)
            pw[cp] = _dot(pwb, pwb)
        for cp in probs:
            tinv[cp] = tinv[cp] + _mm1(tinv[cp], pw[cp])

    w1 = {cp: _dot(amat[cp]["ak"], opnd[cp]["vs"]) for cp in probs}
    mub = {}
    for cp in probs:
        rhs = jnp.concatenate([opnd[cp]["xa"], w1[cp].astype(BF16)], axis=1)
        mub[cp] = _dot(tinv[cp].astype(BF16), rhs).astype(BF16)
    coef = {}
    for cp in probs:
        o = opnd[cp]
        ry = _dot(amat[cp]["rb"], mub[cp])
        gh = _dot_tn(o["xbh"], mub[cp])
        coef[cp] = dict(
            m_r=(o["xr_f"] + ry[:, :LANES]).astype(BF16),
            y1=ry[:, LANES:] + _dot(amat[cp]["rk"], o["vs"]),
            g=(eye * o["decay_end"] + gh[:, :LANES]).astype(BF16),
            h=gh[:, LANES:] + _dot_tn(o["xkh"], o["vs"]))

    states = [s_ref[p] for p in range(npair)]
    ys = {}
    for (c, p) in probs:
        st = states[p].astype(BF16)
        cf = coef[c, p]
        yst = _dot(cf["m_r"], st) + cf["y1"]
        states[p] = _dot(cf["g"], st) + cf["h"]
        ys[c, p] = yst[:L] + yst[L:]
    for p in range(npair):
        s_ref[p] = states[p]

    yc = {cp: ys[cp] - _seg_sum(ys[cp], ones_bd) * (1.0 / N) for cp in probs}
    bonus = {cp: _seg_sum(opnd[cp]["rk"], ones_bd) * opnd[cp]["v"] for cp in probs}
    var = {cp: _seg_sum(yc[cp] * yc[cp], ones_bd) * (1.0 / N) for cp in probs}
    for (c, p) in probs:
        rows, sl = rows_of(c), lanes_of(p)
        yn = yc[c, p] * lax.rsqrt(var[c, p] + RWKV_GN_EPS) * lnw_ref[:, sl] + lnb_ref[:, sl]
        o_ref[rows, sl] = ((yn + bonus[c, p]) * g_ref[rows, sl]).astype(BF16)


def _merge_body(oa_ref, ob_ref, oc_ref, od_ref, g0_ref, g1_ref, g2_ref, g3_ref, wb_ref, out_ref):
    o_refs = (oa_ref, ob_ref, oc_ref, od_ref)
    g_refs = (g0_ref, g1_ref, g2_ref, g3_ref)
    acc = None
    for n in range(N_BRANCH):
        contrib = _sigmoid(g_refs[n][...].astype(F32)) * _dot(o_refs[n][...], wb_ref[n])
        acc = contrib if acc is None else acc + contrib
    out_ref[...] = acc.astype(BF16)


def _merge(o_a, o_b, o_c, o_d, pin, w_branch, layer, tm=512):
    S = pin.shape[0]
    D = D_MODEL
    ospec = pl.BlockSpec((tm, BRANCH_W), lambda i: (i, 0))
    gspec = lambda n: pl.BlockSpec((tm, D), lambda i: (i, n))
    return pl.pallas_call(
        _merge_body,
        grid=(S // tm,),
        in_specs=[ospec, ospec, ospec, ospec, gspec(0), gspec(1), gspec(2), gspec(3),
                  pl.BlockSpec((None, N_BRANCH, BRANCH_W, D), lambda i: (layer, 0, 0, 0))],
        out_specs=pl.BlockSpec((tm, D), lambda i: (i, 0)),
        out_shape=jax.ShapeDtypeStruct((S, D), BF16),
        compiler_params=_cparams(("arbitrary",)),
        name="merge",
    )(o_a, o_b, o_c, o_d, pin, pin, pin, pin, w_branch)


def _outproj_body(m_ref, w_ref, x_ref, o_ref):
    o_ref[...] = x_ref[...] + _dot(m_ref[...], w_ref[...])


def _outproj(merged, w_out, x, layer, tm=512, tn=2048):
    S, D = x.shape
    return pl.pallas_call(
        _outproj_body,
        grid=(S // tm, D // tn),
        in_specs=[
            pl.BlockSpec((tm, D), lambda i, j: (i, 0)),
            pl.BlockSpec((None, D, tn), lambda i, j: (layer, 0, j)),
            pl.BlockSpec((tm, tn), lambda i, j: (i, j)),
        ],
        out_specs=pl.BlockSpec((tm, tn), lambda i, j: (i, j)),
        out_shape=jax.ShapeDtypeStruct((S, D), F32),
        compiler_params=_cparams(("arbitrary", "arbitrary")),
        name="outproj",
    )(merged, w_out, x)


def _ffn_body(x_ref, nw_ref, wg_ref, wu_ref, wd_ref, out_ref, h_ref, acc_ref):
    f = pl.program_id(1)

    @pl.when(f == 0)
    def _():
        x = x_ref[...]
        ms = jnp.mean(x * x, axis=-1, keepdims=True)
        h_ref[...] = (x * lax.rsqrt(ms + RMS_EPS) * nw_ref[...]).astype(BF16)
        acc_ref[...] = jnp.zeros_like(acc_ref)

    h = h_ref[...]
    act = (_silu(_dot(h, wg_ref[...])) * _dot(h, wu_ref[...])).astype(BF16)
    acc_ref[...] += _dot(act, wd_ref[...])

    @pl.when(f == pl.num_programs(1) - 1)
    def _():
        out_ref[...] = x_ref[...] + acc_ref[...]


def _ffn(x, norm_w, w_gu, w_down, layer, tm=512, tf=512):
    S, D = x.shape
    nf = D_FF // tf
    return pl.pallas_call(
        _ffn_body,
        grid=(S // tm, nf),
        in_specs=[
            pl.BlockSpec((tm, D), lambda i, f: (i, 0)),
            pl.BlockSpec((None, 1, D), lambda i, f: (layer, 0, 0)),
            pl.BlockSpec((None, D, tf), lambda i, f: (layer, 0, f)),
            pl.BlockSpec((None, D, tf), lambda i, f: (layer, 0, f + nf)),
            pl.BlockSpec((None, tf, D), lambda i, f: (layer, f, 0)),
        ],
        out_specs=pl.BlockSpec((tm, D), lambda i, f: (i, 0)),
        out_shape=jax.ShapeDtypeStruct((S, D), F32),
        scratch_shapes=[pltpu.VMEM((tm, D), BF16), pltpu.VMEM((tm, D), F32)],
        compiler_params=_cparams(("arbitrary", "arbitrary")),
        name="ffn",
    )(x, norm_w, w_gu, w_gu, w_down)


def _pad_last(a, width):
    pad = width - a.shape[-1]
    if pad == 0:
        return a
    return jnp.pad(a, [(0, 0)] * (a.ndim - 1) + [(0, pad)])


def _pad_axis(a, axis, width):
    pad = width - a.shape[axis]
    if pad == 0:
        return a
    cfg = [(0, 0)] * a.ndim
    cfg[axis] = (0, pad)
    return jnp.pad(a, cfg)


_MAIN_PIECES = (
    (_O_GATE, P_GATE, N_BRANCH * D_MODEL),
    (_O_RQ, P_RQ, 512), (_O_RK, P_RK, 512), (_O_RV, P_RV, 512), (_O_RG, P_RG, 512),
    (_O_SXBC, P_SXBC, SSM_XBC), (_O_SZ, P_SZ, 512),
    (_O_CQ, P_CQ, MLA_Q_RANK),
    (_O_RW, P_WR, RWKV_W), (_O_RW + RWKV_W, P_WK, RWKV_W), (_O_RW + 2 * RWKV_W, P_WV, RWKV_W),
    (_O_CKV, P_CKV, MLA_KV_RANK),
)
_TAIL_PIECES = (
    (_O_SDT, T_DT, SSM_HEADS),
    (_O_KROPE, T_KROPE, MLA_ROPE),
    (_O_RW + 3 * RWKV_W, T_WL, RWKV_W_LORA),
    (_O_RW + 3 * RWKV_W + RWKV_W_LORA, T_AL, RWKV_A_LORA),
    (_O_RW + 3 * RWKV_W + RWKV_W_LORA + RWKV_A_LORA, T_GL, RWKV_G_LORA),
)
PACK_ROWS = 512


def _pack_tables(pieces, total, tile):
    src_of, valid_of = [], []
    for t in range(total // tile):
        lo = t * tile
        src, valid = 0, 0
        for s0, d0, width in pieces:
            if d0 <= lo < d0 + width:
                src, valid = s0 + (lo - d0), min(tile, d0 + width - lo)
        assert src % SUBLANES == 0
        src_of.append(src // SUBLANES)
        valid_of.append(valid)
    return jnp.asarray(src_of, jnp.int32), jnp.asarray(valid_of, jnp.int32)


def _pack_body(src_ref, valid_ref, w_ref, o_ref):
    w = w_ref[0]
    rows = lax.broadcasted_iota(jnp.int32, w.shape, 0)
    keep = rows < valid_ref[pl.program_id(1)]
    o_ref[...] = jnp.where(keep, w, 0.0).astype(BF16)


def _pack_rows(w_t, pieces, total, tile, name):
    depth, _, d = w_t.shape
    src_of, valid_of = _pack_tables(pieces, total, tile)
    grid_spec = pltpu.PrefetchScalarGridSpec(
        num_scalar_prefetch=2,
        grid=(depth, total // tile),
        in_specs=[pl.BlockSpec((pl.Element(1), pl.Element(tile), pl.Element(d)),
                               lambda l, t, src, valid: (l, src[t] * SUBLANES, 0))],
        out_specs=pl.BlockSpec((None, tile, d), lambda l, t, src, valid: (l, t, 0)),
    )
    return pl.pallas_call(
        _pack_body,
        grid_spec=grid_spec,
        out_shape=jax.ShapeDtypeStruct((depth, total, d), BF16),
        compiler_params=_cparams(("arbitrary", "arbitrary")),
        name=name,
    )(src_of, valid_of, w_t)


def _pack_w_in(w_in):
    w_t = jnp.swapaxes(w_in, 1, 2)
    w_main_t = _pack_rows(w_t, _MAIN_PIECES, N_MAIN, PACK_ROWS, "pack_w_main")
    w_tail_t = _pack_rows(w_t, _TAIL_PIECES, N_TAIL, LANES, "pack_w_tail")
    return w_main_t, w_tail_t


def _rope_tables(positions, dim):
    inv = 1.0 / (ROPE_THETA ** (jnp.arange(0, dim, 2, dtype=F32) / dim))
    ang = positions.astype(F32)[:, None] * inv
    return jnp.cos(ang), jnp.sin(ang)


def kernel(x, positions, norm1_w, w_in, ret_gn_w, ssm_conv_w, ssm_conv_b, ssm_dt_bias, ssm_a_log, ssm_d, ssm_norm_w, mla_q_a_norm_w, mla_w_qb, mla_kv_a_norm_w, mla_w_kvb, mla_q_norm_w, mla_k_norm_w, rwkv_mu, rwkv_w0, rwkv_w2, rwkv_a0, rwkv_a2, rwkv_g2, rwkv_v0, rwkv_v1, rwkv_v2, rwkv_k_k, rwkv_k_a, rwkv_r_k, rwkv_ln_w, rwkv_ln_b, w_branch, w_out, norm2_w, ffn_w_gu, ffn_w_down):
    B, S, D = x.shape
    assert B == 1 and D == D_MODEL and S % 1024 == 0
    xs = x[0]
    pos = positions[0]

    c_r, s_r = _rope_tables(pos, RET_DK)
    cos_ret = jnp.concatenate([c_r, c_r], axis=-1)
    sin_ret = jnp.concatenate([-s_r, s_r], axis=-1)
    c_m, s_m = _rope_tables(pos, MLA_ROPE)
    cos_mla = _pad_last(jnp.concatenate([c_m, c_m], axis=-1), LANES)
    sin_mla = _pad_last(jnp.concatenate([-s_m, s_m], axis=-1), LANES)

    row = lambda a: a[:, None, :]
    w_main, w_tail = _pack_w_in(w_in)
    norm1 = row(norm1_w)
    norm2 = row(norm2_w)
    gn_w = row(ret_gn_w)
    conv_b = row(ssm_conv_b)
    d_skip = row(ssm_d)
    ssm_nw = row(ssm_norm_w)
    dt_bias = row(_pad_last(ssm_dt_bias, LANES))
    a_neg = row(_pad_last(-jnp.exp(ssm_a_log.astype(F32)), LANES))
    qaw = row(mla_q_a_norm_w)
    kvaw = row(mla_kv_a_norm_w)
    wqb = _pad_last(mla_w_qb.reshape(DEPTH, MLA_Q_RANK, MLA_HEADS, MLA_QK), MLA_PAD)
    wqb = wqb.reshape(DEPTH, MLA_Q_RANK, MLA_HEADS * MLA_PAD).astype(BF16)
    wkvb = mla_w_kvb.astype(BF16)
    qnw = row(_pad_last(mla_q_norm_w, MLA_PAD))
    knw_n = row(mla_k_norm_w[:, :MLA_NOPE])
    knw_r = row(_pad_last(mla_k_norm_w[:, MLA_NOPE:], LANES))
    W = RWKV_W
    mu = rwkv_mu
    vec = {
        "mu_r": row(mu[:, :W]), "mu_k": row(mu[:, W:2 * W]), "mu_v": row(mu[:, 2 * W:3 * W]),
        "mu_wl": row(_pad_last(mu[:, 3 * W:3 * W + RWKV_W_LORA], LANES)),
        "mu_al": row(_pad_last(mu[:, 3 * W + RWKV_W_LORA:3 * W + RWKV_W_LORA + RWKV_A_LORA], LANES)),
        "mu_gl": row(mu[:, 3 * W + RWKV_W_LORA + RWKV_A_LORA:]),
        "w0": row(rwkv_w0), "a0": row(rwkv_a0), "k_k": row(rwkv_k_k), "k_a": row(rwkv_k_a),
        "ln_w": row(rwkv_ln_w), "ln_b": row(rwkv_ln_b), "r_k": row(rwkv_r_k.reshape(DEPTH, W)),
    }
    mats = {"w2": _pad_axis(rwkv_w2, 1, LANES), "a2": _pad_axis(rwkv_a2, 1, LANES), "g2": rwkv_g2}
    v0 = row(rwkv_v0)
    v1 = _pad_last(rwkv_v1, LANES)
    v2 = _pad_axis(rwkv_v2, 1, LANES)
    wb = w_branch.astype(BF16)
    wo = w_out.astype(BF16)
    wgu = ffn_w_gu.astype(BF16)
    wdn = ffn_w_down.astype(BF16)

    v_first = None
    for l in range(DEPTH):
        pin, tail = _inproj(xs, norm1, w_main, w_tail, l)
        o_a = _retention(pin, cos_ret, sin_ret, gn_w, l)
        o_b = _ssd(pin, tail, dt_bias, a_neg, ssm_conv_w, conv_b, d_skip, ssm_nw, l)
        q, k, v = _mla_prep(pin, tail, qaw, wqb, kvaw, wkvb, qnw, knw_n, knw_r, cos_mla, sin_mla, l)
        o_c = _flash(q, k, v)
        vmix = None if l == 0 else (v0, v1, v2)
        o_d, vf = _rwkv(pin, tail, vec, mats, v_first, vmix, l)
        if l == 0:
            v_first = vf
        merged = _merge(o_a, o_b, o_c, o_d, pin, wb, l)
        xs = _outproj(merged, wo, xs, l)
        xs = _ffn(xs, norm2, wgu, wdn, l)
    return xs[None]
```

```python
import functools
import math

import numpy as np
import jax
import jax.numpy as jnp
from jax import lax
from jax.experimental import pallas as pl
from jax.experimental.pallas import tpu as pltpu

F32 = jnp.float32
BF16 = jnp.bfloat16

D_MODEL = 2048
DEPTH = 4
CHUNK = 64
N_BRANCH = 4
BRANCH_W = 512
RMS_EPS = 1e-6
GN_EPS = 1e-5
ROPE_THETA = 10000.0
RET_HEADS, RET_DK, RET_DV = 4, 128, 128
SSM_HEADS, SSM_HEADDIM, SSM_GROUPS, SSM_STATE, SSM_CONV = 8, 64, 2, 128, 4
SSM_XBC = SSM_HEADS * SSM_HEADDIM + 2 * SSM_GROUPS * SSM_STATE
MLA_HEADS, MLA_Q_RANK, MLA_KV_RANK, MLA_NOPE, MLA_ROPE, MLA_V = 4, 512, 256, 128, 64, 128
MLA_QK = MLA_NOPE + MLA_ROPE
RWKV_HEADS, RWKV_HEAD = 8, 64
RWKV_W = RWKV_HEADS * RWKV_HEAD
RWKV_W_LORA, RWKV_A_LORA, RWKV_V_LORA, RWKV_G_LORA = 64, 64, 32, 128
RWKV_GN_EPS = 64e-5
D_FF = 5632

LANES = 128
SUBLANES = 8
VMEM_LIMIT_BYTES = 56 * 1024 * 1024

_O_RQ, _O_RK, _O_RV, _O_RG = 0, 512, 1024, 1536
_O_SZ, _O_SXBC, _O_SDT = 2048, 2560, 3584
_O_CQ, _O_CKV, _O_KROPE = 3592, 4104, 4360
_O_RW = 4424
_O_GATE = 6216
P_GATE = 0
P_RQ, P_RK, P_RV, P_RG = 8192, 8704, 9216, 9728
P_SXBC, P_SZ = 10240, 11264
P_CQ = 11776
P_WR, P_WK, P_WV = 12288, 12800, 13312
P_CKV = 13824
N_MAIN = 14336
T_DT, T_KROPE, T_WL, T_AL, T_GL = 0, 128, 256, 384, 512
N_TAIL = 640

ROW_BLOCK = 256
RWKV_L = 64
LOG2E = 1.4426950408889634


def _cparams(sem):
    return pltpu.CompilerParams(dimension_semantics=sem, vmem_limit_bytes=VMEM_LIMIT_BYTES)


def _dot(a, b):
    return jnp.dot(a, b, preferred_element_type=F32)


def _dot_nt(a, b):
    return lax.dot_general(a, b, (((1,), (1,)), ((), ())), preferred_element_type=F32)


def _dot_tn(a, b):
    return lax.dot_general(a, b, (((0,), (0,)), ((), ())), preferred_element_type=F32)


def _split_bf16(x):
    hi = x.astype(BF16)
    lo = (x - hi.astype(F32)).astype(BF16)
    return hi, lo


def _mm3(a, b, dot=_dot):
    ah, al = _split_bf16(a)
    bh, bl = _split_bf16(b)
    return dot(ah, bh) + dot(ah, bl) + dot(al, bh)


def _mm1(a, b, dot=_dot):
    return dot(a.astype(BF16), b.astype(BF16))


def _mm_exact_lhs(a01, x, dot=_dot):
    xh, xl = _split_bf16(x)
    a = a01.astype(BF16)
    return dot(a, xh) + dot(a, xl)


def _seg_sum(x, ones_bd):
    xh, xl = _split_bf16(x)
    return _dot(xh, ones_bd) + _dot(xl, ones_bd)


def _sigmoid(x):
    return 0.5 * jnp.tanh(0.5 * x) + 0.5


def _silu(x):
    return x * _sigmoid(x)


def _softplus(x):
    return jnp.maximum(x, 0.0) + jnp.log(1.0 + jnp.exp(-jnp.abs(x)))


def _shift_rows(x, carry8, s):
    xr = pltpu.roll(x, s, 0)
    pr = pltpu.roll(carry8, s, 0)
    row = lax.broadcasted_iota(jnp.int32, carry8.shape, 0)
    top = jnp.where(row < s, pr, xr[:SUBLANES])
    return jnp.concatenate([top, xr[SUBLANES:]], axis=0)


def _inproj_body(x_ref, nw_ref, w_ref, wt_ref, o_ref, ot_ref, h_ref):
    @pl.when(pl.program_id(1) == 0)
    def _():
        x = x_ref[...]
        ms = jnp.mean(x * x, axis=-1, keepdims=True)
        h = (x * lax.rsqrt(ms + RMS_EPS) * nw_ref[...]).astype(BF16)
        h_ref[...] = h
        ot_ref[...] = _dot_nt(h, wt_ref[...])

    o_ref[...] = _dot_nt(h_ref[...], w_ref[...]).astype(BF16)


def _inproj(x, norm_w, w_main, w_tail, layer, tm=1024, tn=1024):
    S, D = x.shape
    return pl.pallas_call(
        _inproj_body,
        grid=(S // tm, N_MAIN // tn),
        in_specs=[
            pl.BlockSpec((tm, D), lambda i, j: (i, 0)),
            pl.BlockSpec((None, 1, D), lambda i, j: (layer, 0, 0)),
            pl.BlockSpec((None, tn, D), lambda i, j: (layer, j, 0)),
            pl.BlockSpec((None, N_TAIL, D), lambda i, j: (layer, 0, 0)),
        ],
        out_specs=[pl.BlockSpec((tm, tn), lambda i, j: (i, j)),
                   pl.BlockSpec((tm, N_TAIL), lambda i, j: (i, 0))],
        out_shape=[jax.ShapeDtypeStruct((S, N_MAIN), BF16), jax.ShapeDtypeStruct((S, N_TAIL), F32)],
        scratch_shapes=[pltpu.VMEM((tm, D), BF16)],
        compiler_params=_cparams(("arbitrary", "arbitrary")),
        name="inproj",
    )(x, norm_w, w_main, w_tail)


def _ret_body(q_ref, k_ref, v_ref, g_ref, cos_ref, sin_ref, m_ref, qd_ref, kd_ref, gnw_ref,
              o_ref, s_ref, *, block_decay):
    @pl.when(pl.program_id(0) == 0)
    def _():
        s_ref[...] = jnp.zeros_like(s_ref)

    cos = cos_ref[...]
    sin = sin_ref[...]
    heads = range(RET_HEADS)
    lanes = [slice(h * RET_DK, (h + 1) * RET_DK) for h in heads]
    q, k, vb = [], [], []
    for h in heads:
        qh = q_ref[:, lanes[h]].astype(F32)
        kh = k_ref[:, lanes[h]].astype(F32)
        q.append(qh * cos + pltpu.roll(qh, RET_DK // 2, 1) * sin)
        k.append((kh * cos + pltpu.roll(kh, RET_DK // 2, 1) * sin) * (RET_DK ** -0.5))
        vb.append(v_ref[:, lanes[h]])
    sc = [(_dot_nt(q[h].astype(BF16), k[h].astype(BF16)) * m_ref[h]).astype(BF16) for h in heads]
    st = [s_ref[h] for h in heads]
    o = [_dot(sc[h], vb[h]) + _dot((q[h] * qd_ref[h]).astype(BF16), st[h].astype(BF16)) for h in heads]
    for h in heads:
        s_ref[h] = block_decay[h] * st[h] + _dot_tn((k[h] * kd_ref[h]).astype(BF16), vb[h])
    oc = [o[h] - jnp.mean(o[h], axis=-1, keepdims=True) for h in heads]
    var = [jnp.mean(oc[h] * oc[h], axis=-1, keepdims=True) for h in heads]
    for h in heads:
        on = oc[h] * lax.rsqrt(var[h] + GN_EPS) * gnw_ref[:, lanes[h]]
        o_ref[:, lanes[h]] = (_silu(g_ref[:, lanes[h]].astype(F32)) * on).astype(BF16)


def _ret_tables(tb):
    lg = np.log1p(-np.exp2(-5.0 - np.arange(RET_HEADS, dtype=np.float64)))
    pos = np.arange(tb, dtype=np.float64)
    dist = np.abs(pos[:, None] - pos[None, :])
    visible = (pos[None, :] // CHUNK) <= (pos[:, None] // CHUNK)
    mask = np.where(visible[None], np.exp(lg[:, None, None] * dist[None]), 0.0)
    qd = np.exp(lg[:, None] * (pos[None, :] + 1.0))[:, :, None]
    kd = np.exp(lg[:, None] * (tb - 1.0 - pos[None, :]))[:, :, None]
    bd = tuple(float(v) for v in np.exp(lg * tb))
    return (jnp.asarray(mask, F32), jnp.asarray(qd, F32), jnp.asarray(kd, F32), bd)


def _retention(pin, cos2, sin2, gn_w, layer, tb=ROW_BLOCK):
    S = pin.shape[0]
    mask, qd, kd, bd = _ret_tables(tb)
    w = RET_HEADS * RET_DK
    col = lambda off: (lambda i: (i, off // w))
    full3 = lambda i: (0, 0, 0)
    return pl.pallas_call(
        functools.partial(_ret_body, block_decay=bd),
        grid=(S // tb,),
        in_specs=[
            pl.BlockSpec((tb, w), col(P_RQ)),
            pl.BlockSpec((tb, w), col(P_RK)),
            pl.BlockSpec((tb, w), col(P_RV)),
            pl.BlockSpec((tb, w), col(P_RG)),
            pl.BlockSpec((tb, RET_DK), lambda i: (i, 0)),
            pl.BlockSpec((tb, RET_DK), lambda i: (i, 0)),
            pl.BlockSpec((RET_HEADS, tb, tb), full3),
            pl.BlockSpec((RET_HEADS, tb, 1), full3),
            pl.BlockSpec((RET_HEADS, tb, 1), full3),
            pl.BlockSpec((None, 1, w), lambda i: (layer, 0, 0)),
        ],
        out_specs=pl.BlockSpec((tb, w), lambda i: (i, 0)),
        out_shape=jax.ShapeDtypeStruct((S, w), BF16),
        scratch_shapes=[pltpu.VMEM((RET_HEADS, RET_DK, RET_DV), F32)],
        compiler_params=_cparams(("arbitrary",)),
        name="retention",
    )(pin, pin, pin, pin, cos2, sin2, mask, qd, kd, gn_w)


def _ssd_body(xbc_ref, z_ref, dt_ref, dtb_ref, aneg_ref, cw_ref, cb_ref, d_ref, nw_ref,
              o_ref, carry_ref, st_ref):
    tb = xbc_ref.shape[0]
    P, N = SSM_HEADDIM, SSM_STATE
    nx = SSM_HEADS * P

    @pl.when(pl.program_id(0) == 0)
    def _():
        carry_ref[...] = jnp.zeros_like(carry_ref)
        st_ref[...] = jnp.zeros_like(st_ref)

    x = xbc_ref[...].astype(F32)
    carry = carry_ref[...]
    acc = x * cw_ref[SSM_CONV - 1:SSM_CONV, :] + cb_ref[...]
    for s in range(1, SSM_CONV):
        acc = acc + _shift_rows(x, carry, s) * cw_ref[SSM_CONV - 1 - s:SSM_CONV - s, :]
    carry_ref[...] = x[tb - SUBLANES:, :]
    xbc = _silu(acc)

    dt = _softplus(dt_ref[...] + dtb_ref[...])
    adt = dt * aneg_ref[...]
    row = lax.broadcasted_iota(jnp.int32, (tb, tb), 0)
    colm = lax.broadcasted_iota(jnp.int32, (tb, tb), 1)
    causal = colm <= row
    tril = causal.astype(F32)
    a_col = jnp.dot(tril, adt, preferred_element_type=F32, precision=lax.Precision.HIGHEST)
    triu = (row <= colm).astype(F32)
    a_row = lax.dot_general(adt, triu, (((0,), (0,)), ((), ())),
                            preferred_element_type=F32, precision=lax.Precision.HIGHEST)
    a_end = a_col[tb - 1:tb, :]
    lane = lax.broadcasted_iota(jnp.int32, (tb, 2 * P), 1)
    first = lane < P

    groups = range(SSM_GROUPS)
    heads = range(SSM_HEADS)
    hpg = SSM_HEADS // SSM_GROUPS
    npair = SSM_HEADS // 2
    bm = [xbc[:, nx + g * N: nx + (g + 1) * N] for g in groups]
    cm = [xbc[:, nx + SSM_GROUPS * N + g * N: nx + SSM_GROUPS * N + (g + 1) * N] for g in groups]
    cb = [_dot_nt(cm[g].astype(BF16), bm[g].astype(BF16)) for g in groups]
    xp = [xbc[:, 2 * q * P:(2 * q + 2) * P] for q in range(npair)]
    xdt = [(xp[q] * jnp.where(first, dt[:, 2 * q:2 * q + 1], dt[:, 2 * q + 1:2 * q + 2])).astype(BF16)
           for q in range(npair)]
    st = [st_ref[q] for q in range(npair)]
    ac = [a_col[:, h:h + 1] for h in heads]
    dec = [jnp.exp(jnp.where(causal, ac[h] - a_row[h:h + 1, :], -jnp.inf)) for h in heads]
    wts = [(cb[h // hpg] * dec[h]).astype(BF16) for h in heads]
    y = [_dot(wts[h], xdt[h // 2]) for h in heads]
    cdec = [(cm[h // hpg] * jnp.exp(ac[h])).astype(BF16) for h in heads]
    y = [y[h] + _dot(cdec[h], st[h // 2].astype(BF16)) for h in heads]
    bdec = [(bm[h // hpg] * jnp.exp(a_end[:, h:h + 1] - ac[h])).astype(BF16) for h in heads]
    snew = [jnp.exp(a_end[:, h:h + 1]) * st[h // 2] + _dot_tn(bdec[h], xdt[h // 2]) for h in heads]
    lane_s = lax.broadcasted_iota(jnp.int32, (N, 2 * P), 1)
    for q in range(npair):
        st_ref[q] = jnp.where(lane_s < P, snew[2 * q], snew[2 * q + 1])
    ypair = [jnp.where(first, y[2 * q], y[2 * q + 1])
             + xp[q] * jnp.where(first, d_ref[:, 2 * q:2 * q + 1], d_ref[:, 2 * q + 1:2 * q + 2])
             for q in range(npair)]
    for g in groups:
        yg = jnp.concatenate(ypair[g * hpg // 2:(g + 1) * hpg // 2], axis=1)
        gsl = slice(g * hpg * P, (g + 1) * hpg * P)
        yg = yg * _silu(z_ref[:, gsl].astype(F32))
        ms = jnp.mean(yg * yg, axis=-1, keepdims=True)
        o_ref[:, gsl] = (yg * lax.rsqrt(ms + RMS_EPS) * nw_ref[:, gsl]).astype(BF16)


def _ssd(pin, tail, dt_bias, a_neg, conv_w, conv_b, d_skip, norm_w, layer, tb=ROW_BLOCK):
    S = pin.shape[0]
    nx = SSM_HEADS * SSM_HEADDIM
    H = SSM_HEADS
    lsel = lambda i: (layer, 0, 0)
    return pl.pallas_call(
        _ssd_body,
        grid=(S // tb,),
        in_specs=[
            pl.BlockSpec((tb, SSM_XBC), lambda i: (i, P_SXBC // SSM_XBC)),
            pl.BlockSpec((tb, nx), lambda i: (i, P_SZ // nx)),
            pl.BlockSpec((tb, LANES), lambda i: (i, T_DT // LANES)),
            pl.BlockSpec((None, 1, LANES), lsel),
            pl.BlockSpec((None, 1, LANES), lsel),
            pl.BlockSpec((None, SSM_CONV, SSM_XBC), lsel),
            pl.BlockSpec((None, 1, SSM_XBC), lsel),
            pl.BlockSpec((None, 1, H), lsel),
            pl.BlockSpec((None, 1, nx), lsel),
        ],
        out_specs=pl.BlockSpec((tb, nx), lambda i: (i, 0)),
        out_shape=jax.ShapeDtypeStruct((S, nx), BF16),
        scratch_shapes=[pltpu.VMEM((SUBLANES, SSM_XBC), F32),
                        pltpu.VMEM((H // 2, SSM_STATE, 2 * SSM_HEADDIM), F32)],
        compiler_params=_cparams(("arbitrary",)),
        name="ssd",
    )(pin, pin, tail, dt_bias, a_neg, conv_w, conv_b, d_skip, norm_w)


MLA_PAD = 256
MLA_VX = MLA_V + 16


def _rope_pad(x, cosp, sinp):
    half = MLA_ROPE // 2
    return x * cosp + (pltpu.roll(x, half, 1) + pltpu.roll(x, LANES - half, 1)) * sinp


def _mla_prep_body(cq_ref, ckv_ref, kr_ref, qaw_ref, wqb_ref, kvaw_ref, wkvb_ref, qnw_ref,
                   knw_n_ref, knw_r_ref, cos_ref, sin_ref, q_out, k_out, v_out):
    cosp = cos_ref[...]
    sinp = sin_ref[...]
    cq = cq_ref[...].astype(F32)
    ms = jnp.mean(cq * cq, axis=-1, keepdims=True)
    cqn = (cq * lax.rsqrt(ms + RMS_EPS) * qaw_ref[...]).astype(BF16)
    q = _dot(cqn, wqb_ref[...])
    ckv = ckv_ref[...].astype(F32)
    ms = jnp.mean(ckv * ckv, axis=-1, keepdims=True)
    ckvn = (ckv * lax.rsqrt(ms + RMS_EPS) * kvaw_ref[...]).astype(BF16)
    kv = _dot(ckvn, wkvb_ref[...])
    kr = kr_ref[...]
    ssr = jnp.sum(kr * kr, axis=-1, keepdims=True)
    scale = (MLA_QK ** -0.5) * LOG2E
    heads = range(MLA_HEADS)
    qh = [q[:, h * MLA_PAD:(h + 1) * MLA_PAD] for h in heads]
    kn = [kv[:, h * MLA_PAD: h * MLA_PAD + MLA_NOPE] for h in heads]
    qinv = [lax.rsqrt(jnp.sum(qh[h] * qh[h], axis=-1, keepdims=True) * (1.0 / MLA_QK) + RMS_EPS) for h in heads]
    kinv = [lax.rsqrt((jnp.sum(kn[h] * kn[h], axis=-1, keepdims=True) + ssr) * (1.0 / MLA_QK) + RMS_EPS)
            for h in heads]
    qh = [qh[h] * qinv[h] * qnw_ref[...] for h in heads]
    qr = [_rope_pad(qh[h][:, MLA_NOPE:], cosp, sinp) for h in heads]
    krh = [_rope_pad(kr * kinv[h] * knw_r_ref[...], cosp, sinp) for h in heads]
    ones = jnp.ones((MLA_VX - MLA_V, kv.shape[0]), F32)
    for h in heads:
        q_out[h] = (jnp.concatenate([qh[h][:, :MLA_NOPE], qr[h]], axis=1) * scale).T.astype(BF16)
    for h in heads:
        k_out[h] = jnp.concatenate([kn[h] * kinv[h] * knw_n_ref[...], krh[h]], axis=1).astype(BF16)
    for h in heads:
        vv = kv[:, h * MLA_PAD + MLA_NOPE:(h + 1) * MLA_PAD]
        v_out[h] = jnp.concatenate([vv.T, ones], axis=0).astype(BF16)


def _mla_prep(pin, tail, qaw, wqb, kvaw, wkvb, qnw, knw_n, knw_r, cosp, sinp, layer, tb=ROW_BLOCK):
    S = pin.shape[0]
    H = MLA_HEADS
    lsel = lambda i: (layer, 0, 0)
    return pl.pallas_call(
        _mla_prep_body,
        grid=(S // tb,),
        in_specs=[
            pl.BlockSpec((tb, MLA_Q_RANK), lambda i: (i, P_CQ // MLA_Q_RANK)),
            pl.BlockSpec((tb, MLA_KV_RANK), lambda i: (i, P_CKV // MLA_KV_RANK)),
            pl.BlockSpec((tb, LANES), lambda i: (i, T_KROPE // LANES)),
            pl.BlockSpec((None, 1, MLA_Q_RANK), lsel),
            pl.BlockSpec((None, MLA_Q_RANK, H * MLA_PAD), lsel),
            pl.BlockSpec((None, 1, MLA_KV_RANK), lsel),
            pl.BlockSpec((None, MLA_KV_RANK, H * MLA_PAD), lsel),
            pl.BlockSpec((None, 1, MLA_PAD), lsel),
            pl.BlockSpec((None, 1, MLA_NOPE), lsel),
            pl.BlockSpec((None, 1, LANES), lsel),
            pl.BlockSpec((tb, LANES), lambda i: (i, 0)),
            pl.BlockSpec((tb, LANES), lambda i: (i, 0)),
        ],
        out_specs=[
            pl.BlockSpec((H, MLA_PAD, tb), lambda i: (0, 0, i)),
            pl.BlockSpec((H, tb, MLA_PAD), lambda i: (0, i, 0)),
            pl.BlockSpec((H, MLA_VX, tb), lambda i: (0, 0, i)),
        ],
        out_shape=[
            jax.ShapeDtypeStruct((H, MLA_PAD, S), BF16),
            jax.ShapeDtypeStruct((H, S, MLA_PAD), BF16),
            jax.ShapeDtypeStruct((H, MLA_VX, S), BF16),
        ],
        compiler_params=_cparams(("arbitrary",)),
        name="mla_prep",
    )(pin, pin, tail, qaw, wqb, kvaw, wkvb, qnw, knw_n, knw_r, cosp, sinp)


def _flash_body(it_ref, jt_ref, qt_ref, k_ref, vt_ref, o_ref, m_ref, acc_ref):
    t = pl.program_id(0)
    i = it_ref[t]
    j = jt_ref[t]
    tq = qt_ref.shape[2]
    tk = k_ref.shape[1]
    ratio = tq // tk
    H = MLA_HEADS

    @pl.when(j == 0)
    def _():
        m_ref[...] = jnp.full_like(m_ref, -jnp.inf)
        acc_ref[...] = jnp.zeros_like(acc_ref)

    def step(diagonal):
        if diagonal:
            kc = (j * tk + lax.broadcasted_iota(jnp.int32, (tk, tq), 0)) // CHUNK
            qc = (i * tq + lax.broadcasted_iota(jnp.int32, (tk, tq), 1)) // CHUNK
            visible = kc <= qc

        def logits(h):
            s = _dot(k_ref[h], qt_ref[h])
            return jnp.where(visible, s, -jnp.inf) if diagonal else s

        def softmax(h, s):
            m_prev = m_ref[h]
            m_new = jnp.maximum(m_prev, jnp.max(s, axis=0, keepdims=True))
            alpha = jnp.exp2(m_prev - m_new)
            m_ref[h] = m_new
            return alpha, jnp.exp2((s - m_new).astype(BF16))

        def accumulate(h, alpha, p):
            acc_ref[h] = alpha * acc_ref[h] + _dot(vt_ref[h], p)

        s_next = logits(0)
        pending = None
        for h in range(H):
            s_cur = s_next
            if h + 1 < H:
                s_next = logits(h + 1)
            if pending is not None:
                accumulate(*pending)
            pending = (h,) + softmax(h, s_cur)
        accumulate(*pending)

    @pl.when(j < ratio * i)
    def _():
        step(False)

    @pl.when(j >= ratio * i)
    def _():
        step(True)

    @pl.when(j == ratio * (i + 1) - 1)
    def _():
        for h in range(H):
            acc = acc_ref[h]
            o_ref[:, h * MLA_V:(h + 1) * MLA_V] = (acc[:MLA_V] / acc[MLA_V:MLA_V + 1]).T.astype(BF16)


def _flash(qt, k, vt, tq=512, tk=512):
    H, S, _ = k.shape
    ratio = tq // tk
    pairs = [(i, j) for i in range(S // tq) for j in range(ratio * (i + 1))]
    it = jnp.asarray([p[0] for p in pairs], jnp.int32)
    jt = jnp.asarray([p[1] for p in pairs], jnp.int32)
    grid_spec = pltpu.PrefetchScalarGridSpec(
        num_scalar_prefetch=2,
        grid=(len(pairs),),
        in_specs=[
            pl.BlockSpec((H, MLA_PAD, tq), lambda t, it, jt: (0, 0, it[t])),
            pl.BlockSpec((H, tk, MLA_PAD), lambda t, it, jt: (0, jt[t], 0)),
            pl.BlockSpec((H, MLA_VX, tk), lambda t, it, jt: (0, 0, jt[t])),
        ],
        out_specs=pl.BlockSpec((tq, H * MLA_V), lambda t, it, jt: (it[t], 0)),
        scratch_shapes=[pltpu.VMEM((H, 1, tq), F32), pltpu.VMEM((H, MLA_VX, tq), F32)],
    )
    return pl.pallas_call(
        _flash_body,
        grid_spec=grid_spec,
        out_shape=jax.ShapeDtypeStruct((S, H * MLA_V), BF16),
        compiler_params=_cparams(("arbitrary",)),
        name="mla_flash",
    )(it, jt, qt, k, vt)


def _interleave(*streams):
    active = list(streams)
    while active:
        for g in list(active):
            try:
                next(g)
            except StopIteration:
                active.remove(g)


def _rwkv_body(*refs, mix_v):
    if mix_v:
        (r_ref, k_ref, v_ref, wl_ref, al_ref, gl_ref, mur, muk, muv, muwl, mual, mugl, w0, a0, w2, a2, g2,
         kk_ref, ka_ref, lnw_ref, lnb_ref, rk_ref, vf_ref, v0, v1, v2,
         o_ref, c_r, c_k, c_v, c_wl, c_al, c_gl, s_ref) = refs
    else:
        (r_ref, k_ref, v_ref, wl_ref, al_ref, gl_ref, mur, muk, muv, muwl, mual, mugl, w0, a0, w2, a2, g2,
         kk_ref, ka_ref, lnw_ref, lnb_ref, rk_ref,
         o_ref, vfirst_ref, c_r, c_k, c_v, c_wl, c_al, c_gl, s_ref) = refs
    tb = r_ref.shape[0]
    L = RWKV_L
    N = RWKV_HEAD
    L2 = 2 * L
    npair = RWKV_W // LANES
    half = tb // 2
    carries = (c_r, c_k, c_v, c_wl, c_al, c_gl)

    @pl.when(pl.program_id(0) == 0)
    def _():
        for c in carries:
            c[...] = jnp.zeros_like(c)
        s_ref[...] = jnp.zeros_like(s_ref)

    def mixed(x_ref, c_ref, mu_ref):
        x = x_ref[...].astype(F32)
        prev = _shift_rows(x, c_ref[...], 1)
        c_ref[...] = x[tb - SUBLANES:, :]
        return x + (prev - x) * mu_ref[...]

    r = mixed(r_ref, c_r, mur)
    k = mixed(k_ref, c_k, muk)
    v = mixed(v_ref, c_v, muv)
    wl = mixed(wl_ref, c_wl, muwl)
    al = mixed(al_ref, c_al, mual)
    gl = mixed(gl_ref, c_gl, mugl)
    if not mix_v:
        vfirst_ref[...] = v

    lane = lax.broadcasted_iota(jnp.int32, (L, LANES), 1)
    m0 = lane < N
    r2 = lax.broadcasted_iota(jnp.int32, (L2, L2), 0)
    c2 = lax.broadcasted_iota(jnp.int32, (L2, L2), 1)
    same = (r2 // L) == (c2 // L)
    strict = same & (c2 < r2)
    incl = same & (c2 <= r2)
    eye = (r2 == c2).astype(F32)
    ones_bd = ((r2 // N) == (c2 // N)).astype(BF16)
    rb = lax.broadcasted_iota(jnp.int32, (half, half), 0)
    cb = lax.broadcasted_iota(jnp.int32, (half, half), 1)
    tril_bd = ((rb // L) == (cb // L)) & (cb <= rb)

    def stack(x):
        return jnp.concatenate([jnp.where(m0, x, 0.0), jnp.where(m0, 0.0, x)], axis=0).astype(BF16)

    probs = [(c, p) for c in range(half // L) for p in range(npair)]
    rows_of = lambda c: slice(c * L, (c + 1) * L)
    lanes_of = lambda p: slice(p * LANES, (p + 1) * LANES)
    opnds = [{}, {}]
    gates = [None, None]
    states = [s_ref[p] for p in range(npair)]

    def front(hf):
        rs = slice(hf * half, (hf + 1) * half)
        w_raw = w0[...] + _mm3(jnp.tanh(wl[rs]), w2[...])
        lw = -jnp.exp(-_softplus(-w_raw) - 0.5)
        yield
        a_sig = _sigmoid(a0[...] + _mm3(al[rs], a2[...]))
        gates[hf] = _mm1(_sigmoid(gl[rs]), g2[...])
        yield
        vh = v[rs]
        if mix_v:
            lora = _mm1(_mm1(vh, v1[...]), v2[...])
            vh = vh + (vf_ref[rs, :] - vh) * _sigmoid(v0[...] + lora)
        yield
        kk = k[rs] * kk_ref[...]
        parts = []
        for p in range(npair):
            kp = kk[:, lanes_of(p)]
            n2 = _seg_sum(kp * kp, ones_bd)
            parts.append(kp / jnp.maximum(jnp.sqrt(n2), 1e-12))
            if p % 2:
                yield
        kk = jnp.concatenate(parts, axis=1)
        kh = k[rs] * (1.0 + (a_sig - 1.0) * ka_ref[...])
        ah = -kk
        bh = kk * a_sig
        rh = r[rs]
        cum_blk = _mm_exact_lhs(tril_bd, lw)
        yield
        for n, (c, p) in enumerate(probs):
            rows, sl = rows_of(c), lanes_of(p)
            cum = cum_blk[rows, sl]
            cum_end = cum[L - 1:L, :]
            rr = rh[rows, sl]
            kc = kh[rows, sl]
            vc = vh[rows, sl]
            e_neg = jnp.exp(-cum)
            e_end = jnp.exp(cum_end - cum)
            re = rr * jnp.exp(cum)
            xr_f = jnp.concatenate([jnp.where(m0, re, 0.0), jnp.where(m0, 0.0, re)], axis=0)
            opnds[hf][c, p] = dict(
                xr_f=xr_f, xa=stack(ah[rows, sl] * jnp.exp(cum - lw[rows, sl])), xr=xr_f.astype(BF16),
                xb=stack(bh[rows, sl] * e_neg), xk=stack(kc * e_neg), xbh=stack(bh[rows, sl] * e_end),
                xkh=stack(kc * e_end), vs=stack(vc), decay_end=jnp.exp(cum_end),
                rk=rr * kc * rk_ref[:, sl], v=vc)
            if n % 2:
                yield

    def back(hf):
        opnd = opnds[hf]
        amat = {}
        for cp in probs:
            o = opnd[cp]
            pmat = _dot_nt(jnp.concatenate([o["xa"], o["xr"]], axis=0),
                           jnp.concatenate([o["xb"], o["xk"]], axis=0))
            amat[cp] = dict(
                ab=jnp.where(strict, pmat[:L2, :L2], 0.0),
                ak=jnp.where(strict, pmat[:L2, L2:], 0.0).astype(BF16),
                rb=jnp.where(incl, pmat[L2:, :L2], 0.0).astype(BF16),
                rk=jnp.where(incl, pmat[L2:, L2:], 0.0).astype(BF16))
        yield
        tinv = {cp: eye + amat[cp]["ab"] for cp in probs}
        pw = {cp: amat[cp]["ab"] for cp in probs}
        for _ in range(int(math.log2(L)) - 1):
            for cp in probs:
                pwb = pw[cp].astype(BF16)
                pw[cp] = _dot(pwb, pwb)
            yield
            for cp in probs:
                tinv[cp] = tinv[cp] + _mm1(tinv[cp], pw[cp])
            yield
        w1 = {cp: _dot(amat[cp]["ak"], opnd[cp]["vs"]) for cp in probs}
        yield
        mub = {}
        for cp in probs:
            rhs = jnp.concatenate([opnd[cp]["xa"], w1[cp].astype(BF16)], axis=1)
            mub[cp] = _dot(tinv[cp].astype(BF16), rhs).astype(BF16)
        yield
        coef = {}
        for cp in probs:
            o = opnd[cp]
            ry = _dot(amat[cp]["rb"], mub[cp])
            gh = _dot_tn(o["xbh"], mub[cp])
            coef[cp] = dict(
                m_r=(o["xr_f"] + ry[:, :LANES]).astype(BF16),
                y1=ry[:, LANES:] + _dot(amat[cp]["rk"], o["vs"]),
                g=(eye * o["decay_end"] + gh[:, :LANES]).astype(BF16),
                h=gh[:, LANES:] + _dot_tn(o["xkh"], o["vs"]))
        yield
        ys = {}
        for (c, p) in probs:
            st = states[p].astype(BF16)
            cf = coef[c, p]
            yst = _dot(cf["m_r"], st) + cf["y1"]
            states[p] = _dot(cf["g"], st) + cf["h"]
            ys[c, p] = yst[:L] + yst[L:]
        yield
        yc = {cp: ys[cp] - _seg_sum(ys[cp], ones_bd) * (1.0 / N) for cp in probs}
        yield
        bonus = {cp: _seg_sum(opnd[cp]["rk"], ones_bd) * opnd[cp]["v"] for cp in probs}
        var = {cp: _seg_sum(yc[cp] * yc[cp], ones_bd) * (1.0 / N) for cp in probs}
        yield
        for (c, p) in probs:
            rows, sl = rows_of(c), lanes_of(p)
            out_rows = slice(hf * half + c * L, hf * half + (c + 1) * L)
            yn = yc[c, p] * lax.rsqrt(var[c, p] + RWKV_GN_EPS) * lnw_ref[:, sl] + lnb_ref[:, sl]
            o_ref[out_rows, sl] = ((yn + bonus[c, p]) * gates[hf][rows, sl]).astype(BF16)

    _interleave(front(0), front(1))
    _interleave(back(0), back(1))
    for p in range(npair):
        s_ref[p] = states[p]


def _rwkv(pin, tail, vec, mats, vfirst, vmix, layer, tb=ROW_BLOCK):
    S = pin.shape[0]
    W = RWKV_W
    lsel = lambda i: (layer, 0, 0)
    lsel1 = lambda i: (layer - 1, 0, 0)
    in_specs = [
        pl.BlockSpec((tb, W), lambda i: (i, P_WR // W)),
        pl.BlockSpec((tb, W), lambda i: (i, P_WK // W)),
        pl.BlockSpec((tb, W), lambda i: (i, P_WV // W)),
        pl.BlockSpec((tb, LANES), lambda i: (i, T_WL // LANES)),
        pl.BlockSpec((tb, LANES), lambda i: (i, T_AL // LANES)),
        pl.BlockSpec((tb, LANES), lambda i: (i, T_GL // LANES)),
    ]
    args = [pin] * 3 + [tail] * 3
    for name, width in (("mu_r", W), ("mu_k", W), ("mu_v", W), ("mu_wl", LANES), ("mu_al", LANES),
                        ("mu_gl", LANES), ("w0", W), ("a0", W)):
        in_specs.append(pl.BlockSpec((None, 1, width), lsel))
        args.append(vec[name])
    for name in ("w2", "a2", "g2"):
        in_specs.append(pl.BlockSpec((None, LANES, W), lsel))
        args.append(mats[name])
    for name in ("k_k", "k_a", "ln_w", "ln_b", "r_k"):
        in_specs.append(pl.BlockSpec((None, 1, W), lsel))
        args.append(vec[name])
    mix_v = vmix is not None
    if mix_v:
        v0, v1, v2 = vmix
        in_specs += [pl.BlockSpec((tb, W), lambda i: (i, 0)),
                     pl.BlockSpec((None, 1, W), lsel1),
                     pl.BlockSpec((None, W, LANES), lsel1),
                     pl.BlockSpec((None, LANES, W), lsel1)]
        args += [vfirst, v0, v1, v2]
    out_spec = pl.BlockSpec((tb, W), lambda i: (i, 0))
    out_specs = [out_spec] if mix_v else [out_spec, out_spec]
    out_shape = [jax.ShapeDtypeStruct((S, W), BF16)]
    if not mix_v:
        out_shape.append(jax.ShapeDtypeStruct((S, W), F32))
    outs = pl.pallas_call(
        functools.partial(_rwkv_body, mix_v=mix_v),
        grid=(S // tb,),
        in_specs=in_specs,
        out_specs=out_specs,
        out_shape=out_shape,
        scratch_shapes=[pltpu.VMEM((SUBLANES, W), F32)] * 3 + [pltpu.VMEM((SUBLANES, LANES), F32)] * 3
        + [pltpu.VMEM((W // LANES, LANES, LANES), F32)],
        compiler_params=_cparams(("arbitrary",)),
        name="rwkv",
    )(*args)
    return (outs[0], None) if mix_v else (outs[0], outs[1])


def _merge_body(oa_ref, ob_ref, oc_ref, od_ref, g0_ref, g1_ref, g2_ref, g3_ref, wb_ref, out_ref):
    o_refs = (oa_ref, ob_ref, oc_ref, od_ref)
    g_refs = (g0_ref, g1_ref, g2_ref, g3_ref)
    acc = None
    for n in range(N_BRANCH):
        contrib = _sigmoid(g_refs[n][...].astype(F32)) * _dot(o_refs[n][...], wb_ref[n])
        acc = contrib if acc is None else acc + contrib
    out_ref[...] = acc.astype(BF16)


def _merge(o_a, o_b, o_c, o_d, pin, w_branch, layer, tm=512):
    S = pin.shape[0]
    D = D_MODEL
    ospec = pl.BlockSpec((tm, BRANCH_W), lambda i: (i, 0))
    gspec = lambda n: pl.BlockSpec((tm, D), lambda i: (i, n))
    return pl.pallas_call(
        _merge_body,
        grid=(S // tm,),
        in_specs=[ospec, ospec, ospec, ospec, gspec(0), gspec(1), gspec(2), gspec(3),
                  pl.BlockSpec((None, N_BRANCH, BRANCH_W, D), lambda i: (layer, 0, 0, 0))],
        out_specs=pl.BlockSpec((tm, D), lambda i: (i, 0)),
        out_shape=jax.ShapeDtypeStruct((S, D), BF16),
        compiler_params=_cparams(("arbitrary",)),
        name="merge",
    )(o_a, o_b, o_c, o_d, pin, pin, pin, pin, w_branch)


def _outproj_body(m_ref, w_ref, x_ref, o_ref):
    o_ref[...] = x_ref[...] + _dot(m_ref[...], w_ref[...])


def _outproj(merged, w_out, x, layer, tm=512, tn=2048):
    S, D = x.shape
    return pl.pallas_call(
        _outproj_body,
        grid=(S // tm, D // tn),
        in_specs=[
            pl.BlockSpec((tm, D), lambda i, j: (i, 0)),
            pl.BlockSpec((None, D, tn), lambda i, j: (layer, 0, j)),
            pl.BlockSpec((tm, tn), lambda i, j: (i, j)),
        ],
        out_specs=pl.BlockSpec((tm, tn), lambda i, j: (i, j)),
        out_shape=jax.ShapeDtypeStruct((S, D), F32),
        compiler_params=_cparams(("arbitrary", "arbitrary")),
        name="outproj",
    )(merged, w_out, x)


def _ffn_body(x_ref, nw_ref, wg_ref, wu_ref, wd_ref, out_ref, h_ref, acc_ref):
    f = pl.program_id(1)

    @pl.when(f == 0)
    def _():
        x = x_ref[...]
        ms = jnp.mean(x * x, axis=-1, keepdims=True)
        h_ref[...] = (x * lax.rsqrt(ms + RMS_EPS) * nw_ref[...]).astype(BF16)
        acc_ref[...] = jnp.zeros_like(acc_ref)

    h = h_ref[...]
    act = (_silu(_dot(h, wg_ref[...])) * _dot(h, wu_ref[...])).astype(BF16)
    acc_ref[...] += _dot(act, wd_ref[...])

    @pl.when(f == pl.num_programs(1) - 1)
    def _():
        out_ref[...] = x_ref[...] + acc_ref[...]


def _ffn(x, norm_w, w_gu, w_down, layer, tm=512, tf=512):
    S, D = x.shape
    nf = D_FF // tf
    return pl.pallas_call(
        _ffn_body,
        grid=(S // tm, nf),
        in_specs=[
            pl.BlockSpec((tm, D), lambda i, f: (i, 0)),
            pl.BlockSpec((None, 1, D), lambda i, f: (layer, 0, 0)),
            pl.BlockSpec((None, D, tf), lambda i, f: (layer, 0, f)),
            pl.BlockSpec((None, D, tf), lambda i, f: (layer, 0, f + nf)),
            pl.BlockSpec((None, tf, D), lambda i, f: (layer, f, 0)),
        ],
        out_specs=pl.BlockSpec((tm, D), lambda i, f: (i, 0)),
        out_shape=jax.ShapeDtypeStruct((S, D), F32),
        scratch_shapes=[pltpu.VMEM((tm, D), BF16), pltpu.VMEM((tm, D), F32)],
        compiler_params=_cparams(("arbitrary", "arbitrary")),
        name="ffn",
    )(x, norm_w, w_gu, w_gu, w_down)


def _pad_last(a, width):
    pad = width - a.shape[-1]
    if pad == 0:
        return a
    return jnp.pad(a, [(0, 0)] * (a.ndim - 1) + [(0, pad)])


def _pad_axis(a, axis, width):
    pad = width - a.shape[axis]
    if pad == 0:
        return a
    cfg = [(0, 0)] * a.ndim
    cfg[axis] = (0, pad)
    return jnp.pad(a, cfg)


_MAIN_PIECES = (
    (_O_GATE, P_GATE, N_BRANCH * D_MODEL),
    (_O_RQ, P_RQ, 512), (_O_RK, P_RK, 512), (_O_RV, P_RV, 512), (_O_RG, P_RG, 512),
    (_O_SXBC, P_SXBC, SSM_XBC), (_O_SZ, P_SZ, 512),
    (_O_CQ, P_CQ, MLA_Q_RANK),
    (_O_RW, P_WR, RWKV_W), (_O_RW + RWKV_W, P_WK, RWKV_W), (_O_RW + 2 * RWKV_W, P_WV, RWKV_W),
    (_O_CKV, P_CKV, MLA_KV_RANK),
)
_TAIL_PIECES = (
    (_O_SDT, T_DT, SSM_HEADS),
    (_O_KROPE, T_KROPE, MLA_ROPE),
    (_O_RW + 3 * RWKV_W, T_WL, RWKV_W_LORA),
    (_O_RW + 3 * RWKV_W + RWKV_W_LORA, T_AL, RWKV_A_LORA),
    (_O_RW + 3 * RWKV_W + RWKV_W_LORA + RWKV_A_LORA, T_GL, RWKV_G_LORA),
)
PACK_ROWS = 512


def _pack_tables(pieces, total, tile):
    src_of, valid_of = [], []
    for t in range(total // tile):
        lo = t * tile
        src, valid = 0, 0
        for s0, d0, width in pieces:
            if d0 <= lo < d0 + width:
                src, valid = s0 + (lo - d0), min(tile, d0 + width - lo)
        assert src % SUBLANES == 0
        src_of.append(src // SUBLANES)
        valid_of.append(valid)
    return jnp.asarray(src_of, jnp.int32), jnp.asarray(valid_of, jnp.int32)


def _pack_body(src_ref, valid_ref, w_ref, o_ref):
    w = w_ref[0]
    rows = lax.broadcasted_iota(jnp.int32, w.shape, 0)
    keep = rows < valid_ref[pl.program_id(1)]
    o_ref[...] = jnp.where(keep, w, 0.0).astype(BF16)


def _pack_rows(w_t, pieces, total, tile, name):
    depth, _, d = w_t.shape
    src_of, valid_of = _pack_tables(pieces, total, tile)
    grid_spec = pltpu.PrefetchScalarGridSpec(
        num_scalar_prefetch=2,
        grid=(depth, total // tile),
        in_specs=[pl.BlockSpec((pl.Element(1), pl.Element(tile), pl.Element(d)),
                               lambda l, t, src, valid: (l, src[t] * SUBLANES, 0))],
        out_specs=pl.BlockSpec((None, tile, d), lambda l, t, src, valid: (l, t, 0)),
    )
    return pl.pallas_call(
        _pack_body,
        grid_spec=grid_spec,
        out_shape=jax.ShapeDtypeStruct((depth, total, d), BF16),
        compiler_params=_cparams(("arbitrary", "arbitrary")),
        name=name,
    )(src_of, valid_of, w_t)


def _pack_w_in(w_in):
    w_t = jnp.swapaxes(w_in, 1, 2)
    w_main_t = _pack_rows(w_t, _MAIN_PIECES, N_MAIN, PACK_ROWS, "pack_w_main")
    w_tail_t = _pack_rows(w_t, _TAIL_PIECES, N_TAIL, LANES, "pack_w_tail")
    return w_main_t, w_tail_t


def _rope_tables(positions, dim):
    inv = 1.0 / (ROPE_THETA ** (jnp.arange(0, dim, 2, dtype=F32) / dim))
    ang = positions.astype(F32)[:, None] * inv
    return jnp.cos(ang), jnp.sin(ang)


def kernel(x, positions, norm1_w, w_in, ret_gn_w, ssm_conv_w, ssm_conv_b, ssm_dt_bias, ssm_a_log, ssm_d, ssm_norm_w, mla_q_a_norm_w, mla_w_qb, mla_kv_a_norm_w, mla_w_kvb, mla_q_norm_w, mla_k_norm_w, rwkv_mu, rwkv_w0, rwkv_w2, rwkv_a0, rwkv_a2, rwkv_g2, rwkv_v0, rwkv_v1, rwkv_v2, rwkv_k_k, rwkv_k_a, rwkv_r_k, rwkv_ln_w, rwkv_ln_b, w_branch, w_out, norm2_w, ffn_w_gu, ffn_w_down):
    B, S, D = x.shape
    assert B == 1 and D == D_MODEL and S % 1024 == 0
    xs = x[0]
    pos = positions[0]

    c_r, s_r = _rope_tables(pos, RET_DK)
    cos_ret = jnp.concatenate([c_r, c_r], axis=-1)
    sin_ret = jnp.concatenate([-s_r, s_r], axis=-1)
    c_m, s_m = _rope_tables(pos, MLA_ROPE)
    cos_mla = _pad_last(jnp.concatenate([c_m, c_m], axis=-1), LANES)
    sin_mla = _pad_last(jnp.concatenate([-s_m, s_m], axis=-1), LANES)

    row = lambda a: a[:, None, :]
    w_main, w_tail = _pack_w_in(w_in)
    norm1 = row(norm1_w)
    norm2 = row(norm2_w)
    gn_w = row(ret_gn_w)
    conv_b = row(ssm_conv_b)
    d_skip = row(ssm_d)
    ssm_nw = row(ssm_norm_w)
    dt_bias = row(_pad_last(ssm_dt_bias, LANES))
    a_neg = row(_pad_last(-jnp.exp(ssm_a_log.astype(F32)), LANES))
    qaw = row(mla_q_a_norm_w)
    kvaw = row(mla_kv_a_norm_w)
    wqb = _pad_last(mla_w_qb.reshape(DEPTH, MLA_Q_RANK, MLA_HEADS, MLA_QK), MLA_PAD)
    wqb = wqb.reshape(DEPTH, MLA_Q_RANK, MLA_HEADS * MLA_PAD).astype(BF16)
    wkvb = mla_w_kvb.astype(BF16)
    qnw = row(_pad_last(mla_q_norm_w, MLA_PAD))
    knw_n = row(mla_k_norm_w[:, :MLA_NOPE])
    knw_r = row(_pad_last(mla_k_norm_w[:, MLA_NOPE:], LANES))
    W = RWKV_W
    mu = rwkv_mu
    vec = {
        "mu_r": row(mu[:, :W]), "mu_k": row(mu[:, W:2 * W]), "mu_v": row(mu[:, 2 * W:3 * W]),
        "mu_wl": row(_pad_last(mu[:, 3 * W:3 * W + RWKV_W_LORA], LANES)),
        "mu_al": row(_pad_last(mu[:, 3 * W + RWKV_W_LORA:3 * W + RWKV_W_LORA + RWKV_A_LORA], LANES)),
        "mu_gl": row(mu[:, 3 * W + RWKV_W_LORA + RWKV_A_LORA:]),
        "w0": row(rwkv_w0), "a0": row(rwkv_a0), "k_k": row(rwkv_k_k), "k_a": row(rwkv_k_a),
        "ln_w": row(rwkv_ln_w), "ln_b": row(rwkv_ln_b), "r_k": row(rwkv_r_k.reshape(DEPTH, W)),
    }
    mats = {"w2": _pad_axis(rwkv_w2, 1, LANES), "a2": _pad_axis(rwkv_a2, 1, LANES), "g2": rwkv_g2}
    v0 = row(rwkv_v0)
    v1 = _pad_last(rwkv_v1, LANES)
    v2 = _pad_axis(rwkv_v2, 1, LANES)
    wb = w_branch.astype(BF16)
    wo = w_out.astype(BF16)
    wgu = ffn_w_gu.astype(BF16)
    wdn = ffn_w_down.astype(BF16)

    v_first = None
    for l in range(DEPTH):
        pin, tail = _inproj(xs, norm1, w_main, w_tail, l)
        o_a = _retention(pin, cos_ret, sin_ret, gn_w, l)
        o_b = _ssd(pin, tail, dt_bias, a_neg, ssm_conv_w, conv_b, d_skip, ssm_nw, l)
        q, k, v = _mla_prep(pin, tail, qaw, wqb, kvaw, wkvb, qnw, knw_n, knw_r, cos_mla, sin_mla, l)
        o_c = _flash(q, k, v)
        vmix = None if l == 0 else (v0, v1, v2)
        o_d, vf = _rwkv(pin, tail, vec, mats, v_first, vmix, l)
        if l == 0:
            v_first = vf
        merged = _merge(o_a, o_b, o_c, o_d, pin, wb, l)
        xs = _outproj(merged, wo, xs, l)
        xs = _ffn(xs, norm2, wgu, wdn, l)
    return xs[None]
```

```python
import functools
import math

import numpy as np
import jax
import jax.numpy as jnp
from jax import lax
from jax.experimental import pallas as pl
from jax.experimental.pallas import tpu as pltpu

F32 = jnp.float32
BF16 = jnp.bfloat16

D_MODEL = 2048
DEPTH = 4
CHUNK = 64
N_BRANCH = 4
BRANCH_W = 512
RMS_EPS = 1e-6
GN_EPS = 1e-5
ROPE_THETA = 10000.0
RET_HEADS, RET_DK, RET_DV = 4, 128, 128
SSM_HEADS, SSM_HEADDIM, SSM_GROUPS, SSM_STATE, SSM_CONV = 8, 64, 2, 128, 4
SSM_XBC = SSM_HEADS * SSM_HEADDIM + 2 * SSM_GROUPS * SSM_STATE
MLA_HEADS, MLA_Q_RANK, MLA_KV_RANK, MLA_NOPE, MLA_ROPE, MLA_V = 4, 512, 256, 128, 64, 128
MLA_QK = MLA_NOPE + MLA_ROPE
RWKV_HEADS, RWKV_HEAD = 8, 64
RWKV_W = RWKV_HEADS * RWKV_HEAD
RWKV_W_LORA, RWKV_A_LORA, RWKV_V_LORA, RWKV_G_LORA = 64, 64, 32, 128
RWKV_GN_EPS = 64e-5
D_FF = 5632

LANES = 128
SUBLANES = 8
VMEM_LIMIT_BYTES = 56 * 1024 * 1024

_O_RQ, _O_RK, _O_RV, _O_RG = 0, 512, 1024, 1536
_O_SZ, _O_SXBC, _O_SDT = 2048, 2560, 3584
_O_CQ, _O_CKV, _O_KROPE = 3592, 4104, 4360
_O_RW = 4424
_O_GATE = 6216
P_GATE = 0
P_RQ, P_RK, P_RV, P_RG = 8192, 8704, 9216, 9728
P_SXBC, P_SZ = 10240, 11264
P_CQ = 11776
P_WR, P_WK, P_WV = 12288, 12800, 13312
P_CKV = 13824
N_MAIN = 14336
T_DT, T_KROPE, T_WL, T_AL, T_GL = 0, 128, 256, 384, 512
N_TAIL = 640

ROW_BLOCK = 256
RWKV_L = 64
LOG2E = 1.4426950408889634


def _cparams(sem):
    return pltpu.CompilerParams(dimension_semantics=sem, vmem_limit_bytes=VMEM_LIMIT_BYTES)


def _dot(a, b):
    return jnp.dot(a, b, preferred_element_type=F32)


def _dot_nt(a, b):
    return lax.dot_general(a, b, (((1,), (1,)), ((), ())), preferred_element_type=F32)


def _dot_tn(a, b):
    return lax.dot_general(a, b, (((0,), (0,)), ((), ())), preferred_element_type=F32)


def _split_bf16(x):
    hi = x.astype(BF16)
    lo = (x - hi.astype(F32)).astype(BF16)
    return hi, lo


def _mm3(a, b, dot=_dot):
    ah, al = _split_bf16(a)
    bh, bl = _split_bf16(b)
    return dot(ah, bh) + dot(ah, bl) + dot(al, bh)


def _mm1(a, b, dot=_dot):
    return dot(a.astype(BF16), b.astype(BF16))


def _mm_exact_lhs(a01, x, dot=_dot):
    xh, xl = _split_bf16(x)
    a = a01.astype(BF16)
    return dot(a, xh) + dot(a, xl)


def _seg_sum(x, ones_bd):
    xh, xl = _split_bf16(x)
    return _dot(xh, ones_bd) + _dot(xl, ones_bd)


def _sigmoid(x):
    return 0.5 * jnp.tanh(0.5 * x) + 0.5


def _silu(x):
    return x * _sigmoid(x)


def _softplus(x):
    return jnp.maximum(x, 0.0) + jnp.log(1.0 + jnp.exp(-jnp.abs(x)))


def _shift_rows(x, carry8, s):
    xr = pltpu.roll(x, s, 0)
    pr = pltpu.roll(carry8, s, 0)
    row = lax.broadcasted_iota(jnp.int32, carry8.shape, 0)
    top = jnp.where(row < s, pr, xr[:SUBLANES])
    return jnp.concatenate([top, xr[SUBLANES:]], axis=0)


def _inproj_body(x_ref, nw_ref, w_ref, wt_ref, o_ref, ot_ref, h_ref):
    @pl.when(pl.program_id(1) == 0)
    def _():
        x = x_ref[...]
        ms = jnp.mean(x * x, axis=-1, keepdims=True)
        h = (x * lax.rsqrt(ms + RMS_EPS) * nw_ref[...]).astype(BF16)
        h_ref[...] = h
        ot_ref[...] = _dot_nt(h, wt_ref[...])

    o_ref[...] = _dot_nt(h_ref[...], w_ref[...]).astype(BF16)


def _inproj(x, norm_w, w_main, w_tail, layer, tm=1024, tn=1024):
    S, D = x.shape
    return pl.pallas_call(
        _inproj_body,
        grid=(S // tm, N_MAIN // tn),
        in_specs=[
            pl.BlockSpec((tm, D), lambda i, j: (i, 0)),
            pl.BlockSpec((None, 1, D), lambda i, j: (layer, 0, 0)),
            pl.BlockSpec((None, tn, D), lambda i, j: (layer, j, 0)),
            pl.BlockSpec((None, N_TAIL, D), lambda i, j: (layer, 0, 0)),
        ],
        out_specs=[pl.BlockSpec((tm, tn), lambda i, j: (i, j)),
                   pl.BlockSpec((tm, N_TAIL), lambda i, j: (i, 0))],
        out_shape=[jax.ShapeDtypeStruct((S, N_MAIN), BF16), jax.ShapeDtypeStruct((S, N_TAIL), F32)],
        scratch_shapes=[pltpu.VMEM((tm, D), BF16)],
        compiler_params=_cparams(("arbitrary", "arbitrary")),
        name="inproj",
    )(x, norm_w, w_main, w_tail)


def _ret_body(q_ref, k_ref, v_ref, g_ref, cos_ref, sin_ref, m_ref, qd_ref, kd_ref, gnw_ref,
              o_ref, s_ref, *, block_decay):
    @pl.when(pl.program_id(0) == 0)
    def _():
        s_ref[...] = jnp.zeros_like(s_ref)

    cos = cos_ref[...]
    sin = sin_ref[...]
    heads = range(RET_HEADS)
    lanes = [slice(h * RET_DK, (h + 1) * RET_DK) for h in heads]
    q, k, vb = [], [], []
    for h in heads:
        qh = q_ref[:, lanes[h]].astype(F32)
        kh = k_ref[:, lanes[h]].astype(F32)
        q.append(qh * cos + pltpu.roll(qh, RET_DK // 2, 1) * sin)
        k.append((kh * cos + pltpu.roll(kh, RET_DK // 2, 1) * sin) * (RET_DK ** -0.5))
        vb.append(v_ref[:, lanes[h]])
    sc = [(_dot_nt(q[h].astype(BF16), k[h].astype(BF16)) * m_ref[h]).astype(BF16) for h in heads]
    st = [s_ref[h] for h in heads]
    o = [_dot(sc[h], vb[h]) + _dot((q[h] * qd_ref[h]).astype(BF16), st[h].astype(BF16)) for h in heads]
    for h in heads:
        s_ref[h] = block_decay[h] * st[h] + _dot_tn((k[h] * kd_ref[h]).astype(BF16), vb[h])
    oc = [o[h] - jnp.mean(o[h], axis=-1, keepdims=True) for h in heads]
    var = [jnp.mean(oc[h] * oc[h], axis=-1, keepdims=True) for h in heads]
    for h in heads:
        on = oc[h] * lax.rsqrt(var[h] + GN_EPS) * gnw_ref[:, lanes[h]]
        o_ref[:, lanes[h]] = (_silu(g_ref[:, lanes[h]].astype(F32)) * on).astype(BF16)


def _ret_tables(tb):
    lg = np.log1p(-np.exp2(-5.0 - np.arange(RET_HEADS, dtype=np.float64)))
    pos = np.arange(tb, dtype=np.float64)
    dist = np.abs(pos[:, None] - pos[None, :])
    visible = (pos[None, :] // CHUNK) <= (pos[:, None] // CHUNK)
    mask = np.where(visible[None], np.exp(lg[:, None, None] * dist[None]), 0.0)
    qd = np.exp(lg[:, None] * (pos[None, :] + 1.0))[:, :, None]
    kd = np.exp(lg[:, None] * (tb - 1.0 - pos[None, :]))[:, :, None]
    bd = tuple(float(v) for v in np.exp(lg * tb))
    return (jnp.asarray(mask, F32), jnp.asarray(qd, F32), jnp.asarray(kd, F32), bd)


def _retention(pin, cos2, sin2, gn_w, layer, tb=ROW_BLOCK):
    S = pin.shape[0]
    mask, qd, kd, bd = _ret_tables(tb)
    w = RET_HEADS * RET_DK
    col = lambda off: (lambda i: (i, off // w))
    full3 = lambda i: (0, 0, 0)
    return pl.pallas_call(
        functools.partial(_ret_body, block_decay=bd),
        grid=(S // tb,),
        in_specs=[
            pl.BlockSpec((tb, w), col(P_RQ)),
            pl.BlockSpec((tb, w), col(P_RK)),
            pl.BlockSpec((tb, w), col(P_RV)),
            pl.BlockSpec((tb, w), col(P_RG)),
            pl.BlockSpec((tb, RET_DK), lambda i: (i, 0)),
            pl.BlockSpec((tb, RET_DK), lambda i: (i, 0)),
            pl.BlockSpec((RET_HEADS, tb, tb), full3),
            pl.BlockSpec((RET_HEADS, tb, 1), full3),
            pl.BlockSpec((RET_HEADS, tb, 1), full3),
            pl.BlockSpec((None, 1, w), lambda i: (layer, 0, 0)),
        ],
        out_specs=pl.BlockSpec((tb, w), lambda i: (i, 0)),
        out_shape=jax.ShapeDtypeStruct((S, w), BF16),
        scratch_shapes=[pltpu.VMEM((RET_HEADS, RET_DK, RET_DV), F32)],
        compiler_params=_cparams(("arbitrary",)),
        name="retention",
    )(pin, pin, pin, pin, cos2, sin2, mask, qd, kd, gn_w)


def _ssd_body(xbc_ref, z_ref, dt_ref, dtb_ref, aneg_ref, cw_ref, cb_ref, d_ref, nw_ref,
              o_ref, carry_ref, st_ref):
    tb = xbc_ref.shape[0]
    P, N = SSM_HEADDIM, SSM_STATE
    nx = SSM_HEADS * P

    @pl.when(pl.program_id(0) == 0)
    def _():
        carry_ref[...] = jnp.zeros_like(carry_ref)
        st_ref[...] = jnp.zeros_like(st_ref)

    x = xbc_ref[...].astype(F32)
    carry = carry_ref[...]
    acc = x * cw_ref[SSM_CONV - 1:SSM_CONV, :] + cb_ref[...]
    for s in range(1, SSM_CONV):
        acc = acc + _shift_rows(x, carry, s) * cw_ref[SSM_CONV - 1 - s:SSM_CONV - s, :]
    carry_ref[...] = x[tb - SUBLANES:, :]
    xbc = _silu(acc)

    dt = _softplus(dt_ref[...] + dtb_ref[...])
    adt = dt * aneg_ref[...]
    row = lax.broadcasted_iota(jnp.int32, (tb, tb), 0)
    colm = lax.broadcasted_iota(jnp.int32, (tb, tb), 1)
    causal = colm <= row
    tril = causal.astype(F32)
    a_col = jnp.dot(tril, adt, preferred_element_type=F32, precision=lax.Precision.HIGHEST)
    triu = (row <= colm).astype(F32)
    a_row = lax.dot_general(adt, triu, (((0,), (0,)), ((), ())),
                            preferred_element_type=F32, precision=lax.Precision.HIGHEST)
    a_end = a_col[tb - 1:tb, :]
    lane = lax.broadcasted_iota(jnp.int32, (tb, 2 * P), 1)
    first = lane < P

    groups = range(SSM_GROUPS)
    heads = range(SSM_HEADS)
    hpg = SSM_HEADS // SSM_GROUPS
    npair = SSM_HEADS // 2
    bm = [xbc[:, nx + g * N: nx + (g + 1) * N] for g in groups]
    cm = [xbc[:, nx + SSM_GROUPS * N + g * N: nx + SSM_GROUPS * N + (g + 1) * N] for g in groups]
    cb = [_dot_nt(cm[g].astype(BF16), bm[g].astype(BF16)) for g in groups]
    xp = [xbc[:, 2 * q * P:(2 * q + 2) * P] for q in range(npair)]
    xdt = [(xp[q] * jnp.where(first, dt[:, 2 * q:2 * q + 1], dt[:, 2 * q + 1:2 * q + 2])).astype(BF16)
           for q in range(npair)]
    st = [st_ref[q] for q in range(npair)]
    ac = [a_col[:, h:h + 1] for h in heads]
    dec = [jnp.exp(jnp.where(causal, ac[h] - a_row[h:h + 1, :], -jnp.inf)) for h in heads]
    wts = [(cb[h // hpg] * dec[h]).astype(BF16) for h in heads]
    y = [_dot(wts[h], xdt[h // 2]) for h in heads]
    cdec = [(cm[h // hpg] * jnp.exp(ac[h])).astype(BF16) for h in heads]
    y = [y[h] + _dot(cdec[h], st[h // 2].astype(BF16)) for h in heads]
    bdec = [(bm[h // hpg] * jnp.exp(a_end[:, h:h + 1] - ac[h])).astype(BF16) for h in heads]
    snew = [jnp.exp(a_end[:, h:h + 1]) * st[h // 2] + _dot_tn(bdec[h], xdt[h // 2]) for h in heads]
    lane_s = lax.broadcasted_iota(jnp.int32, (N, 2 * P), 1)
    for q in range(npair):
        st_ref[q] = jnp.where(lane_s < P, snew[2 * q], snew[2 * q + 1])
    ypair = [jnp.where(first, y[2 * q], y[2 * q + 1])
             + xp[q] * jnp.where(first, d_ref[:, 2 * q:2 * q + 1], d_ref[:, 2 * q + 1:2 * q + 2])
             for q in range(npair)]
    for g in groups:
        yg = jnp.concatenate(ypair[g * hpg // 2:(g + 1) * hpg // 2], axis=1)
        gsl = slice(g * hpg * P, (g + 1) * hpg * P)
        yg = yg * _silu(z_ref[:, gsl].astype(F32))
        ms = jnp.mean(yg * yg, axis=-1, keepdims=True)
        o_ref[:, gsl] = (yg * lax.rsqrt(ms + RMS_EPS) * nw_ref[:, gsl]).astype(BF16)


def _ssd(pin, tail, dt_bias, a_neg, conv_w, conv_b, d_skip, norm_w, layer, tb=ROW_BLOCK):
    S = pin.shape[0]
    nx = SSM_HEADS * SSM_HEADDIM
    H = SSM_HEADS
    lsel = lambda i: (layer, 0, 0)
    return pl.pallas_call(
        _ssd_body,
        grid=(S // tb,),
        in_specs=[
            pl.BlockSpec((tb, SSM_XBC), lambda i: (i, P_SXBC // SSM_XBC)),
            pl.BlockSpec((tb, nx), lambda i: (i, P_SZ // nx)),
            pl.BlockSpec((tb, LANES), lambda i: (i, T_DT // LANES)),
            pl.BlockSpec((None, 1, LANES), lsel),
            pl.BlockSpec((None, 1, LANES), lsel),
            pl.BlockSpec((None, SSM_CONV, SSM_XBC), lsel),
            pl.BlockSpec((None, 1, SSM_XBC), lsel),
            pl.BlockSpec((None, 1, H), lsel),
            pl.BlockSpec((None, 1, nx), lsel),
        ],
        out_specs=pl.BlockSpec((tb, nx), lambda i: (i, 0)),
        out_shape=jax.ShapeDtypeStruct((S, nx), BF16),
        scratch_shapes=[pltpu.VMEM((SUBLANES, SSM_XBC), F32),
                        pltpu.VMEM((H // 2, SSM_STATE, 2 * SSM_HEADDIM), F32)],
        compiler_params=_cparams(("arbitrary",)),
        name="ssd",
    )(pin, pin, tail, dt_bias, a_neg, conv_w, conv_b, d_skip, norm_w)


MLA_PAD = 256
MLA_VX = MLA_V + 16


def _rope_pad(x, cosp, sinp):
    half = MLA_ROPE // 2
    return x * cosp + (pltpu.roll(x, half, 1) + pltpu.roll(x, LANES - half, 1)) * sinp


def _mla_prep_body(cq_ref, ckv_ref, kr_ref, qaw_ref, wqb_ref, kvaw_ref, wkvb_ref, qnw_ref,
                   knw_n_ref, knw_r_ref, cos_ref, sin_ref, q_out, k_out, v_out):
    cosp = cos_ref[...]
    sinp = sin_ref[...]
    cq = cq_ref[...].astype(F32)
    ms = jnp.mean(cq * cq, axis=-1, keepdims=True)
    cqn = (cq * lax.rsqrt(ms + RMS_EPS) * qaw_ref[...]).astype(BF16)
    q = _dot(cqn, wqb_ref[...])
    ckv = ckv_ref[...].astype(F32)
    ms = jnp.mean(ckv * ckv, axis=-1, keepdims=True)
    ckvn = (ckv * lax.rsqrt(ms + RMS_EPS) * kvaw_ref[...]).astype(BF16)
    kv = _dot(ckvn, wkvb_ref[...])
    kr = kr_ref[...]
    ssr = jnp.sum(kr * kr, axis=-1, keepdims=True)
    scale = (MLA_QK ** -0.5) * LOG2E
    heads = range(MLA_HEADS)
    qh = [q[:, h * MLA_PAD:(h + 1) * MLA_PAD] for h in heads]
    kn = [kv[:, h * MLA_PAD: h * MLA_PAD + MLA_NOPE] for h in heads]
    qinv = [lax.rsqrt(jnp.sum(qh[h] * qh[h], axis=-1, keepdims=True) * (1.0 / MLA_QK) + RMS_EPS) for h in heads]
    kinv = [lax.rsqrt((jnp.sum(kn[h] * kn[h], axis=-1, keepdims=True) + ssr) * (1.0 / MLA_QK) + RMS_EPS)
            for h in heads]
    qh = [qh[h] * qinv[h] * qnw_ref[...] for h in heads]
    qr = [_rope_pad(qh[h][:, MLA_NOPE:], cosp, sinp) for h in heads]
    krh = [_rope_pad(kr * kinv[h] * knw_r_ref[...], cosp, sinp) for h in heads]
    ones = jnp.ones((MLA_VX - MLA_V, kv.shape[0]), F32)
    for h in heads:
        q_out[h] = (jnp.concatenate([qh[h][:, :MLA_NOPE], qr[h]], axis=1) * scale).T.astype(BF16)
    for h in heads:
        k_out[h] = jnp.concatenate([kn[h] * kinv[h] * knw_n_ref[...], krh[h]], axis=1).astype(BF16)
    for h in heads:
        vv = kv[:, h * MLA_PAD + MLA_NOPE:(h + 1) * MLA_PAD]
        v_out[h] = jnp.concatenate([vv.T, ones], axis=0).astype(BF16)


def _mla_prep(pin, tail, qaw, wqb, kvaw, wkvb, qnw, knw_n, knw_r, cosp, sinp, layer, tb=ROW_BLOCK):
    S = pin.shape[0]
    H = MLA_HEADS
    lsel = lambda i: (layer, 0, 0)
    return pl.pallas_call(
        _mla_prep_body,
        grid=(S // tb,),
        in_specs=[
            pl.BlockSpec((tb, MLA_Q_RANK), lambda i: (i, P_CQ // MLA_Q_RANK)),
            pl.BlockSpec((tb, MLA_KV_RANK), lambda i: (i, P_CKV // MLA_KV_RANK)),
            pl.BlockSpec((tb, LANES), lambda i: (i, T_KROPE // LANES)),
            pl.BlockSpec((None, 1, MLA_Q_RANK), lsel),
            pl.BlockSpec((None, MLA_Q_RANK, H * MLA_PAD), lsel),
            pl.BlockSpec((None, 1, MLA_KV_RANK), lsel),
            pl.BlockSpec((None, MLA_KV_RANK, H * MLA_PAD), lsel),
            pl.BlockSpec((None, 1, MLA_PAD), lsel),
            pl.BlockSpec((None, 1, MLA_NOPE), lsel),
            pl.BlockSpec((None, 1, LANES), lsel),
            pl.BlockSpec((tb, LANES), lambda i: (i, 0)),
            pl.BlockSpec((tb, LANES), lambda i: (i, 0)),
        ],
        out_specs=[
            pl.BlockSpec((H, MLA_PAD, tb), lambda i: (0, 0, i)),
            pl.BlockSpec((H, tb, MLA_PAD), lambda i: (0, i, 0)),
            pl.BlockSpec((H, MLA_VX, tb), lambda i: (0, 0, i)),
        ],
        out_shape=[
            jax.ShapeDtypeStruct((H, MLA_PAD, S), BF16),
            jax.ShapeDtypeStruct((H, S, MLA_PAD), BF16),
            jax.ShapeDtypeStruct((H, MLA_VX, S), BF16),
        ],
        compiler_params=_cparams(("arbitrary",)),
        name="mla_prep",
    )(pin, pin, tail, qaw, wqb, kvaw, wkvb, qnw, knw_n, knw_r, cosp, sinp)


def _flash_body(it_ref, jt_ref, qt_ref, k_ref, vt_ref, o_ref, m_ref, acc_ref):
    t = pl.program_id(0)
    i = it_ref[t]
    j = jt_ref[t]
    tq = qt_ref.shape[2]
    tk = k_ref.shape[1]
    full = (j + 1) * tk <= i * tq
    last = j == ((i + 1) * tq - 1) // tk
    H = MLA_HEADS

    @pl.when(j == 0)
    def _():
        m_ref[...] = jnp.full_like(m_ref, -jnp.inf)
        acc_ref[...] = jnp.zeros_like(acc_ref)

    def step(diagonal):
        if diagonal:
            kc = (j * tk + lax.broadcasted_iota(jnp.int32, (tk, tq), 0)) // CHUNK
            qc = (i * tq + lax.broadcasted_iota(jnp.int32, (tk, tq), 1)) // CHUNK
            visible = kc <= qc

        def logits(h):
            s = _dot(k_ref[h], qt_ref[h])
            return jnp.where(visible, s, -jnp.inf) if diagonal else s

        def softmax(h, s):
            m_prev = m_ref[h]
            m_new = jnp.maximum(m_prev, jnp.max(s, axis=0, keepdims=True))
            alpha = jnp.exp2(m_prev - m_new)
            m_ref[h] = m_new
            return alpha, jnp.exp2((s - m_new).astype(BF16))

        def accumulate(h, alpha, p):
            acc_ref[h] = alpha * acc_ref[h] + _dot(vt_ref[h], p)

        s_next = logits(0)
        pending = None
        for h in range(H):
            s_cur = s_next
            if h + 1 < H:
                s_next = logits(h + 1)
            if pending is not None:
                accumulate(*pending)
            pending = (h,) + softmax(h, s_cur)
        accumulate(*pending)

    @pl.when(full)
    def _():
        step(False)

    @pl.when(jnp.logical_not(full))
    def _():
        step(True)

    @pl.when(last)
    def _():
        for h in range(H):
            acc = acc_ref[h]
            o_ref[:, h * MLA_V:(h + 1) * MLA_V] = (acc[:MLA_V] / acc[MLA_V:MLA_V + 1]).T.astype(BF16)


def _flash(qt, k, vt, tq=512, tk=1024):
    H, S, _ = k.shape
    pairs = [(i, j) for i in range(S // tq) for j in range(((i + 1) * tq - 1) // tk + 1)]
    it = jnp.asarray([p[0] for p in pairs], jnp.int32)
    jt = jnp.asarray([p[1] for p in pairs], jnp.int32)
    grid_spec = pltpu.PrefetchScalarGridSpec(
        num_scalar_prefetch=2,
        grid=(len(pairs),),
        in_specs=[
            pl.BlockSpec((H, MLA_PAD, tq), lambda t, it, jt: (0, 0, it[t])),
            pl.BlockSpec((H, tk, MLA_PAD), lambda t, it, jt: (0, jt[t], 0)),
            pl.BlockSpec((H, MLA_VX, tk), lambda t, it, jt: (0, 0, jt[t])),
        ],
        out_specs=pl.BlockSpec((tq, H * MLA_V), lambda t, it, jt: (it[t], 0)),
        scratch_shapes=[pltpu.VMEM((H, 1, tq), F32), pltpu.VMEM((H, MLA_VX, tq), F32)],
    )
    return pl.pallas_call(
        _flash_body,
        grid_spec=grid_spec,
        out_shape=jax.ShapeDtypeStruct((S, H * MLA_V), BF16),
        compiler_params=_cparams(("arbitrary",)),
        name="mla_flash",
    )(it, jt, qt, k, vt)


def _interleave(*streams):
    active = list(streams)
    while active:
        for g in list(active):
            try:
                next(g)
            except StopIteration:
                active.remove(g)


def _rwkv_body(*refs, mix_v):
    if mix_v:
        (r_ref, k_ref, v_ref, wl_ref, al_ref, gl_ref, mur, muk, muv, muwl, mual, mugl, w0, a0, w2, a2, g2,
         kk_ref, ka_ref, lnw_ref, lnb_ref, rk_ref, vf_ref, v0, v1, v2,
         o_ref, c_r, c_k, c_v, c_wl, c_al, c_gl, s_ref) = refs
    else:
        (r_ref, k_ref, v_ref, wl_ref, al_ref, gl_ref, mur, muk, muv, muwl, mual, mugl, w0, a0, w2, a2, g2,
         kk_ref, ka_ref, lnw_ref, lnb_ref, rk_ref,
         o_ref, vfirst_ref, c_r, c_k, c_v, c_wl, c_al, c_gl, s_ref) = refs
    tb = r_ref.shape[0]
    L = RWKV_L
    N = RWKV_HEAD
    L2 = 2 * L
    npair = RWKV_W // LANES
    half = tb // 2
    carries = (c_r, c_k, c_v, c_wl, c_al, c_gl)

    @pl.when(pl.program_id(0) == 0)
    def _():
        for c in carries:
            c[...] = jnp.zeros_like(c)
        s_ref[...] = jnp.zeros_like(s_ref)

    def mixed(x_ref, c_ref, mu_ref):
        x = x_ref[...].astype(F32)
        prev = _shift_rows(x, c_ref[...], 1)
        c_ref[...] = x[tb - SUBLANES:, :]
        return x + (prev - x) * mu_ref[...]

    r = mixed(r_ref, c_r, mur)
    k = mixed(k_ref, c_k, muk)
    v = mixed(v_ref, c_v, muv)
    wl = mixed(wl_ref, c_wl, muwl)
    al = mixed(al_ref, c_al, mual)
    gl = mixed(gl_ref, c_gl, mugl)
    if not mix_v:
        vfirst_ref[...] = v

    lane = lax.broadcasted_iota(jnp.int32, (L, LANES), 1)
    m0 = lane < N
    r2 = lax.broadcasted_iota(jnp.int32, (L2, L2), 0)
    c2 = lax.broadcasted_iota(jnp.int32, (L2, L2), 1)
    same = (r2 // L) == (c2 // L)
    strict = same & (c2 < r2)
    incl = same & (c2 <= r2)
    eye = (r2 == c2).astype(F32)
    ones_bd = ((r2 // N) == (c2 // N)).astype(BF16)
    rb = lax.broadcasted_iota(jnp.int32, (half, half), 0)
    cb = lax.broadcasted_iota(jnp.int32, (half, half), 1)
    tril_bd = ((rb // L) == (cb // L)) & (cb <= rb)

    def stack(x):
        return jnp.concatenate([jnp.where(m0, x, 0.0), jnp.where(m0, 0.0, x)], axis=0).astype(BF16)

    probs = [(c, p) for c in range(half // L) for p in range(npair)]
    rows_of = lambda c: slice(c * L, (c + 1) * L)
    lanes_of = lambda p: slice(p * LANES, (p + 1) * LANES)
    opnds = [{}, {}]
    gates = [None, None]
    states = [s_ref[p] for p in range(npair)]

    def front(hf):
        rs = slice(hf * half, (hf + 1) * half)
        w_raw = w0[...] + _mm3(jnp.tanh(wl[rs]), w2[...])
        lw = -jnp.exp(-_softplus(-w_raw) - 0.5)
        yield
        a_sig = _sigmoid(a0[...] + _mm3(al[rs], a2[...]))
        gates[hf] = _mm1(_sigmoid(gl[rs]), g2[...])
        yield
        vh = v[rs]
        if mix_v:
            lora = _mm1(_mm1(vh, v1[...]), v2[...])
            vh = vh + (vf_ref[rs, :] - vh) * _sigmoid(v0[...] + lora)
        yield
        kk = k[rs] * kk_ref[...]
        parts = []
        for p in range(npair):
            kp = kk[:, lanes_of(p)]
            n2 = _seg_sum(kp * kp, ones_bd)
            parts.append(kp / jnp.maximum(jnp.sqrt(n2), 1e-12))
            if p % 2:
                yield
        kk = jnp.concatenate(parts, axis=1)
        kh = k[rs] * (1.0 + (a_sig - 1.0) * ka_ref[...])
        ah = -kk
        bh = kk * a_sig
        rh = r[rs]
        cum_blk = _mm_exact_lhs(tril_bd, lw)
        yield
        for n, (c, p) in enumerate(probs):
            rows, sl = rows_of(c), lanes_of(p)
            cum = cum_blk[rows, sl]
            cum_end = cum[L - 1:L, :]
            rr = rh[rows, sl]
            kc = kh[rows, sl]
            vc = vh[rows, sl]
            e_neg = jnp.exp(-cum)
            e_end = jnp.exp(cum_end - cum)
            re = rr * jnp.exp(cum)
            xr_f = jnp.concatenate([jnp.where(m0, re, 0.0), jnp.where(m0, 0.0, re)], axis=0)
            opnds[hf][c, p] = dict(
                xr_f=xr_f, xa=stack(ah[rows, sl] * jnp.exp(cum - lw[rows, sl])), xr=xr_f.astype(BF16),
                xb=stack(bh[rows, sl] * e_neg), xk=stack(kc * e_neg), xbh=stack(bh[rows, sl] * e_end),
                xkh=stack(kc * e_end), vs=stack(vc), decay_end=jnp.exp(cum_end),
                rk=rr * kc * rk_ref[:, sl], v=vc)
            if n % 2:
                yield

    def back(hf):
        opnd = opnds[hf]
        amat = {}
        for cp in probs:
            o = opnd[cp]
            pmat = _dot_nt(jnp.concatenate([o["xa"], o["xr"]], axis=0),
                           jnp.concatenate([o["xb"], o["xk"]], axis=0))
            amat[cp] = dict(
                ab=jnp.where(strict, pmat[:L2, :L2], 0.0),
                ak=jnp.where(strict, pmat[:L2, L2:], 0.0).astype(BF16),
                rb=jnp.where(incl, pmat[L2:, :L2], 0.0).astype(BF16),
                rk=jnp.where(incl, pmat[L2:, L2:], 0.0).astype(BF16))
        yield
        tinv = {cp: eye + amat[cp]["ab"] for cp in probs}
        pw = {cp: amat[cp]["ab"] for cp in probs}
        for _ in range(int(math.log2(L)) - 1):
            for cp in probs:
                pwb = pw[cp].astype(BF16)
                pw[cp] = _dot(pwb, pwb)
            yield
            for cp in probs:
                tinv[cp] = tinv[cp] + _mm1(tinv[cp], pw[cp])
            yield
        w1 = {cp: _dot(amat[cp]["ak"], opnd[cp]["vs"]) for cp in probs}
        yield
        mub = {}
        for cp in probs:
            rhs = jnp.concatenate([opnd[cp]["xa"], w1[cp].astype(BF16)], axis=1)
            mub[cp] = _dot(tinv[cp].astype(BF16), rhs).astype(BF16)
        yield
        coef = {}
        for cp in probs:
            o = opnd[cp]
            ry = _dot(amat[cp]["rb"], mub[cp])
            gh = _dot_tn(o["xbh"], mub[cp])
            coef[cp] = dict(
                m_r=(o["xr_f"] + ry[:, :LANES]).astype(BF16),
                y1=ry[:, LANES:] + _dot(amat[cp]["rk"], o["vs"]),
                g=(eye * o["decay_end"] + gh[:, :LANES]).astype(BF16),
                h=gh[:, LANES:] + _dot_tn(o["xkh"], o["vs"]))
        yield
        ys = {}
        for (c, p) in probs:
            st = states[p].astype(BF16)
            cf = coef[c, p]
            yst = _dot(cf["m_r"], st) + cf["y1"]
            states[p] = _dot(cf["g"], st) + cf["h"]
            ys[c, p] = yst[:L] + yst[L:]
        yield
        yc = {cp: ys[cp] - _seg_sum(ys[cp], ones_bd) * (1.0 / N) for cp in probs}
        yield
        bonus = {cp: _seg_sum(opnd[cp]["rk"], ones_bd) * opnd[cp]["v"] for cp in probs}
        var = {cp: _seg_sum(yc[cp] * yc[cp], ones_bd) * (1.0 / N) for cp in probs}
        yield
        for (c, p) in probs:
            rows, sl = rows_of(c), lanes_of(p)
            out_rows = slice(hf * half + c * L, hf * half + (c + 1) * L)
            yn = yc[c, p] * lax.rsqrt(var[c, p] + RWKV_GN_EPS) * lnw_ref[:, sl] + lnb_ref[:, sl]
            o_ref[out_rows, sl] = ((yn + bonus[c, p]) * gates[hf][rows, sl]).astype(BF16)

    _interleave(front(0), front(1))
    _interleave(back(0), back(1))
    for p in range(npair):
        s_ref[p] = states[p]


def _rwkv(pin, tail, vec, mats, vfirst, vmix, layer, tb=ROW_BLOCK):
    S = pin.shape[0]
    W = RWKV_W
    lsel = lambda i: (layer, 0, 0)
    lsel1 = lambda i: (layer - 1, 0, 0)
    in_specs = [
        pl.BlockSpec((tb, W), lambda i: (i, P_WR // W)),
        pl.BlockSpec((tb, W), lambda i: (i, P_WK // W)),
        pl.BlockSpec((tb, W), lambda i: (i, P_WV // W)),
        pl.BlockSpec((tb, LANES), lambda i: (i, T_WL // LANES)),
        pl.BlockSpec((tb, LANES), lambda i: (i, T_AL // LANES)),
        pl.BlockSpec((tb, LANES), lambda i: (i, T_GL // LANES)),
    ]
    args = [pin] * 3 + [tail] * 3
    for name, width in (("mu_r", W), ("mu_k", W), ("mu_v", W), ("mu_wl", LANES), ("mu_al", LANES),
                        ("mu_gl", LANES), ("w0", W), ("a0", W)):
        in_specs.append(pl.BlockSpec((None, 1, width), lsel))
        args.append(vec[name])
    for name in ("w2", "a2", "g2"):
        in_specs.append(pl.BlockSpec((None, LANES, W), lsel))
        args.append(mats[name])
    for name in ("k_k", "k_a", "ln_w", "ln_b", "r_k"):
        in_specs.append(pl.BlockSpec((None, 1, W), lsel))
        args.append(vec[name])
    mix_v = vmix is not None
    if mix_v:
        v0, v1, v2 = vmix
        in_specs += [pl.BlockSpec((tb, W), lambda i: (i, 0)),
                     pl.BlockSpec((None, 1, W), lsel1),
                     pl.BlockSpec((None, W, LANES), lsel1),
                     pl.BlockSpec((None, LANES, W), lsel1)]
        args += [vfirst, v0, v1, v2]
    out_spec = pl.BlockSpec((tb, W), lambda i: (i, 0))
    out_specs = [out_spec] if mix_v else [out_spec, out_spec]
    out_shape = [jax.ShapeDtypeStruct((S, W), BF16)]
    if not mix_v:
        out_shape.append(jax.ShapeDtypeStruct((S, W), F32))
    outs = pl.pallas_call(
        functools.partial(_rwkv_body, mix_v=mix_v),
        grid=(S // tb,),
        in_specs=in_specs,
        out_specs=out_specs,
        out_shape=out_shape,
        scratch_shapes=[pltpu.VMEM((SUBLANES, W), F32)] * 3 + [pltpu.VMEM((SUBLANES, LANES), F32)] * 3
        + [pltpu.VMEM((W // LANES, LANES, LANES), F32)],
        compiler_params=_cparams(("arbitrary",)),
        name="rwkv",
    )(*args)
    return (outs[0], None) if mix_v else (outs[0], outs[1])


def _merge_body(oa_ref, ob_ref, oc_ref, od_ref, g0_ref, g1_ref, g2_ref, g3_ref, wb_ref, out_ref):
    o_refs = (oa_ref, ob_ref, oc_ref, od_ref)
    g_refs = (g0_ref, g1_ref, g2_ref, g3_ref)
    acc = None
    for n in range(N_BRANCH):
        contrib = _sigmoid(g_refs[n][...].astype(F32)) * _dot(o_refs[n][...], wb_ref[n])
        acc = contrib if acc is None else acc + contrib
    out_ref[...] = acc.astype(BF16)


def _merge(o_a, o_b, o_c, o_d, pin, w_branch, layer, tm=512):
    S = pin.shape[0]
    D = D_MODEL
    ospec = pl.BlockSpec((tm, BRANCH_W), lambda i: (i, 0))
    gspec = lambda n: pl.BlockSpec((tm, D), lambda i: (i, n))
    return pl.pallas_call(
        _merge_body,
        grid=(S // tm,),
        in_specs=[ospec, ospec, ospec, ospec, gspec(0), gspec(1), gspec(2), gspec(3),
                  pl.BlockSpec((None, N_BRANCH, BRANCH_W, D), lambda i: (layer, 0, 0, 0))],
        out_specs=pl.BlockSpec((tm, D), lambda i: (i, 0)),
        out_shape=jax.ShapeDtypeStruct((S, D), BF16),
        compiler_params=_cparams(("arbitrary",)),
        name="merge",
    )(o_a, o_b, o_c, o_d, pin, pin, pin, pin, w_branch)


def _outproj_body(m_ref, w_ref, x_ref, o_ref):
    o_ref[...] = x_ref[...] + _dot(m_ref[...], w_ref[...])


def _outproj(merged, w_out, x, layer, tm=512, tn=2048):
    S, D = x.shape
    return pl.pallas_call(
        _outproj_body,
        grid=(S // tm, D // tn),
        in_specs=[
            pl.BlockSpec((tm, D), lambda i, j: (i, 0)),
            pl.BlockSpec((None, D, tn), lambda i, j: (layer, 0, j)),
            pl.BlockSpec((tm, tn), lambda i, j: (i, j)),
        ],
        out_specs=pl.BlockSpec((tm, tn), lambda i, j: (i, j)),
        out_shape=jax.ShapeDtypeStruct((S, D), F32),
        compiler_params=_cparams(("arbitrary", "arbitrary")),
        name="outproj",
    )(merged, w_out, x)


def _ffn_body(x_ref, nw_ref, wg_ref, wu_ref, wd_ref, out_ref, h_ref):
    @pl.when(pl.program_id(1) == 0)
    def _():
        x = x_ref[...]
        ms = jnp.mean(x * x, axis=-1, keepdims=True)
        h_ref[...] = (x * lax.rsqrt(ms + RMS_EPS) * nw_ref[...]).astype(BF16)
        out_ref[...] = x

    h = h_ref[...]
    act = (_silu(_dot(h, wg_ref[...])) * _dot(h, wu_ref[...])).astype(BF16)
    out_ref[...] += _dot(act, wd_ref[...])


def _ffn(x, norm_w, w_gu, w_down, layer, tm=1024, tf=512):
    S, D = x.shape
    nf = D_FF // tf
    return pl.pallas_call(
        _ffn_body,
        grid=(S // tm, nf),
        in_specs=[
            pl.BlockSpec((tm, D), lambda i, f: (i, 0)),
            pl.BlockSpec((None, 1, D), lambda i, f: (layer, 0, 0)),
            pl.BlockSpec((None, D, tf), lambda i, f: (layer, 0, f)),
            pl.BlockSpec((None, D, tf), lambda i, f: (layer, 0, f + nf)),
            pl.BlockSpec((None, tf, D), lambda i, f: (layer, f, 0)),
        ],
        out_specs=pl.BlockSpec((tm, D), lambda i, f: (i, 0)),
        out_shape=jax.ShapeDtypeStruct((S, D), F32),
        scratch_shapes=[pltpu.VMEM((tm, D), BF16)],
        compiler_params=_cparams(("arbitrary", "arbitrary")),
        name="ffn",
    )(x, norm_w, w_gu, w_gu, w_down)


def _pad_last(a, width):
    pad = width - a.shape[-1]
    if pad == 0:
        return a
    return jnp.pad(a, [(0, 0)] * (a.ndim - 1) + [(0, pad)])


def _pad_axis(a, axis, width):
    pad = width - a.shape[axis]
    if pad == 0:
        return a
    cfg = [(0, 0)] * a.ndim
    cfg[axis] = (0, pad)
    return jnp.pad(a, cfg)


_MAIN_PIECES = (
    (_O_GATE, P_GATE, N_BRANCH * D_MODEL),
    (_O_RQ, P_RQ, 512), (_O_RK, P_RK, 512), (_O_RV, P_RV, 512), (_O_RG, P_RG, 512),
    (_O_SXBC, P_SXBC, SSM_XBC), (_O_SZ, P_SZ, 512),
    (_O_CQ, P_CQ, MLA_Q_RANK),
    (_O_RW, P_WR, RWKV_W), (_O_RW + RWKV_W, P_WK, RWKV_W), (_O_RW + 2 * RWKV_W, P_WV, RWKV_W),
    (_O_CKV, P_CKV, MLA_KV_RANK),
)
_TAIL_PIECES = (
    (_O_SDT, T_DT, SSM_HEADS),
    (_O_KROPE, T_KROPE, MLA_ROPE),
    (_O_RW + 3 * RWKV_W, T_WL, RWKV_W_LORA),
    (_O_RW + 3 * RWKV_W + RWKV_W_LORA, T_AL, RWKV_A_LORA),
    (_O_RW + 3 * RWKV_W + RWKV_W_LORA + RWKV_A_LORA, T_GL, RWKV_G_LORA),
)
PACK_ROWS = 512


def _pack_tables(pieces, total, tile):
    src_of, valid_of = [], []
    for t in range(total // tile):
        lo = t * tile
        src, valid = 0, 0
        for s0, d0, width in pieces:
            if d0 <= lo < d0 + width:
                src, valid = s0 + (lo - d0), min(tile, d0 + width - lo)
        assert src % SUBLANES == 0
        src_of.append(src // SUBLANES)
        valid_of.append(valid)
    return jnp.asarray(src_of, jnp.int32), jnp.asarray(valid_of, jnp.int32)


def _pack_body(src_ref, valid_ref, w_ref, o_ref):
    w = w_ref[0]
    rows = lax.broadcasted_iota(jnp.int32, w.shape, 0)
    keep = rows < valid_ref[pl.program_id(1)]
    o_ref[...] = jnp.where(keep, w, 0.0).astype(BF16)


def _pack_rows(w_t, pieces, total, tile, name):
    depth, _, d = w_t.shape
    src_of, valid_of = _pack_tables(pieces, total, tile)
    grid_spec = pltpu.PrefetchScalarGridSpec(
        num_scalar_prefetch=2,
        grid=(depth, total // tile),
        in_specs=[pl.BlockSpec((pl.Element(1), pl.Element(tile), pl.Element(d)),
                               lambda l, t, src, valid: (l, src[t] * SUBLANES, 0))],
        out_specs=pl.BlockSpec((None, tile, d), lambda l, t, src, valid: (l, t, 0)),
    )
    return pl.pallas_call(
        _pack_body,
        grid_spec=grid_spec,
        out_shape=jax.ShapeDtypeStruct((depth, total, d), BF16),
        compiler_params=_cparams(("arbitrary", "arbitrary")),
        name=name,
    )(src_of, valid_of, w_t)


def _pack_w_in(w_in):
    w_t = jnp.swapaxes(w_in, 1, 2)
    w_main_t = _pack_rows(w_t, _MAIN_PIECES, N_MAIN, PACK_ROWS, "pack_w_main")
    w_tail_t = _pack_rows(w_t, _TAIL_PIECES, N_TAIL, LANES, "pack_w_tail")
    return w_main_t, w_tail_t


def _rope_tables(positions, dim):
    inv = 1.0 / (ROPE_THETA ** (jnp.arange(0, dim, 2, dtype=F32) / dim))
    ang = positions.astype(F32)[:, None] * inv
    return jnp.cos(ang), jnp.sin(ang)


def kernel(x, positions, norm1_w, w_in, ret_gn_w, ssm_conv_w, ssm_conv_b, ssm_dt_bias, ssm_a_log, ssm_d, ssm_norm_w, mla_q_a_norm_w, mla_w_qb, mla_kv_a_norm_w, mla_w_kvb, mla_q_norm_w, mla_k_norm_w, rwkv_mu, rwkv_w0, rwkv_w2, rwkv_a0, rwkv_a2, rwkv_g2, rwkv_v0, rwkv_v1, rwkv_v2, rwkv_k_k, rwkv_k_a, rwkv_r_k, rwkv_ln_w, rwkv_ln_b, w_branch, w_out, norm2_w, ffn_w_gu, ffn_w_down):
    B, S, D = x.shape
    assert B == 1 and D == D_MODEL and S % 1024 == 0
    xs = x[0]
    pos = positions[0]

    c_r, s_r = _rope_tables(pos, RET_DK)
    cos_ret = jnp.concatenate([c_r, c_r], axis=-1)
    sin_ret = jnp.concatenate([-s_r, s_r], axis=-1)
    c_m, s_m = _rope_tables(pos, MLA_ROPE)
    cos_mla = _pad_last(jnp.concatenate([c_m, c_m], axis=-1), LANES)
    sin_mla = _pad_last(jnp.concatenate([-s_m, s_m], axis=-1), LANES)

    row = lambda a: a[:, None, :]
    w_main, w_tail = _pack_w_in(w_in)
    norm1 = row(norm1_w)
    norm2 = row(norm2_w)
    gn_w = row(ret_gn_w)
    conv_b = row(ssm_conv_b)
    d_skip = row(ssm_d)
    ssm_nw = row(ssm_norm_w)
    dt_bias = row(_pad_last(ssm_dt_bias, LANES))
    a_neg = row(_pad_last(-jnp.exp(ssm_a_log.astype(F32)), LANES))
    qaw = row(mla_q_a_norm_w)
    kvaw = row(mla_kv_a_norm_w)
    wqb = _pad_last(mla_w_qb.reshape(DEPTH, MLA_Q_RANK, MLA_HEADS, MLA_QK), MLA_PAD)
    wqb = wqb.reshape(DEPTH, MLA_Q_RANK, MLA_HEADS * MLA_PAD).astype(BF16)
    wkvb = mla_w_kvb.astype(BF16)
    qnw = row(_pad_last(mla_q_norm_w, MLA_PAD))
    knw_n = row(mla_k_norm_w[:, :MLA_NOPE])
    knw_r = row(_pad_last(mla_k_norm_w[:, MLA_NOPE:], LANES))
    W = RWKV_W
    mu = rwkv_mu
    vec = {
        "mu_r": row(mu[:, :W]), "mu_k": row(mu[:, W:2 * W]), "mu_v": row(mu[:, 2 * W:3 * W]),
        "mu_wl": row(_pad_last(mu[:, 3 * W:3 * W + RWKV_W_LORA], LANES)),
        "mu_al": row(_pad_last(mu[:, 3 * W + RWKV_W_LORA:3 * W + RWKV_W_LORA + RWKV_A_LORA], LANES)),
        "mu_gl": row(mu[:, 3 * W + RWKV_W_LORA + RWKV_A_LORA:]),
        "w0": row(rwkv_w0), "a0": row(rwkv_a0), "k_k": row(rwkv_k_k), "k_a": row(rwkv_k_a),
        "ln_w": row(rwkv_ln_w), "ln_b": row(rwkv_ln_b), "r_k": row(rwkv_r_k.reshape(DEPTH, W)),
    }
    mats = {"w2": _pad_axis(rwkv_w2, 1, LANES), "a2": _pad_axis(rwkv_a2, 1, LANES), "g2": rwkv_g2}
    v0 = row(rwkv_v0)
    v1 = _pad_last(rwkv_v1, LANES)
    v2 = _pad_axis(rwkv_v2, 1, LANES)
    wb = w_branch.astype(BF16)
    wo = w_out.astype(BF16)
    wgu = ffn_w_gu.astype(BF16)
    wdn = ffn_w_down.astype(BF16)

    v_first = None
    for l in range(DEPTH):
        pin, tail = _inproj(xs, norm1, w_main, w_tail, l)
        o_a = _retention(pin, cos_ret, sin_ret, gn_w, l)
        o_b = _ssd(pin, tail, dt_bias, a_neg, ssm_conv_w, conv_b, d_skip, ssm_nw, l)
        q, k, v = _mla_prep(pin, tail, qaw, wqb, kvaw, wkvb, qnw, knw_n, knw_r, cos_mla, sin_mla, l)
        o_c = _flash(q, k, v)
        vmix = None if l == 0 else (v0, v1, v2)
        o_d, vf = _rwkv(pin, tail, vec, mats, v_first, vmix, l)
        if l == 0:
            v_first = vf
        merged = _merge(o_a, o_b, o_c, o_d, pin, wb, l)
        xs = _outproj(merged, wo, xs, l)
        xs = _ffn(xs, norm2, wgu, wdn, l)
    return xs[None]
```

```python
import functools
import math

import numpy as np
import jax
import jax.numpy as jnp
from jax import lax
from jax.experimental import pallas as pl
from jax.experimental.pallas import tpu as pltpu

F32 = jnp.float32
BF16 = jnp.bfloat16

D_MODEL = 2048
DEPTH = 4
CHUNK = 64
N_BRANCH = 4
BRANCH_W = 512
RMS_EPS = 1e-6
GN_EPS = 1e-5
ROPE_THETA = 10000.0
RET_HEADS, RET_DK, RET_DV = 4, 128, 128
SSM_HEADS, SSM_HEADDIM, SSM_GROUPS, SSM_STATE, SSM_CONV = 8, 64, 2, 128, 4
SSM_XBC = SSM_HEADS * SSM_HEADDIM + 2 * SSM_GROUPS * SSM_STATE
MLA_HEADS, MLA_Q_RANK, MLA_KV_RANK, MLA_NOPE, MLA_ROPE, MLA_V = 4, 512, 256, 128, 64, 128
MLA_QK = MLA_NOPE + MLA_ROPE
RWKV_HEADS, RWKV_HEAD = 8, 64
RWKV_W = RWKV_HEADS * RWKV_HEAD
RWKV_W_LORA, RWKV_A_LORA, RWKV_V_LORA, RWKV_G_LORA = 64, 64, 32, 128
RWKV_GN_EPS = 64e-5
D_FF = 5632

LANES = 128
SUBLANES = 8
VMEM_LIMIT_BYTES = 56 * 1024 * 1024

_O_RQ, _O_RK, _O_RV, _O_RG = 0, 512, 1024, 1536
_O_SZ, _O_SXBC, _O_SDT = 2048, 2560, 3584
_O_CQ, _O_CKV, _O_KROPE = 3592, 4104, 4360
_O_RW = 4424
_O_GATE = 6216
P_GATE = 0
P_RQ, P_RK, P_RV, P_RG = 8192, 8704, 9216, 9728
P_SXBC, P_SZ = 10240, 11264
P_CQ = 11776
P_WR, P_WK, P_WV = 12288, 12800, 13312
P_CKV = 13824
N_MAIN = 14336
T_DT, T_KROPE, T_WL, T_AL, T_GL = 0, 128, 256, 384, 512
N_TAIL = 640

ROW_BLOCK = 256
RWKV_L = 64
LOG2E = 1.4426950408889634


def _cparams(sem):
    return pltpu.CompilerParams(dimension_semantics=sem, vmem_limit_bytes=VMEM_LIMIT_BYTES)


def _dot(a, b):
    return jnp.dot(a, b, preferred_element_type=F32)


def _dot_nt(a, b):
    return lax.dot_general(a, b, (((1,), (1,)), ((), ())), preferred_element_type=F32)


def _dot_tn(a, b):
    return lax.dot_general(a, b, (((0,), (0,)), ((), ())), preferred_element_type=F32)


def _split_bf16(x):
    hi = x.astype(BF16)
    lo = (x - hi.astype(F32)).astype(BF16)
    return hi, lo


def _mm3(a, b, dot=_dot):
    ah, al = _split_bf16(a)
    bh, bl = _split_bf16(b)
    return dot(ah, bh) + dot(ah, bl) + dot(al, bh)


def _mm1(a, b, dot=_dot):
    return dot(a.astype(BF16), b.astype(BF16))


def _mm_exact_lhs(a01, x, dot=_dot):
    xh, xl = _split_bf16(x)
    a = a01.astype(BF16)
    return dot(a, xh) + dot(a, xl)


def _seg_sum(x, ones_bd):
    return _dot(x.astype(BF16), ones_bd)


def _sigmoid(x):
    return 0.5 * jnp.tanh(0.5 * x) + 0.5


def _silu(x):
    return x * _sigmoid(x)


def _softplus(x):
    return jnp.maximum(x, 0.0) + jnp.log(1.0 + jnp.exp(-jnp.abs(x)))


def _shift_rows(x, carry8, s):
    xr = pltpu.roll(x, s, 0)
    pr = pltpu.roll(carry8, s, 0)
    row = lax.broadcasted_iota(jnp.int32, carry8.shape, 0)
    top = jnp.where(row < s, pr, xr[:SUBLANES])
    return jnp.concatenate([top, xr[SUBLANES:]], axis=0)


def _inproj_body(x_ref, nw_ref, w_ref, wt_ref, o_ref, ot_ref, h_ref):
    @pl.when(pl.program_id(1) == 0)
    def _():
        x = x_ref[...]
        ms = jnp.mean(x * x, axis=-1, keepdims=True)
        h = (x * lax.rsqrt(ms + RMS_EPS) * nw_ref[...]).astype(BF16)
        h_ref[...] = h
        ot_ref[...] = _dot_nt(h, wt_ref[...])

    o_ref[...] = _dot_nt(h_ref[...], w_ref[...]).astype(BF16)


def _inproj(x, norm_w, w_main, w_tail, layer, tm=1024, tn=1024):
    S, D = x.shape
    return pl.pallas_call(
        _inproj_body,
        grid=(S // tm, N_MAIN // tn),
        in_specs=[
            pl.BlockSpec((tm, D), lambda i, j: (i, 0)),
            pl.BlockSpec((None, 1, D), lambda i, j: (layer, 0, 0)),
            pl.BlockSpec((None, tn, D), lambda i, j: (layer, j, 0)),
            pl.BlockSpec((None, N_TAIL, D), lambda i, j: (layer, 0, 0)),
        ],
        out_specs=[pl.BlockSpec((tm, tn), lambda i, j: (i, j)),
                   pl.BlockSpec((tm, N_TAIL), lambda i, j: (i, 0))],
        out_shape=[jax.ShapeDtypeStruct((S, N_MAIN), BF16), jax.ShapeDtypeStruct((S, N_TAIL), F32)],
        scratch_shapes=[pltpu.VMEM((tm, D), BF16)],
        compiler_params=_cparams(("arbitrary", "arbitrary")),
        name="inproj",
    )(x, norm_w, w_main, w_tail)


def _ret_body(q_ref, k_ref, v_ref, g_ref, cos_ref, sin_ref, m_ref, qd_ref, kd_ref, gnw_ref,
              o_ref, s_ref, *, block_decay):
    @pl.when(pl.program_id(0) == 0)
    def _():
        s_ref[...] = jnp.zeros_like(s_ref)

    cos = cos_ref[...]
    sin = sin_ref[...]
    heads = range(RET_HEADS)
    lanes = [slice(h * RET_DK, (h + 1) * RET_DK) for h in heads]
    q, k, vb = [], [], []
    for h in heads:
        qh = q_ref[:, lanes[h]].astype(F32)
        kh = k_ref[:, lanes[h]].astype(F32)
        q.append(qh * cos + pltpu.roll(qh, RET_DK // 2, 1) * sin)
        k.append((kh * cos + pltpu.roll(kh, RET_DK // 2, 1) * sin) * (RET_DK ** -0.5))
        vb.append(v_ref[:, lanes[h]])
    sc = [(_dot_nt(q[h].astype(BF16), k[h].astype(BF16)) * m_ref[h]).astype(BF16) for h in heads]
    st = [s_ref[h] for h in heads]
    o = [_dot(sc[h], vb[h]) + _dot((q[h] * qd_ref[h]).astype(BF16), st[h].astype(BF16)) for h in heads]
    for h in heads:
        s_ref[h] = block_decay[h] * st[h] + _dot_tn((k[h] * kd_ref[h]).astype(BF16), vb[h])
    oc = [o[h] - jnp.mean(o[h], axis=-1, keepdims=True) for h in heads]
    var = [jnp.mean(oc[h] * oc[h], axis=-1, keepdims=True) for h in heads]
    for h in heads:
        on = oc[h] * lax.rsqrt(var[h] + GN_EPS) * gnw_ref[:, lanes[h]]
        o_ref[:, lanes[h]] = (_silu(g_ref[:, lanes[h]].astype(F32)) * on).astype(BF16)


def _ret_tables(tb):
    lg = np.log1p(-np.exp2(-5.0 - np.arange(RET_HEADS, dtype=np.float64)))
    pos = np.arange(tb, dtype=np.float64)
    dist = np.abs(pos[:, None] - pos[None, :])
    visible = (pos[None, :] // CHUNK) <= (pos[:, None] // CHUNK)
    mask = np.where(visible[None], np.exp(lg[:, None, None] * dist[None]), 0.0)
    qd = np.exp(lg[:, None] * (pos[None, :] + 1.0))[:, :, None]
    kd = np.exp(lg[:, None] * (tb - 1.0 - pos[None, :]))[:, :, None]
    bd = tuple(float(v) for v in np.exp(lg * tb))
    return (jnp.asarray(mask, F32), jnp.asarray(qd, F32), jnp.asarray(kd, F32), bd)


def _retention(pin, cos2, sin2, gn_w, layer, tb=ROW_BLOCK):
    S = pin.shape[0]
    mask, qd, kd, bd = _ret_tables(tb)
    w = RET_HEADS * RET_DK
    col = lambda off: (lambda i: (i, off // w))
    full3 = lambda i: (0, 0, 0)
    return pl.pallas_call(
        functools.partial(_ret_body, block_decay=bd),
        grid=(S // tb,),
        in_specs=[
            pl.BlockSpec((tb, w), col(P_RQ)),
            pl.BlockSpec((tb, w), col(P_RK)),
            pl.BlockSpec((tb, w), col(P_RV)),
            pl.BlockSpec((tb, w), col(P_RG)),
            pl.BlockSpec((tb, RET_DK), lambda i: (i, 0)),
            pl.BlockSpec((tb, RET_DK), lambda i: (i, 0)),
            pl.BlockSpec((RET_HEADS, tb, tb), full3),
            pl.BlockSpec((RET_HEADS, tb, 1), full3),
            pl.BlockSpec((RET_HEADS, tb, 1), full3),
            pl.BlockSpec((None, 1, w), lambda i: (layer, 0, 0)),
        ],
        out_specs=pl.BlockSpec((tb, w), lambda i: (i, 0)),
        out_shape=jax.ShapeDtypeStruct((S, w), BF16),
        scratch_shapes=[pltpu.VMEM((RET_HEADS, RET_DK, RET_DV), F32)],
        compiler_params=_cparams(("arbitrary",)),
        name="retention",
    )(pin, pin, pin, pin, cos2, sin2, mask, qd, kd, gn_w)


def _ssd_body(xbc_ref, z_ref, dt_ref, dtb_ref, aneg_ref, cw_ref, cb_ref, d_ref, nw_ref,
              o_ref, carry_ref, st_ref):
    tb = xbc_ref.shape[0]
    P, N = SSM_HEADDIM, SSM_STATE
    nx = SSM_HEADS * P

    @pl.when(pl.program_id(0) == 0)
    def _():
        carry_ref[...] = jnp.zeros_like(carry_ref)
        st_ref[...] = jnp.zeros_like(st_ref)

    x = xbc_ref[...].astype(F32)
    carry = carry_ref[...]
    acc = x * cw_ref[SSM_CONV - 1:SSM_CONV, :] + cb_ref[...]
    for s in range(1, SSM_CONV):
        acc = acc + _shift_rows(x, carry, s) * cw_ref[SSM_CONV - 1 - s:SSM_CONV - s, :]
    carry_ref[...] = x[tb - SUBLANES:, :]
    xbc = _silu(acc)

    dt = _softplus(dt_ref[...] + dtb_ref[...])
    adt = dt * aneg_ref[...]
    row = lax.broadcasted_iota(jnp.int32, (tb, tb), 0)
    colm = lax.broadcasted_iota(jnp.int32, (tb, tb), 1)
    causal = colm <= row
    tril = causal.astype(F32)
    a_col = jnp.dot(tril, adt, preferred_element_type=F32, precision=lax.Precision.HIGHEST)
    triu = (row <= colm).astype(F32)
    a_row = lax.dot_general(adt, triu, (((0,), (0,)), ((), ())),
                            preferred_element_type=F32, precision=lax.Precision.HIGHEST)
    a_end = a_col[tb - 1:tb, :]
    lane = lax.broadcasted_iota(jnp.int32, (tb, 2 * P), 1)
    first = lane < P

    groups = range(SSM_GROUPS)
    heads = range(SSM_HEADS)
    hpg = SSM_HEADS // SSM_GROUPS
    npair = SSM_HEADS // 2
    bm = [xbc[:, nx + g * N: nx + (g + 1) * N] for g in groups]
    cm = [xbc[:, nx + SSM_GROUPS * N + g * N: nx + SSM_GROUPS * N + (g + 1) * N] for g in groups]
    cb = [_dot_nt(cm[g].astype(BF16), bm[g].astype(BF16)) for g in groups]
    xp = [xbc[:, 2 * q * P:(2 * q + 2) * P] for q in range(npair)]
    xdt = [(xp[q] * jnp.where(first, dt[:, 2 * q:2 * q + 1], dt[:, 2 * q + 1:2 * q + 2])).astype(BF16)
           for q in range(npair)]
    st = [st_ref[q] for q in range(npair)]
    ac = [a_col[:, h:h + 1] for h in heads]
    dec = [jnp.exp(jnp.where(causal, ac[h] - a_row[h:h + 1, :], -jnp.inf)) for h in heads]
    wts = [(cb[h // hpg] * dec[h]).astype(BF16) for h in heads]
    y = [_dot(wts[h], xdt[h // 2]) for h in heads]
    cdec = [(cm[h // hpg] * jnp.exp(ac[h])).astype(BF16) for h in heads]
    y = [y[h] + _dot(cdec[h], st[h // 2].astype(BF16)) for h in heads]
    bdec = [(bm[h // hpg] * jnp.exp(a_end[:, h:h + 1] - ac[h])).astype(BF16) for h in heads]
    snew = [jnp.exp(a_end[:, h:h + 1]) * st[h // 2] + _dot_tn(bdec[h], xdt[h // 2]) for h in heads]
    lane_s = lax.broadcasted_iota(jnp.int32, (N, 2 * P), 1)
    for q in range(npair):
        st_ref[q] = jnp.where(lane_s < P, snew[2 * q], snew[2 * q + 1])
    ypair = [jnp.where(first, y[2 * q], y[2 * q + 1])
             + xp[q] * jnp.where(first, d_ref[:, 2 * q:2 * q + 1], d_ref[:, 2 * q + 1:2 * q + 2])
             for q in range(npair)]
    for g in groups:
        yg = jnp.concatenate(ypair[g * hpg // 2:(g + 1) * hpg // 2], axis=1)
        gsl = slice(g * hpg * P, (g + 1) * hpg * P)
        yg = yg * _silu(z_ref[:, gsl].astype(F32))
        ms = jnp.mean(yg * yg, axis=-1, keepdims=True)
        o_ref[:, gsl] = (yg * lax.rsqrt(ms + RMS_EPS) * nw_ref[:, gsl]).astype(BF16)


def _ssd(pin, tail, dt_bias, a_neg, conv_w, conv_b, d_skip, norm_w, layer, tb=ROW_BLOCK):
    S = pin.shape[0]
    nx = SSM_HEADS * SSM_HEADDIM
    H = SSM_HEADS
    lsel = lambda i: (layer, 0, 0)
    return pl.pallas_call(
        _ssd_body,
        grid=(S // tb,),
        in_specs=[
            pl.BlockSpec((tb, SSM_XBC), lambda i: (i, P_SXBC // SSM_XBC)),
            pl.BlockSpec((tb, nx), lambda i: (i, P_SZ // nx)),
            pl.BlockSpec((tb, LANES), lambda i: (i, T_DT // LANES)),
            pl.BlockSpec((None, 1, LANES), lsel),
            pl.BlockSpec((None, 1, LANES), lsel),
            pl.BlockSpec((None, SSM_CONV, SSM_XBC), lsel),
            pl.BlockSpec((None, 1, SSM_XBC), lsel),
            pl.BlockSpec((None, 1, H), lsel),
            pl.BlockSpec((None, 1, nx), lsel),
        ],
        out_specs=pl.BlockSpec((tb, nx), lambda i: (i, 0)),
        out_shape=jax.ShapeDtypeStruct((S, nx), BF16),
        scratch_shapes=[pltpu.VMEM((SUBLANES, SSM_XBC), F32),
                        pltpu.VMEM((H // 2, SSM_STATE, 2 * SSM_HEADDIM), F32)],
        compiler_params=_cparams(("arbitrary",)),
        name="ssd",
    )(pin, pin, tail, dt_bias, a_neg, conv_w, conv_b, d_skip, norm_w)


MLA_PAD = 256
MLA_VX = MLA_V + 16


def _rope_pad(x, cosp, sinp):
    half = MLA_ROPE // 2
    return x * cosp + (pltpu.roll(x, half, 1) + pltpu.roll(x, LANES - half, 1)) * sinp


def _mla_prep_body(cq_ref, ckv_ref, kr_ref, qaw_ref, wqb_ref, kvaw_ref, wkvb_ref, qnw_ref,
                   knw_n_ref, knw_r_ref, cos_ref, sin_ref, q_out, k_out, v_out):
    cosp = cos_ref[...]
    sinp = sin_ref[...]
    cq = cq_ref[...].astype(F32)
    ms = jnp.mean(cq * cq, axis=-1, keepdims=True)
    cqn = (cq * lax.rsqrt(ms + RMS_EPS) * qaw_ref[...]).astype(BF16)
    q = _dot(cqn, wqb_ref[...])
    ckv = ckv_ref[...].astype(F32)
    ms = jnp.mean(ckv * ckv, axis=-1, keepdims=True)
    ckvn = (ckv * lax.rsqrt(ms + RMS_EPS) * kvaw_ref[...]).astype(BF16)
    kv = _dot(ckvn, wkvb_ref[...])
    kr = kr_ref[...]
    ssr = jnp.sum(kr * kr, axis=-1, keepdims=True)
    scale = (MLA_QK ** -0.5) * LOG2E
    heads = range(MLA_HEADS)
    qh = [q[:, h * MLA_PAD:(h + 1) * MLA_PAD] for h in heads]
    kn = [kv[:, h * MLA_PAD: h * MLA_PAD + MLA_NOPE] for h in heads]
    qinv = [lax.rsqrt(jnp.sum(qh[h] * qh[h], axis=-1, keepdims=True) * (1.0 / MLA_QK) + RMS_EPS) for h in heads]
    kinv = [lax.rsqrt((jnp.sum(kn[h] * kn[h], axis=-1, keepdims=True) + ssr) * (1.0 / MLA_QK) + RMS_EPS)
            for h in heads]
    qh = [qh[h] * qinv[h] * qnw_ref[...] for h in heads]
    qr = [_rope_pad(qh[h][:, MLA_NOPE:], cosp, sinp) for h in heads]
    krh = [_rope_pad(kr * kinv[h] * knw_r_ref[...], cosp, sinp) for h in heads]
    ones = jnp.ones((MLA_VX - MLA_V, kv.shape[0]), F32)
    for h in heads:
        q_out[h] = (jnp.concatenate([qh[h][:, :MLA_NOPE], qr[h]], axis=1) * scale).T.astype(BF16)
    for h in heads:
        k_out[h] = jnp.concatenate([kn[h] * kinv[h] * knw_n_ref[...], krh[h]], axis=1).astype(BF16)
    for h in heads:
        vv = kv[:, h * MLA_PAD + MLA_NOPE:(h + 1) * MLA_PAD]
        v_out[h] = jnp.concatenate([vv.T, ones], axis=0).astype(BF16)


def _mla_prep(pin, tail, qaw, wqb, kvaw, wkvb, qnw, knw_n, knw_r, cosp, sinp, layer, tb=ROW_BLOCK):
    S = pin.shape[0]
    H = MLA_HEADS
    lsel = lambda i: (layer, 0, 0)
    return pl.pallas_call(
        _mla_prep_body,
        grid=(S // tb,),
        in_specs=[
            pl.BlockSpec((tb, MLA_Q_RANK), lambda i: (i, P_CQ // MLA_Q_RANK)),
            pl.BlockSpec((tb, MLA_KV_RANK), lambda i: (i, P_CKV // MLA_KV_RANK)),
            pl.BlockSpec((tb, LANES), lambda i: (i, T_KROPE // LANES)),
            pl.BlockSpec((None, 1, MLA_Q_RANK), lsel),
            pl.BlockSpec((None, MLA_Q_RANK, H * MLA_PAD), lsel),
            pl.BlockSpec((None, 1, MLA_KV_RANK), lsel),
            pl.BlockSpec((None, MLA_KV_RANK, H * MLA_PAD), lsel),
            pl.BlockSpec((None, 1, MLA_PAD), lsel),
            pl.BlockSpec((None, 1, MLA_NOPE), lsel),
            pl.BlockSpec((None, 1, LANES), lsel),
            pl.BlockSpec((tb, LANES), lambda i: (i, 0)),
            pl.BlockSpec((tb, LANES), lambda i: (i, 0)),
        ],
        out_specs=[
            pl.BlockSpec((H, MLA_PAD, tb), lambda i: (0, 0, i)),
            pl.BlockSpec((H, tb, MLA_PAD), lambda i: (0, i, 0)),
            pl.BlockSpec((H, MLA_VX, tb), lambda i: (0, 0, i)),
        ],
        out_shape=[
            jax.ShapeDtypeStruct((H, MLA_PAD, S), BF16),
            jax.ShapeDtypeStruct((H, S, MLA_PAD), BF16),
            jax.ShapeDtypeStruct((H, MLA_VX, S), BF16),
        ],
        compiler_params=_cparams(("arbitrary",)),
        name="mla_prep",
    )(pin, pin, tail, qaw, wqb, kvaw, wkvb, qnw, knw_n, knw_r, cosp, sinp)


def _flash_body(it_ref, jt_ref, qt_ref, k_ref, vt_ref, o_ref, m_ref, acc_ref):
    t = pl.program_id(0)
    i = it_ref[t]
    j = jt_ref[t]
    tq = qt_ref.shape[2]
    tk = k_ref.shape[1]
    full = (j + 1) * tk <= i * tq
    last = j == ((i + 1) * tq - 1) // tk
    H = MLA_HEADS

    @pl.when(j == 0)
    def _():
        m_ref[...] = jnp.full_like(m_ref, -jnp.inf)
        acc_ref[...] = jnp.zeros_like(acc_ref)

    def step(diagonal):
        if diagonal:
            kc = (j * tk + lax.broadcasted_iota(jnp.int32, (tk, tq), 0)) // CHUNK
            qc = (i * tq + lax.broadcasted_iota(jnp.int32, (tk, tq), 1)) // CHUNK
            visible = kc <= qc

        def logits(h):
            s = _dot(k_ref[h], qt_ref[h])
            return jnp.where(visible, s, -jnp.inf) if diagonal else s

        def softmax(h, s):
            m_prev = m_ref[h]
            m_new = jnp.maximum(m_prev, jnp.max(s, axis=0, keepdims=True))
            alpha = jnp.exp2(m_prev - m_new)
            m_ref[h] = m_new
            return alpha, jnp.exp2((s - m_new).astype(BF16))

        def accumulate(h, alpha, p):
            acc_ref[h] = alpha * acc_ref[h] + _dot(vt_ref[h], p)

        s_next = logits(0)
        pending = None
        for h in range(H):
            s_cur = s_next
            if h + 1 < H:
                s_next = logits(h + 1)
            if pending is not None:
                accumulate(*pending)
            pending = (h,) + softmax(h, s_cur)
        accumulate(*pending)

    @pl.when(full)
    def _():
        step(False)

    @pl.when(jnp.logical_not(full))
    def _():
        step(True)

    @pl.when(last)
    def _():
        for h in range(H):
            acc = acc_ref[h]
            o_ref[:, h * MLA_V:(h + 1) * MLA_V] = (acc[:MLA_V] / acc[MLA_V:MLA_V + 1]).T.astype(BF16)


def _flash(qt, k, vt, tq=1024, tk=1024):
    H, S, _ = k.shape
    pairs = [(i, j) for i in range(S // tq) for j in range(((i + 1) * tq - 1) // tk + 1)]
    it = jnp.asarray([p[0] for p in pairs], jnp.int32)
    jt = jnp.asarray([p[1] for p in pairs], jnp.int32)
    grid_spec = pltpu.PrefetchScalarGridSpec(
        num_scalar_prefetch=2,
        grid=(len(pairs),),
        in_specs=[
            pl.BlockSpec((H, MLA_PAD, tq), lambda t, it, jt: (0, 0, it[t])),
            pl.BlockSpec((H, tk, MLA_PAD), lambda t, it, jt: (0, jt[t], 0)),
            pl.BlockSpec((H, MLA_VX, tk), lambda t, it, jt: (0, 0, jt[t])),
        ],
        out_specs=pl.BlockSpec((tq, H * MLA_V), lambda t, it, jt: (it[t], 0)),
        scratch_shapes=[pltpu.VMEM((H, 1, tq), F32), pltpu.VMEM((H, MLA_VX, tq), F32)],
    )
    return pl.pallas_call(
        _flash_body,
        grid_spec=grid_spec,
        out_shape=jax.ShapeDtypeStruct((S, H * MLA_V), BF16),
        compiler_params=_cparams(("arbitrary",)),
        name="mla_flash",
    )(it, jt, qt, k, vt)


def _interleave(*streams):
    active = list(streams)
    while active:
        for g in list(active):
            try:
                next(g)
            except StopIteration:
                active.remove(g)


def _rwkv_body(*refs, mix_v):
    if mix_v:
        (r_ref, k_ref, v_ref, wl_ref, al_ref, gl_ref, mur, muk, muv, muwl, mual, mugl, w0, a0, w2, a2, g2,
         kk_ref, ka_ref, lnw_ref, lnb_ref, rk_ref, vf_ref, v0, v1, v2,
         o_ref, c_r, c_k, c_v, c_wl, c_al, c_gl, s_ref) = refs
    else:
        (r_ref, k_ref, v_ref, wl_ref, al_ref, gl_ref, mur, muk, muv, muwl, mual, mugl, w0, a0, w2, a2, g2,
         kk_ref, ka_ref, lnw_ref, lnb_ref, rk_ref,
         o_ref, vfirst_ref, c_r, c_k, c_v, c_wl, c_al, c_gl, s_ref) = refs
    tb = r_ref.shape[0]
    L = RWKV_L
    N = RWKV_HEAD
    L2 = 2 * L
    npair = RWKV_W // LANES
    half = tb // 2
    carries = (c_r, c_k, c_v, c_wl, c_al, c_gl)

    @pl.when(pl.program_id(0) == 0)
    def _():
        for c in carries:
            c[...] = jnp.zeros_like(c)
        s_ref[...] = jnp.zeros_like(s_ref)

    def mixed(x_ref, c_ref, mu_ref):
        x = x_ref[...].astype(F32)
        prev = _shift_rows(x, c_ref[...], 1)
        c_ref[...] = x[tb - SUBLANES:, :]
        return x + (prev - x) * mu_ref[...]

    r = mixed(r_ref, c_r, mur)
    k = mixed(k_ref, c_k, muk)
    v = mixed(v_ref, c_v, muv)
    wl = mixed(wl_ref, c_wl, muwl)
    al = mixed(al_ref, c_al, mual)
    gl = mixed(gl_ref, c_gl, mugl)
    if not mix_v:
        vfirst_ref[...] = v

    lane = lax.broadcasted_iota(jnp.int32, (L, LANES), 1)
    m0 = lane < N
    r2 = lax.broadcasted_iota(jnp.int32, (L2, L2), 0)
    c2 = lax.broadcasted_iota(jnp.int32, (L2, L2), 1)
    same = (r2 // L) == (c2 // L)
    strict = same & (c2 < r2)
    incl = same & (c2 <= r2)
    eye = (r2 == c2).astype(F32)
    ones_bd = ((r2 // N) == (c2 // N)).astype(BF16)
    rb = lax.broadcasted_iota(jnp.int32, (half, half), 0)
    cb = lax.broadcasted_iota(jnp.int32, (half, half), 1)
    tril_bd = ((rb // L) == (cb // L)) & (cb <= rb)

    def stack(x):
        return jnp.concatenate([jnp.where(m0, x, 0.0), jnp.where(m0, 0.0, x)], axis=0).astype(BF16)

    probs = [(c, p) for c in range(half // L) for p in range(npair)]
    rows_of = lambda c: slice(c * L, (c + 1) * L)
    lanes_of = lambda p: slice(p * LANES, (p + 1) * LANES)
    opnds = [{}, {}]
    gates = [None, None]
    states = [s_ref[p] for p in range(npair)]

    def front(hf):
        rs = slice(hf * half, (hf + 1) * half)
        w_raw = w0[...] + _mm3(jnp.tanh(wl[rs]), w2[...])
        lw = -jnp.exp(-_softplus(-w_raw) - 0.5)
        yield
        a_sig = _sigmoid(a0[...] + _mm3(al[rs], a2[...]))
        gates[hf] = _mm1(_sigmoid(gl[rs]), g2[...])
        yield
        vh = v[rs]
        if mix_v:
            lora = _mm1(_mm1(vh, v1[...]), v2[...])
            vh = vh + (vf_ref[rs, :] - vh) * _sigmoid(v0[...] + lora)
        yield
        kk = k[rs] * kk_ref[...]
        parts = []
        for p in range(npair):
            kp = kk[:, lanes_of(p)]
            n2 = _seg_sum(kp * kp, ones_bd)
            parts.append(kp / jnp.maximum(jnp.sqrt(n2), 1e-12))
            if p % 2:
                yield
        kk = jnp.concatenate(parts, axis=1)
        kh = k[rs] * (1.0 + (a_sig - 1.0) * ka_ref[...])
        ah = -kk
        bh = kk * a_sig
        rh = r[rs]
        cum_blk = _mm_exact_lhs(tril_bd, lw)
        yield
        for n, (c, p) in enumerate(probs):
            rows, sl = rows_of(c), lanes_of(p)
            cum = cum_blk[rows, sl]
            cum_end = cum[L - 1:L, :]
            rr = rh[rows, sl]
            kc = kh[rows, sl]
            vc = vh[rows, sl]
            e_neg = jnp.exp(-cum)
            e_end = jnp.exp(cum_end - cum)
            re = rr * jnp.exp(cum)
            xr_f = jnp.concatenate([jnp.where(m0, re, 0.0), jnp.where(m0, 0.0, re)], axis=0)
            opnds[hf][c, p] = dict(
                xr_f=xr_f, xa=stack(ah[rows, sl] * jnp.exp(cum - lw[rows, sl])), xr=xr_f.astype(BF16),
                xb=stack(bh[rows, sl] * e_neg), xk=stack(kc * e_neg), xbh=stack(bh[rows, sl] * e_end),
                xkh=stack(kc * e_end), vs=stack(vc), decay_end=jnp.exp(cum_end),
                rk=rr * kc * rk_ref[:, sl], v=vc)
            if n % 2:
                yield

    def back(hf):
        opnd = opnds[hf]
        amat = {}
        for cp in probs:
            o = opnd[cp]
            pmat = _dot_nt(jnp.concatenate([o["xa"], o["xr"]], axis=0),
                           jnp.concatenate([o["xb"], o["xk"]], axis=0))
            amat[cp] = dict(
                ab=jnp.where(strict, pmat[:L2, :L2], 0.0),
                ak=jnp.where(strict, pmat[:L2, L2:], 0.0).astype(BF16),
                rb=jnp.where(incl, pmat[L2:, :L2], 0.0).astype(BF16),
                rk=jnp.where(incl, pmat[L2:, L2:], 0.0).astype(BF16))
        yield
        tinv = {cp: eye + amat[cp]["ab"] for cp in probs}
        pw = {cp: amat[cp]["ab"] for cp in probs}
        for _ in range(int(math.log2(L)) - 1):
            for cp in probs:
                pwb = pw[cp].astype(BF16)
                pw[cp] = _dot(pwb, pwb)
            yield
            for cp in probs:
                tinv[cp] = tinv[cp] + _mm1(tinv[cp], pw[cp])
            yield
        w1 = {cp: _dot(amat[cp]["ak"], opnd[cp]["vs"]) for cp in probs}
        yield
        mub = {}
        for cp in probs:
            rhs = jnp.concatenate([opnd[cp]["xa"], w1[cp].astype(BF16)], axis=1)
            mub[cp] = _dot(tinv[cp].astype(BF16), rhs).astype(BF16)
        yield
        coef = {}
        for cp in probs:
            o = opnd[cp]
            ry = _dot(amat[cp]["rb"], mub[cp])
            gh = _dot_tn(o["xbh"], mub[cp])
            coef[cp] = dict(
                m_r=(o["xr_f"] + ry[:, :LANES]).astype(BF16),
                y1=ry[:, LANES:] + _dot(amat[cp]["rk"], o["vs"]),
                g=(eye * o["decay_end"] + gh[:, :LANES]).astype(BF16),
                h=gh[:, LANES:] + _dot_tn(o["xkh"], o["vs"]))
        yield
        ys = {}
        for (c, p) in probs:
            st = states[p].astype(BF16)
            cf = coef[c, p]
            yst = _dot(cf["m_r"], st) + cf["y1"]
            states[p] = _dot(cf["g"], st) + cf["h"]
            ys[c, p] = yst[:L] + yst[L:]
        yield
        yc = {cp: ys[cp] - _seg_sum(ys[cp], ones_bd) * (1.0 / N) for cp in probs}
        yield
        bonus = {cp: _seg_sum(opnd[cp]["rk"], ones_bd) * opnd[cp]["v"] for cp in probs}
        var = {cp: _seg_sum(yc[cp] * yc[cp], ones_bd) * (1.0 / N) for cp in probs}
        yield
        for (c, p) in probs:
            rows, sl = rows_of(c), lanes_of(p)
            out_rows = slice(hf * half + c * L, hf * half + (c + 1) * L)
            yn = yc[c, p] * lax.rsqrt(var[c, p] + RWKV_GN_EPS) * lnw_ref[:, sl] + lnb_ref[:, sl]
            o_ref[out_rows, sl] = ((yn + bonus[c, p]) * gates[hf][rows, sl]).astype(BF16)

    _interleave(front(0), front(1))
    _interleave(back(0), back(1))
    for p in range(npair):
        s_ref[p] = states[p]


def _rwkv(pin, tail, vec, mats, vfirst, vmix, layer, tb=ROW_BLOCK):
    S = pin.shape[0]
    W = RWKV_W
    lsel = lambda i: (layer, 0, 0)
    lsel1 = lambda i: (layer - 1, 0, 0)
    in_specs = [
        pl.BlockSpec((tb, W), lambda i: (i, P_WR // W)),
        pl.BlockSpec((tb, W), lambda i: (i, P_WK // W)),
        pl.BlockSpec((tb, W), lambda i: (i, P_WV // W)),
        pl.BlockSpec((tb, LANES), lambda i: (i, T_WL // LANES)),
        pl.BlockSpec((tb, LANES), lambda i: (i, T_AL // LANES)),
        pl.BlockSpec((tb, LANES), lambda i: (i, T_GL // LANES)),
    ]
    args = [pin] * 3 + [tail] * 3
    for name, width in (("mu_r", W), ("mu_k", W), ("mu_v", W), ("mu_wl", LANES), ("mu_al", LANES),
                        ("mu_gl", LANES), ("w0", W), ("a0", W)):
        in_specs.append(pl.BlockSpec((None, 1, width), lsel))
        args.append(vec[name])
    for name in ("w2", "a2", "g2"):
        in_specs.append(pl.BlockSpec((None, LANES, W), lsel))
        args.append(mats[name])
    for name in ("k_k", "k_a", "ln_w", "ln_b", "r_k"):
        in_specs.append(pl.BlockSpec((None, 1, W), lsel))
        args.append(vec[name])
    mix_v = vmix is not None
    if mix_v:
        v0, v1, v2 = vmix
        in_specs += [pl.BlockSpec((tb, W), lambda i: (i, 0)),
                     pl.BlockSpec((None, 1, W), lsel1),
                     pl.BlockSpec((None, W, LANES), lsel1),
                     pl.BlockSpec((None, LANES, W), lsel1)]
        args += [vfirst, v0, v1, v2]
    out_spec = pl.BlockSpec((tb, W), lambda i: (i, 0))
    out_specs = [out_spec] if mix_v else [out_spec, out_spec]
    out_shape = [jax.ShapeDtypeStruct((S, W), BF16)]
    if not mix_v:
        out_shape.append(jax.ShapeDtypeStruct((S, W), F32))
    outs = pl.pallas_call(
        functools.partial(_rwkv_body, mix_v=mix_v),
        grid=(S // tb,),
        in_specs=in_specs,
        out_specs=out_specs,
        out_shape=out_shape,
        scratch_shapes=[pltpu.VMEM((SUBLANES, W), F32)] * 3 + [pltpu.VMEM((SUBLANES, LANES), F32)] * 3
        + [pltpu.VMEM((W // LANES, LANES, LANES), F32)],
        compiler_params=_cparams(("arbitrary",)),
        name="rwkv",
    )(*args)
    return (outs[0], None) if mix_v else (outs[0], outs[1])


def _merge_body(oa_ref, ob_ref, oc_ref, od_ref, g0_ref, g1_ref, g2_ref, g3_ref, wb_ref, out_ref):
    o_refs = (oa_ref, ob_ref, oc_ref, od_ref)
    g_refs = (g0_ref, g1_ref, g2_ref, g3_ref)
    acc = None
    for n in range(N_BRANCH):
        contrib = _sigmoid(g_refs[n][...].astype(F32)) * _dot(o_refs[n][...], wb_ref[n])
        acc = contrib if acc is None else acc + contrib
    out_ref[...] = acc.astype(BF16)


def _merge(o_a, o_b, o_c, o_d, pin, w_branch, layer, tm=512):
    S = pin.shape[0]
    D = D_MODEL
    ospec = pl.BlockSpec((tm, BRANCH_W), lambda i: (i, 0))
    gspec = lambda n: pl.BlockSpec((tm, D), lambda i: (i, n))
    return pl.pallas_call(
        _merge_body,
        grid=(S // tm,),
        in_specs=[ospec, ospec, ospec, ospec, gspec(0), gspec(1), gspec(2), gspec(3),
                  pl.BlockSpec((None, N_BRANCH, BRANCH_W, D), lambda i: (layer, 0, 0, 0))],
        out_specs=pl.BlockSpec((tm, D), lambda i: (i, 0)),
        out_shape=jax.ShapeDtypeStruct((S, D), BF16),
        compiler_params=_cparams(("arbitrary",)),
        name="merge",
    )(o_a, o_b, o_c, o_d, pin, pin, pin, pin, w_branch)


def _outproj_body(m_ref, w_ref, x_ref, o_ref):
    o_ref[...] = x_ref[...] + _dot(m_ref[...], w_ref[...])


def _outproj(merged, w_out, x, layer, tm=512, tn=2048):
    S, D = x.shape
    return pl.pallas_call(
        _outproj_body,
        grid=(S // tm, D // tn),
        in_specs=[
            pl.BlockSpec((tm, D), lambda i, j: (i, 0)),
            pl.BlockSpec((None, D, tn), lambda i, j: (layer, 0, j)),
            pl.BlockSpec((tm, tn), lambda i, j: (i, j)),
        ],
        out_specs=pl.BlockSpec((tm, tn), lambda i, j: (i, j)),
        out_shape=jax.ShapeDtypeStruct((S, D), F32),
        compiler_params=_cparams(("arbitrary", "arbitrary")),
        name="outproj",
    )(merged, w_out, x)


def _ffn_body(x_ref, nw_ref, wg_ref, wu_ref, wd_ref, out_ref, h_ref):
    @pl.when(pl.program_id(1) == 0)
    def _():
        x = x_ref[...]
        ms = jnp.mean(x * x, axis=-1, keepdims=True)
        h_ref[...] = (x * lax.rsqrt(ms + RMS_EPS) * nw_ref[...]).astype(BF16)
        out_ref[...] = x

    h = h_ref[...]
    act = (_silu(_dot(h, wg_ref[...])) * _dot(h, wu_ref[...])).astype(BF16)
    out_ref[...] += _dot(act, wd_ref[...])


def _ffn(x, norm_w, w_gu, w_down, layer, tm=1024, tf=512):
    S, D = x.shape
    nf = D_FF // tf
    return pl.pallas_call(
        _ffn_body,
        grid=(S // tm, nf),
        in_specs=[
            pl.BlockSpec((tm, D), lambda i, f: (i, 0)),
            pl.BlockSpec((None, 1, D), lambda i, f: (layer, 0, 0)),
            pl.BlockSpec((None, D, tf), lambda i, f: (layer, 0, f)),
            pl.BlockSpec((None, D, tf), lambda i, f: (layer, 0, f + nf)),
            pl.BlockSpec((None, tf, D), lambda i, f: (layer, f, 0)),
        ],
        out_specs=pl.BlockSpec((tm, D), lambda i, f: (i, 0)),
        out_shape=jax.ShapeDtypeStruct((S, D), F32),
        scratch_shapes=[pltpu.VMEM((tm, D), BF16)],
        compiler_params=_cparams(("arbitrary", "arbitrary")),
        name="ffn",
    )(x, norm_w, w_gu, w_gu, w_down)


def _pad_last(a, width):
    pad = width - a.shape[-1]
    if pad == 0:
        return a
    return jnp.pad(a, [(0, 0)] * (a.ndim - 1) + [(0, pad)])


def _pad_axis(a, axis, width):
    pad = width - a.shape[axis]
    if pad == 0:
        return a
    cfg = [(0, 0)] * a.ndim
    cfg[axis] = (0, pad)
    return jnp.pad(a, cfg)


_MAIN_PIECES = (
    (_O_GATE, P_GATE, N_BRANCH * D_MODEL),
    (_O_RQ, P_RQ, 512), (_O_RK, P_RK, 512), (_O_RV, P_RV, 512), (_O_RG, P_RG, 512),
    (_O_SXBC, P_SXBC, SSM_XBC), (_O_SZ, P_SZ, 512),
    (_O_CQ, P_CQ, MLA_Q_RANK),
    (_O_RW, P_WR, RWKV_W), (_O_RW + RWKV_W, P_WK, RWKV_W), (_O_RW + 2 * RWKV_W, P_WV, RWKV_W),
    (_O_CKV, P_CKV, MLA_KV_RANK),
)
_TAIL_PIECES = (
    (_O_SDT, T_DT, SSM_HEADS),
    (_O_KROPE, T_KROPE, MLA_ROPE),
    (_O_RW + 3 * RWKV_W, T_WL, RWKV_W_LORA),
    (_O_RW + 3 * RWKV_W + RWKV_W_LORA, T_AL, RWKV_A_LORA),
    (_O_RW + 3 * RWKV_W + RWKV_W_LORA + RWKV_A_LORA, T_GL, RWKV_G_LORA),
)
PACK_ROWS = 512


def _pack_tables(pieces, total, tile):
    src_of, valid_of = [], []
    for t in range(total // tile):
        lo = t * tile
        src, valid = 0, 0
        for s0, d0, width in pieces:
            if d0 <= lo < d0 + width:
                src, valid = s0 + (lo - d0), min(tile, d0 + width - lo)
        assert src % SUBLANES == 0
        src_of.append(src // SUBLANES)
        valid_of.append(valid)
    return jnp.asarray(src_of, jnp.int32), jnp.asarray(valid_of, jnp.int32)


def _pack_body(src_ref, valid_ref, w_ref, o_ref):
    w = w_ref[0]
    rows = lax.broadcasted_iota(jnp.int32, w.shape, 0)
    keep = rows < valid_ref[pl.program_id(1)]
    o_ref[...] = jnp.where(keep, w, 0.0).astype(BF16)


def _pack_rows(w_t, pieces, total, tile, name):
    depth, _, d = w_t.shape
    src_of, valid_of = _pack_tables(pieces, total, tile)
    grid_spec = pltpu.PrefetchScalarGridSpec(
        num_scalar_prefetch=2,
        grid=(depth, total // tile),
        in_specs=[pl.BlockSpec((pl.Element(1), pl.Element(tile), pl.Element(d)),
                               lambda l, t, src, valid: (l, src[t] * SUBLANES, 0))],
        out_specs=pl.BlockSpec((None, tile, d), lambda l, t, src, valid: (l, t, 0)),
    )
    return pl.pallas_call(
        _pack_body,
        grid_spec=grid_spec,
        out_shape=jax.ShapeDtypeStruct((depth, total, d), BF16),
        compiler_params=_cparams(("arbitrary", "arbitrary")),
        name=name,
    )(src_of, valid_of, w_t)


def _pack_w_in(w_in):
    w_t = jnp.swapaxes(w_in, 1, 2)
    w_main_t = _pack_rows(w_t, _MAIN_PIECES, N_MAIN, PACK_ROWS, "pack_w_main")
    w_tail_t = _pack_rows(w_t, _TAIL_PIECES, N_TAIL, LANES, "pack_w_tail")
    return w_main_t, w_tail_t


def _rope_tables(positions, dim):
    inv = 1.0 / (ROPE_THETA ** (jnp.arange(0, dim, 2, dtype=F32) / dim))
    ang = positions.astype(F32)[:, None] * inv
    return jnp.cos(ang), jnp.sin(ang)


def kernel(x, positions, norm1_w, w_in, ret_gn_w, ssm_conv_w, ssm_conv_b, ssm_dt_bias, ssm_a_log, ssm_d, ssm_norm_w, mla_q_a_norm_w, mla_w_qb, mla_kv_a_norm_w, mla_w_kvb, mla_q_norm_w, mla_k_norm_w, rwkv_mu, rwkv_w0, rwkv_w2, rwkv_a0, rwkv_a2, rwkv_g2, rwkv_v0, rwkv_v1, rwkv_v2, rwkv_k_k, rwkv_k_a, rwkv_r_k, rwkv_ln_w, rwkv_ln_b, w_branch, w_out, norm2_w, ffn_w_gu, ffn_w_down):
    B, S, D = x.shape
    assert B == 1 and D == D_MODEL and S % 1024 == 0
    xs = x[0]
    pos = positions[0]

    c_r, s_r = _rope_tables(pos, RET_DK)
    cos_ret = jnp.concatenate([c_r, c_r], axis=-1)
    sin_ret = jnp.concatenate([-s_r, s_r], axis=-1)
    c_m, s_m = _rope_tables(pos, MLA_ROPE)
    cos_mla = _pad_last(jnp.concatenate([c_m, c_m], axis=-1), LANES)
    sin_mla = _pad_last(jnp.concatenate([-s_m, s_m], axis=-1), LANES)

    row = lambda a: a[:, None, :]
    w_main, w_tail = _pack_w_in(w_in)
    norm1 = row(norm1_w)
    norm2 = row(norm2_w)
    gn_w = row(ret_gn_w)
    conv_b = row(ssm_conv_b)
    d_skip = row(ssm_d)
    ssm_nw = row(ssm_norm_w)
    dt_bias = row(_pad_last(ssm_dt_bias, LANES))
    a_neg = row(_pad_last(-jnp.exp(ssm_a_log.astype(F32)), LANES))
    qaw = row(mla_q_a_norm_w)
    kvaw = row(mla_kv_a_norm_w)
    wqb = _pad_last(mla_w_qb.reshape(DEPTH, MLA_Q_RANK, MLA_HEADS, MLA_QK), MLA_PAD)
    wqb = wqb.reshape(DEPTH, MLA_Q_RANK, MLA_HEADS * MLA_PAD).astype(BF16)
    wkvb = mla_w_kvb.astype(BF16)
    qnw = row(_pad_last(mla_q_norm_w, MLA_PAD))
    knw_n = row(mla_k_norm_w[:, :MLA_NOPE])
    knw_r = row(_pad_last(mla_k_norm_w[:, MLA_NOPE:], LANES))
    W = RWKV_W
    mu = rwkv_mu
    vec = {
        "mu_r": row(mu[:, :W]), "mu_k": row(mu[:, W:2 * W]), "mu_v": row(mu[:, 2 * W:3 * W]),
        "mu_wl": row(_pad_last(mu[:, 3 * W:3 * W + RWKV_W_LORA], LANES)),
        "mu_al": row(_pad_last(mu[:, 3 * W + RWKV_W_LORA:3 * W + RWKV_W_LORA + RWKV_A_LORA], LANES)),
        "mu_gl": row(mu[:, 3 * W + RWKV_W_LORA + RWKV_A_LORA:]),
        "w0": row(rwkv_w0), "a0": row(rwkv_a0), "k_k": row(rwkv_k_k), "k_a": row(rwkv_k_a),
        "ln_w": row(rwkv_ln_w), "ln_b": row(rwkv_ln_b), "r_k": row(rwkv_r_k.reshape(DEPTH, W)),
    }
    mats = {"w2": _pad_axis(rwkv_w2, 1, LANES), "a2": _pad_axis(rwkv_a2, 1, LANES), "g2": rwkv_g2}
    v0 = row(rwkv_v0)
    v1 = _pad_last(rwkv_v1, LANES)
    v2 = _pad_axis(rwkv_v2, 1, LANES)
    wb = w_branch.astype(BF16)
    wo = w_out.astype(BF16)
    wgu = ffn_w_gu.astype(BF16)
    wdn = ffn_w_down.astype(BF16)

    v_first = None
    for l in range(DEPTH):
        pin, tail = _inproj(xs, norm1, w_main, w_tail, l)
        o_a = _retention(pin, cos_ret, sin_ret, gn_w, l)
        o_b = _ssd(pin, tail, dt_bias, a_neg, ssm_conv_w, conv_b, d_skip, ssm_nw, l)
        q, k, v = _mla_prep(pin, tail, qaw, wqb, kvaw, wkvb, qnw, knw_n, knw_r, cos_mla, sin_mla, l)
        o_c = _flash(q, k, v)
        vmix = None if l == 0 else (v0, v1, v2)
        o_d, vf = _rwkv(pin, tail, vec, mats, v_first, vmix, l)
        if l == 0:
            v_first = vf
        merged = _merge(o_a, o_b, o_c, o_d, pin, wb, l)
        xs = _outproj(merged, wo, xs, l)
        xs = _ffn(xs, norm2, wgu, wdn, l)
    return xs[None]
```

```python
import functools
import math

import numpy as np
import jax
import jax.numpy as jnp
from jax import lax
from jax.experimental import pallas as pl
from jax.experimental.pallas import tpu as pltpu

F32 = jnp.float32
BF16 = jnp.bfloat16

D_MODEL = 2048
DEPTH = 4
CHUNK = 64
N_BRANCH = 4
BRANCH_W = 512
RMS_EPS = 1e-6
GN_EPS = 1e-5
ROPE_THETA = 10000.0
RET_HEADS, RET_DK, RET_DV = 4, 128, 128
SSM_HEADS, SSM_HEADDIM, SSM_GROUPS, SSM_STATE, SSM_CONV = 8, 64, 2, 128, 4
SSM_XBC = SSM_HEADS * SSM_HEADDIM + 2 * SSM_GROUPS * SSM_STATE
MLA_HEADS, MLA_Q_RANK, MLA_KV_RANK, MLA_NOPE, MLA_ROPE, MLA_V = 4, 512, 256, 128, 64, 128
MLA_QK = MLA_NOPE + MLA_ROPE
RWKV_HEADS, RWKV_HEAD = 8, 64
RWKV_W = RWKV_HEADS * RWKV_HEAD
RWKV_W_LORA, RWKV_A_LORA, RWKV_V_LORA, RWKV_G_LORA = 64, 64, 32, 128
RWKV_GN_EPS = 64e-5
D_FF = 5632

LANES = 128
SUBLANES = 8
VMEM_LIMIT_BYTES = 56 * 1024 * 1024

_O_RQ, _O_RK, _O_RV, _O_RG = 0, 512, 1024, 1536
_O_SZ, _O_SXBC, _O_SDT = 2048, 2560, 3584
_O_CQ, _O_CKV, _O_KROPE = 3592, 4104, 4360
_O_RW = 4424
_O_GATE = 6216
P_GATE = 0
P_RQ, P_RK, P_RV, P_RG = 8192, 8704, 9216, 9728
P_SXBC, P_SZ = 10240, 11264
P_CQ = 11776
P_WR, P_WK, P_WV = 12288, 12800, 13312
P_CKV = 13824
N_MAIN = 14336
T_DT, T_KROPE, T_WL, T_AL, T_GL = 0, 128, 256, 384, 512
N_TAIL = 640

ROW_BLOCK = 256
RWKV_L = 64
LOG2E = 1.4426950408889634


def _cparams(sem):
    return pltpu.CompilerParams(dimension_semantics=sem, vmem_limit_bytes=VMEM_LIMIT_BYTES)


def _dot(a, b):
    return jnp.dot(a, b, preferred_element_type=F32)


def _dot_nt(a, b):
    return lax.dot_general(a, b, (((1,), (1,)), ((), ())), preferred_element_type=F32)


def _dot_tn(a, b):
    return lax.dot_general(a, b, (((0,), (0,)), ((), ())), preferred_element_type=F32)


def _split_bf16(x):
    hi = x.astype(BF16)
    lo = (x - hi.astype(F32)).astype(BF16)
    return hi, lo


def _mm3(a, b, dot=_dot):
    ah, al = _split_bf16(a)
    bh, bl = _split_bf16(b)
    return dot(ah, bh) + dot(ah, bl) + dot(al, bh)


def _mm1(a, b, dot=_dot):
    return dot(a.astype(BF16), b.astype(BF16))


def _mm_exact_lhs(a01, x, dot=_dot):
    xh, xl = _split_bf16(x)
    a = a01.astype(BF16)
    return dot(a, xh) + dot(a, xl)


def _seg_sum(x, ones_bd):
    return _dot(x.astype(BF16), ones_bd)


def _sigmoid(x):
    return 0.5 * jnp.tanh(0.5 * x) + 0.5


def _silu(x):
    return x * _sigmoid(x)


def _softplus(x):
    return jnp.maximum(x, 0.0) + jnp.log(1.0 + jnp.exp(-jnp.abs(x)))


def _shift_rows(x, carry8, s):
    xr = pltpu.roll(x, s, 0)
    pr = pltpu.roll(carry8, s, 0)
    row = lax.broadcasted_iota(jnp.int32, carry8.shape, 0)
    top = jnp.where(row < s, pr, xr[:SUBLANES])
    return jnp.concatenate([top, xr[SUBLANES:]], axis=0)


def _inproj_body(x_ref, nw_ref, w_ref, wt_ref, o_ref, ot_ref, h_ref):
    @pl.when(pl.program_id(1) == 0)
    def _():
        x = x_ref[...]
        ms = jnp.mean(x * x, axis=-1, keepdims=True)
        h = (x * lax.rsqrt(ms + RMS_EPS) * nw_ref[...]).astype(BF16)
        h_ref[...] = h
        ot_ref[...] = _dot_nt(h, wt_ref[...])

    o_ref[...] = _dot_nt(h_ref[...], w_ref[...]).astype(BF16)


def _inproj(x, norm_w, w_main, w_tail, layer, tm=1024, tn=1024):
    S, D = x.shape
    return pl.pallas_call(
        _inproj_body,
        grid=(S // tm, N_MAIN // tn),
        in_specs=[
            pl.BlockSpec((tm, D), lambda i, j: (i, 0)),
            pl.BlockSpec((None, 1, D), lambda i, j: (layer, 0, 0)),
            pl.BlockSpec((None, tn, D), lambda i, j: (layer, j, 0)),
            pl.BlockSpec((None, N_TAIL, D), lambda i, j: (layer, 0, 0)),
        ],
        out_specs=[pl.BlockSpec((tm, tn), lambda i, j: (i, j)),
                   pl.BlockSpec((tm, N_TAIL), lambda i, j: (i, 0))],
        out_shape=[jax.ShapeDtypeStruct((S, N_MAIN), BF16), jax.ShapeDtypeStruct((S, N_TAIL), F32)],
        scratch_shapes=[pltpu.VMEM((tm, D), BF16)],
        compiler_params=_cparams(("arbitrary", "arbitrary")),
        name="inproj",
    )(x, norm_w, w_main, w_tail)


def _roundrobin(*streams):
    active = list(streams)
    while active:
        for g in list(active):
            try:
                next(g)
            except StopIteration:
                active.remove(g)
            else:
                yield


def _chain(*streams):
    for g in streams:
        yield from g


def _interleave(*streams):
    for _ in _roundrobin(*streams):
        pass


def _ret_stream(q_ref, k_ref, v_ref, g_ref, cos_ref, sin_ref, m_ref, qd_ref, kd_ref, gnw_ref,
                o_ref, s_ref, *, block_decay):
    @pl.when(pl.program_id(0) == 0)
    def _():
        s_ref[...] = jnp.zeros_like(s_ref)

    cos = cos_ref[...]
    sin = sin_ref[...]
    heads = range(RET_HEADS)
    lanes = [slice(h * RET_DK, (h + 1) * RET_DK) for h in heads]
    q, k, vb = [], [], []
    for h in heads:
        qh = q_ref[:, lanes[h]].astype(F32)
        kh = k_ref[:, lanes[h]].astype(F32)
        q.append(qh * cos + pltpu.roll(qh, RET_DK // 2, 1) * sin)
        k.append((kh * cos + pltpu.roll(kh, RET_DK // 2, 1) * sin) * (RET_DK ** -0.5))
        vb.append(v_ref[:, lanes[h]])
        if h % 2:
            yield
    sc = [(_dot_nt(q[h].astype(BF16), k[h].astype(BF16)) * m_ref[h]).astype(BF16) for h in heads]
    yield
    st = [s_ref[h] for h in heads]
    o = [_dot(sc[h], vb[h]) + _dot((q[h] * qd_ref[h]).astype(BF16), st[h].astype(BF16)) for h in heads]
    yield
    for h in heads:
        s_ref[h] = block_decay[h] * st[h] + _dot_tn((k[h] * kd_ref[h]).astype(BF16), vb[h])
    yield
    oc = [o[h] - jnp.mean(o[h], axis=-1, keepdims=True) for h in heads]
    var = [jnp.mean(oc[h] * oc[h], axis=-1, keepdims=True) for h in heads]
    yield
    for h in heads:
        on = oc[h] * lax.rsqrt(var[h] + GN_EPS) * gnw_ref[:, lanes[h]]
        o_ref[:, lanes[h]] = (_silu(g_ref[:, lanes[h]].astype(F32)) * on).astype(BF16)
        if h % 2:
            yield


def _ret_tables(tb):
    lg = np.log1p(-np.exp2(-5.0 - np.arange(RET_HEADS, dtype=np.float64)))
    pos = np.arange(tb, dtype=np.float64)
    dist = np.abs(pos[:, None] - pos[None, :])
    visible = (pos[None, :] // CHUNK) <= (pos[:, None] // CHUNK)
    mask = np.where(visible[None], np.exp(lg[:, None, None] * dist[None]), 0.0)
    qd = np.exp(lg[:, None] * (pos[None, :] + 1.0))[:, :, None]
    kd = np.exp(lg[:, None] * (tb - 1.0 - pos[None, :]))[:, :, None]
    bd = tuple(float(v) for v in np.exp(lg * tb))
    return (jnp.asarray(mask, F32), jnp.asarray(qd, F32), jnp.asarray(kd, F32), bd)


def _ret_specs(pin, cos2, sin2, gn_w, layer, tb):
    mask, qd, kd, bd = _ret_tables(tb)
    w = RET_HEADS * RET_DK
    col = lambda off: (lambda i: (i, off // w))
    full3 = lambda i: (0, 0, 0)
    in_specs = [
        pl.BlockSpec((tb, w), col(P_RQ)),
        pl.BlockSpec((tb, w), col(P_RK)),
        pl.BlockSpec((tb, w), col(P_RV)),
        pl.BlockSpec((tb, w), col(P_RG)),
        pl.BlockSpec((tb, RET_DK), lambda i: (i, 0)),
        pl.BlockSpec((tb, RET_DK), lambda i: (i, 0)),
        pl.BlockSpec((RET_HEADS, tb, tb), full3),
        pl.BlockSpec((RET_HEADS, tb, 1), full3),
        pl.BlockSpec((RET_HEADS, tb, 1), full3),
        pl.BlockSpec((None, 1, w), lambda i: (layer, 0, 0)),
    ]
    args = [pin, pin, pin, pin, cos2, sin2, mask, qd, kd, gn_w]
    out_specs = [pl.BlockSpec((tb, w), lambda i: (i, 0))]
    out_shape = [jax.ShapeDtypeStruct((pin.shape[0], w), BF16)]
    scratch = [pltpu.VMEM((RET_HEADS, RET_DK, RET_DV), F32)]
    return in_specs, args, out_specs, out_shape, scratch, bd


def _ssd_stream(xbc_ref, z_ref, dt_ref, dtb_ref, aneg_ref, cw_ref, cb_ref, d_ref, nw_ref,
                o_ref, carry_ref, st_ref):
    tb = xbc_ref.shape[0]
    P, N = SSM_HEADDIM, SSM_STATE
    nx = SSM_HEADS * P

    @pl.when(pl.program_id(0) == 0)
    def _():
        carry_ref[...] = jnp.zeros_like(carry_ref)
        st_ref[...] = jnp.zeros_like(st_ref)

    x = xbc_ref[...].astype(F32)
    carry = carry_ref[...]
    acc = x * cw_ref[SSM_CONV - 1:SSM_CONV, :] + cb_ref[...]
    for s in range(1, SSM_CONV):
        acc = acc + _shift_rows(x, carry, s) * cw_ref[SSM_CONV - 1 - s:SSM_CONV - s, :]
        yield
    carry_ref[...] = x[tb - SUBLANES:, :]
    xbc = _silu(acc)
    yield

    dt = _softplus(dt_ref[...] + dtb_ref[...])
    adt = dt * aneg_ref[...]
    row = lax.broadcasted_iota(jnp.int32, (tb, tb), 0)
    colm = lax.broadcasted_iota(jnp.int32, (tb, tb), 1)
    causal = colm <= row
    tril = causal.astype(F32)
    a_col = jnp.dot(tril, adt, preferred_element_type=F32, precision=lax.Precision.HIGHEST)
    triu = (row <= colm).astype(F32)
    a_row = lax.dot_general(adt, triu, (((0,), (0,)), ((), ())),
                            preferred_element_type=F32, precision=lax.Precision.HIGHEST)
    a_end = a_col[tb - 1:tb, :]
    yield
    lane = lax.broadcasted_iota(jnp.int32, (tb, 2 * P), 1)
    first = lane < P

    groups = range(SSM_GROUPS)
    heads = range(SSM_HEADS)
    hpg = SSM_HEADS // SSM_GROUPS
    npair = SSM_HEADS // 2
    bm = [xbc[:, nx + g * N: nx + (g + 1) * N] for g in groups]
    cm = [xbc[:, nx + SSM_GROUPS * N + g * N: nx + SSM_GROUPS * N + (g + 1) * N] for g in groups]
    cb = [_dot_nt(cm[g].astype(BF16), bm[g].astype(BF16)) for g in groups]
    xp = [xbc[:, 2 * q * P:(2 * q + 2) * P] for q in range(npair)]
    xdt = [(xp[q] * jnp.where(first, dt[:, 2 * q:2 * q + 1], dt[:, 2 * q + 1:2 * q + 2])).astype(BF16)
           for q in range(npair)]
    st = [st_ref[q] for q in range(npair)]
    ac = [a_col[:, h:h + 1] for h in heads]
    yield
    wts = []
    for h in heads:
        dec = jnp.exp(jnp.where(causal, ac[h] - a_row[h:h + 1, :], -jnp.inf))
        wts.append((cb[h // hpg] * dec).astype(BF16))
        if h % 2:
            yield
    y = [_dot(wts[h], xdt[h // 2]) for h in heads]
    yield
    cdec = [(cm[h // hpg] * jnp.exp(ac[h])).astype(BF16) for h in heads]
    y = [y[h] + _dot(cdec[h], st[h // 2].astype(BF16)) for h in heads]
    yield
    bdec = [(bm[h // hpg] * jnp.exp(a_end[:, h:h + 1] - ac[h])).astype(BF16) for h in heads]
    snew = [jnp.exp(a_end[:, h:h + 1]) * st[h // 2] + _dot_tn(bdec[h], xdt[h // 2]) for h in heads]
    yield
    lane_s = lax.broadcasted_iota(jnp.int32, (N, 2 * P), 1)
    for q in range(npair):
        st_ref[q] = jnp.where(lane_s < P, snew[2 * q], snew[2 * q + 1])
    ypair = [jnp.where(first, y[2 * q], y[2 * q + 1])
             + xp[q] * jnp.where(first, d_ref[:, 2 * q:2 * q + 1], d_ref[:, 2 * q + 1:2 * q + 2])
             for q in range(npair)]
    yield
    for g in groups:
        yg = jnp.concatenate(ypair[g * hpg // 2:(g + 1) * hpg // 2], axis=1)
        gsl = slice(g * hpg * P, (g + 1) * hpg * P)
        yg = yg * _silu(z_ref[:, gsl].astype(F32))
        ms = jnp.mean(yg * yg, axis=-1, keepdims=True)
        o_ref[:, gsl] = (yg * lax.rsqrt(ms + RMS_EPS) * nw_ref[:, gsl]).astype(BF16)
        yield


def _ssd_specs(pin, tail, dt_bias, a_neg, conv_w, conv_b, d_skip, norm_w, layer, tb):
    nx = SSM_HEADS * SSM_HEADDIM
    H = SSM_HEADS
    lsel = lambda i: (layer, 0, 0)
    in_specs = [
        pl.BlockSpec((tb, SSM_XBC), lambda i: (i, P_SXBC // SSM_XBC)),
        pl.BlockSpec((tb, nx), lambda i: (i, P_SZ // nx)),
        pl.BlockSpec((tb, LANES), lambda i: (i, T_DT // LANES)),
        pl.BlockSpec((None, 1, LANES), lsel),
        pl.BlockSpec((None, 1, LANES), lsel),
        pl.BlockSpec((None, SSM_CONV, SSM_XBC), lsel),
        pl.BlockSpec((None, 1, SSM_XBC), lsel),
        pl.BlockSpec((None, 1, H), lsel),
        pl.BlockSpec((None, 1, nx), lsel),
    ]
    args = [pin, pin, tail, dt_bias, a_neg, conv_w, conv_b, d_skip, norm_w]
    out_specs = [pl.BlockSpec((tb, nx), lambda i: (i, 0))]
    out_shape = [jax.ShapeDtypeStruct((pin.shape[0], nx), BF16)]
    scratch = [pltpu.VMEM((SUBLANES, SSM_XBC), F32), pltpu.VMEM((H // 2, SSM_STATE, 2 * SSM_HEADDIM), F32)]
    return in_specs, args, out_specs, out_shape, scratch


MLA_PAD = 256
MLA_VX = MLA_V + 16


def _rope_pad(x, cosp, sinp):
    half = MLA_ROPE // 2
    return x * cosp + (pltpu.roll(x, half, 1) + pltpu.roll(x, LANES - half, 1)) * sinp


def _mla_stream(cq_ref, ckv_ref, kr_ref, qaw_ref, wqb_ref, kvaw_ref, wkvb_ref, qnw_ref,
                knw_n_ref, knw_r_ref, cos_ref, sin_ref, q_out, k_out, v_out):
    cosp = cos_ref[...]
    sinp = sin_ref[...]
    cq = cq_ref[...].astype(F32)
    ms = jnp.mean(cq * cq, axis=-1, keepdims=True)
    cqn = (cq * lax.rsqrt(ms + RMS_EPS) * qaw_ref[...]).astype(BF16)
    q = _dot(cqn, wqb_ref[...])
    yield
    ckv = ckv_ref[...].astype(F32)
    ms = jnp.mean(ckv * ckv, axis=-1, keepdims=True)
    ckvn = (ckv * lax.rsqrt(ms + RMS_EPS) * kvaw_ref[...]).astype(BF16)
    kv = _dot(ckvn, wkvb_ref[...])
    yield
    kr = kr_ref[...]
    ssr = jnp.sum(kr * kr, axis=-1, keepdims=True)
    scale = (MLA_QK ** -0.5) * LOG2E
    heads = range(MLA_HEADS)
    qh = [q[:, h * MLA_PAD:(h + 1) * MLA_PAD] for h in heads]
    kn = [kv[:, h * MLA_PAD: h * MLA_PAD + MLA_NOPE] for h in heads]
    qinv = [lax.rsqrt(jnp.sum(qh[h] * qh[h], axis=-1, keepdims=True) * (1.0 / MLA_QK) + RMS_EPS) for h in heads]
    kinv = [lax.rsqrt((jnp.sum(kn[h] * kn[h], axis=-1, keepdims=True) + ssr) * (1.0 / MLA_QK) + RMS_EPS)
            for h in heads]
    yield
    qh = [qh[h] * qinv[h] * qnw_ref[...] for h in heads]
    qr = [_rope_pad(qh[h][:, MLA_NOPE:], cosp, sinp) for h in heads]
    yield
    krh = [_rope_pad(kr * kinv[h] * knw_r_ref[...], cosp, sinp) for h in heads]
    ones = jnp.ones((MLA_VX - MLA_V, kv.shape[0]), F32)
    yield
    for h in heads:
        q_out[h] = (jnp.concatenate([qh[h][:, :MLA_NOPE], qr[h]], axis=1) * scale).T.astype(BF16)
        if h % 2:
            yield
    for h in heads:
        k_out[h] = jnp.concatenate([kn[h] * kinv[h] * knw_n_ref[...], krh[h]], axis=1).astype(BF16)
    yield
    for h in heads:
        vv = kv[:, h * MLA_PAD + MLA_NOPE:(h + 1) * MLA_PAD]
        v_out[h] = jnp.concatenate([vv.T, ones], axis=0).astype(BF16)
        if h % 2:
            yield


def _mla_specs(pin, tail, qaw, wqb, kvaw, wkvb, qnw, knw_n, knw_r, cosp, sinp, layer, tb):
    S = pin.shape[0]
    H = MLA_HEADS
    lsel = lambda i: (layer, 0, 0)
    in_specs = [
        pl.BlockSpec((tb, MLA_Q_RANK), lambda i: (i, P_CQ // MLA_Q_RANK)),
        pl.BlockSpec((tb, MLA_KV_RANK), lambda i: (i, P_CKV // MLA_KV_RANK)),
        pl.BlockSpec((tb, LANES), lambda i: (i, T_KROPE // LANES)),
        pl.BlockSpec((None, 1, MLA_Q_RANK), lsel),
        pl.BlockSpec((None, MLA_Q_RANK, H * MLA_PAD), lsel),
        pl.BlockSpec((None, 1, MLA_KV_RANK), lsel),
        pl.BlockSpec((None, MLA_KV_RANK, H * MLA_PAD), lsel),
        pl.BlockSpec((None, 1, MLA_PAD), lsel),
        pl.BlockSpec((None, 1, MLA_NOPE), lsel),
        pl.BlockSpec((None, 1, LANES), lsel),
        pl.BlockSpec((tb, LANES), lambda i: (i, 0)),
        pl.BlockSpec((tb, LANES), lambda i: (i, 0)),
    ]
    args = [pin, pin, tail, qaw, wqb, kvaw, wkvb, qnw, knw_n, knw_r, cosp, sinp]
    out_specs = [
        pl.BlockSpec((H, MLA_PAD, tb), lambda i: (0, 0, i)),
        pl.BlockSpec((H, tb, MLA_PAD), lambda i: (0, i, 0)),
        pl.BlockSpec((H, MLA_VX, tb), lambda i: (0, 0, i)),
    ]
    out_shape = [
        jax.ShapeDtypeStruct((H, MLA_PAD, S), BF16),
        jax.ShapeDtypeStruct((H, S, MLA_PAD), BF16),
        jax.ShapeDtypeStruct((H, MLA_VX, S), BF16),
    ]
    return in_specs, args, out_specs, out_shape, []


def _flash_body(it_ref, jt_ref, qt_ref, k_ref, vt_ref, o_ref, m_ref, acc_ref):
    t = pl.program_id(0)
    i = it_ref[t]
    j = jt_ref[t]
    tq = qt_ref.shape[2]
    tk = k_ref.shape[1]
    full = (j + 1) * tk <= i * tq
    last = j == ((i + 1) * tq - 1) // tk
    H = MLA_HEADS

    @pl.when(j == 0)
    def _():
        m_ref[...] = jnp.full_like(m_ref, -jnp.inf)
        acc_ref[...] = jnp.zeros_like(acc_ref)

    def step(diagonal):
        if diagonal:
            kc = (j * tk + lax.broadcasted_iota(jnp.int32, (tk, tq), 0)) // CHUNK
            qc = (i * tq + lax.broadcasted_iota(jnp.int32, (tk, tq), 1)) // CHUNK
            visible = kc <= qc

        def logits(h):
            s = _dot(k_ref[h], qt_ref[h])
            return jnp.where(visible, s, -jnp.inf) if diagonal else s

        def softmax(h, s):
            m_prev = m_ref[h]
            m_new = jnp.maximum(m_prev, jnp.max(s, axis=0, keepdims=True))
            alpha = jnp.exp2(m_prev - m_new)
            m_ref[h] = m_new
            return alpha, jnp.exp2((s - m_new).astype(BF16))

        def accumulate(h, alpha, p):
            acc_ref[h] = alpha * acc_ref[h] + _dot(vt_ref[h], p)

        s_next = logits(0)
        pending = None
        for h in range(H):
            s_cur = s_next
            if h + 1 < H:
                s_next = logits(h + 1)
            if pending is not None:
                accumulate(*pending)
            pending = (h,) + softmax(h, s_cur)
        accumulate(*pending)

    @pl.when(full)
    def _():
        step(False)

    @pl.when(jnp.logical_not(full))
    def _():
        step(True)

    @pl.when(last)
    def _():
        for h in range(H):
            acc = acc_ref[h]
            o_ref[:, h * MLA_V:(h + 1) * MLA_V] = (acc[:MLA_V] / acc[MLA_V:MLA_V + 1]).T.astype(BF16)


def _flash(qt, k, vt, tq=1024, tk=1024):
    H, S, _ = k.shape
    pairs = [(i, j) for i in range(S // tq) for j in range(((i + 1) * tq - 1) // tk + 1)]
    it = jnp.asarray([p[0] for p in pairs], jnp.int32)
    jt = jnp.asarray([p[1] for p in pairs], jnp.int32)
    grid_spec = pltpu.PrefetchScalarGridSpec(
        num_scalar_prefetch=2,
        grid=(len(pairs),),
        in_specs=[
            pl.BlockSpec((H, MLA_PAD, tq), lambda t, it, jt: (0, 0, it[t])),
            pl.BlockSpec((H, tk, MLA_PAD), lambda t, it, jt: (0, jt[t], 0)),
            pl.BlockSpec((H, MLA_VX, tk), lambda t, it, jt: (0, 0, jt[t])),
        ],
        out_specs=pl.BlockSpec((tq, H * MLA_V), lambda t, it, jt: (it[t], 0)),
        scratch_shapes=[pltpu.VMEM((H, 1, tq), F32), pltpu.VMEM((H, MLA_VX, tq), F32)],
    )
    return pl.pallas_call(
        _flash_body,
        grid_spec=grid_spec,
        out_shape=jax.ShapeDtypeStruct((S, H * MLA_V), BF16),
        compiler_params=_cparams(("arbitrary",)),
        name="mla_flash",
    )(it, jt, qt, k, vt)


def _rwkv_body(*refs, mix_v, companion):
    if mix_v:
        (r_ref, k_ref, v_ref, wl_ref, al_ref, gl_ref, mur, muk, muv, muwl, mual, mugl, w0, a0, w2, a2, g2,
         kk_ref, ka_ref, lnw_ref, lnb_ref, rk_ref, vf_ref, v0, v1, v2,
         o_ref, c_r, c_k, c_v, c_wl, c_al, c_gl, s_ref) = refs
    else:
        (r_ref, k_ref, v_ref, wl_ref, al_ref, gl_ref, mur, muk, muv, muwl, mual, mugl, w0, a0, w2, a2, g2,
         kk_ref, ka_ref, lnw_ref, lnb_ref, rk_ref,
         o_ref, vfirst_ref, c_r, c_k, c_v, c_wl, c_al, c_gl, s_ref) = refs
    tb = r_ref.shape[0]
    L = RWKV_L
    N = RWKV_HEAD
    L2 = 2 * L
    npair = RWKV_W // LANES
    half = tb // 2
    carries = (c_r, c_k, c_v, c_wl, c_al, c_gl)

    @pl.when(pl.program_id(0) == 0)
    def _():
        for c in carries:
            c[...] = jnp.zeros_like(c)
        s_ref[...] = jnp.zeros_like(s_ref)

    def mixed(x_ref, c_ref, mu_ref):
        x = x_ref[...].astype(F32)
        prev = _shift_rows(x, c_ref[...], 1)
        c_ref[...] = x[tb - SUBLANES:, :]
        return x + (prev - x) * mu_ref[...]

    r = mixed(r_ref, c_r, mur)
    k = mixed(k_ref, c_k, muk)
    v = mixed(v_ref, c_v, muv)
    wl = mixed(wl_ref, c_wl, muwl)
    al = mixed(al_ref, c_al, mual)
    gl = mixed(gl_ref, c_gl, mugl)
    if not mix_v:
        vfirst_ref[...] = v

    lane = lax.broadcasted_iota(jnp.int32, (L, LANES), 1)
    m0 = lane < N
    r2 = lax.broadcasted_iota(jnp.int32, (L2, L2), 0)
    c2 = lax.broadcasted_iota(jnp.int32, (L2, L2), 1)
    same = (r2 // L) == (c2 // L)
    strict = same & (c2 < r2)
    incl = same & (c2 <= r2)
    eye = (r2 == c2).astype(F32)
    ones_bd = ((r2 // N) == (c2 // N)).astype(BF16)
    rb = lax.broadcasted_iota(jnp.int32, (half, half), 0)
    cb = lax.broadcasted_iota(jnp.int32, (half, half), 1)
    tril_bd = ((rb // L) == (cb // L)) & (cb <= rb)

    def stack(x):
        return jnp.concatenate([jnp.where(m0, x, 0.0), jnp.where(m0, 0.0, x)], axis=0).astype(BF16)

    probs = [(c, p) for c in range(half // L) for p in range(npair)]
    rows_of = lambda c: slice(c * L, (c + 1) * L)
    lanes_of = lambda p: slice(p * LANES, (p + 1) * LANES)
    opnds = [{}, {}]
    gates = [None, None]
    states = [s_ref[p] for p in range(npair)]

    def front(hf):
        rs = slice(hf * half, (hf + 1) * half)
        w_raw = w0[...] + _mm3(jnp.tanh(wl[rs]), w2[...])
        lw = -jnp.exp(-_softplus(-w_raw) - 0.5)
        yield
        a_sig = _sigmoid(a0[...] + _mm3(al[rs], a2[...]))
        gates[hf] = _mm1(_sigmoid(gl[rs]), g2[...])
        yield
        vh = v[rs]
        if mix_v:
            lora = _mm1(_mm1(vh, v1[...]), v2[...])
            vh = vh + (vf_ref[rs, :] - vh) * _sigmoid(v0[...] + lora)
        yield
        kk = k[rs] * kk_ref[...]
        parts = []
        for p in range(npair):
            kp = kk[:, lanes_of(p)]
            n2 = _seg_sum(kp * kp, ones_bd)
            parts.append(kp / jnp.maximum(jnp.sqrt(n2), 1e-12))
            if p % 2:
                yield
        kk = jnp.concatenate(parts, axis=1)
        kh = k[rs] * (1.0 + (a_sig - 1.0) * ka_ref[...])
        ah = -kk
        bh = kk * a_sig
        rh = r[rs]
        cum_blk = _mm_exact_lhs(tril_bd, lw)
        yield
        for n, (c, p) in enumerate(probs):
            rows, sl = rows_of(c), lanes_of(p)
            cum = cum_blk[rows, sl]
            cum_end = cum[L - 1:L, :]
            rr = rh[rows, sl]
            kc = kh[rows, sl]
            vc = vh[rows, sl]
            e_neg = jnp.exp(-cum)
            e_end = jnp.exp(cum_end - cum)
            re = rr * jnp.exp(cum)
            xr_f = jnp.concatenate([jnp.where(m0, re, 0.0), jnp.where(m0, 0.0, re)], axis=0)
            opnds[hf][c, p] = dict(
                xr_f=xr_f, xa=stack(ah[rows, sl] * jnp.exp(cum - lw[rows, sl])), xr=xr_f.astype(BF16),
                xb=stack(bh[rows, sl] * e_neg), xk=stack(kc * e_neg), xbh=stack(bh[rows, sl] * e_end),
                xkh=stack(kc * e_end), vs=stack(vc), decay_end=jnp.exp(cum_end),
                rk=rr * kc * rk_ref[:, sl], v=vc)
            if n % 2:
                yield

    def back(hf):
        opnd = opnds[hf]
        amat = {}
        for cp in probs:
            o = opnd[cp]
            pmat = _dot_nt(jnp.concatenate([o["xa"], o["xr"]], axis=0),
                           jnp.concatenate([o["xb"], o["xk"]], axis=0))
            amat[cp] = dict(
                ab=jnp.where(strict, pmat[:L2, :L2], 0.0),
                ak=jnp.where(strict, pmat[:L2, L2:], 0.0).astype(BF16),
                rb=jnp.where(incl, pmat[L2:, :L2], 0.0).astype(BF16),
                rk=jnp.where(incl, pmat[L2:, L2:], 0.0).astype(BF16))
        yield
        tinv = {cp: eye + amat[cp]["ab"] for cp in probs}
        pw = {cp: amat[cp]["ab"] for cp in probs}
        for _ in range(int(math.log2(L)) - 1):
            for cp in probs:
                pwb = pw[cp].astype(BF16)
                pw[cp] = _dot(pwb, pwb)
            yield
            for cp in probs:
                tinv[cp] = tinv[cp] + _mm1(tinv[cp], pw[cp])
            yield
        w1 = {cp: _dot(amat[cp]["ak"], opnd[cp]["vs"]) for cp in probs}
        yield
        mub = {}
        for cp in probs:
            rhs = jnp.concatenate([opnd[cp]["xa"], w1[cp].astype(BF16)], axis=1)
            mub[cp] = _dot(tinv[cp].astype(BF16), rhs).astype(BF16)
        yield
        coef = {}
        for cp in probs:
            o = opnd[cp]
            ry = _dot(amat[cp]["rb"], mub[cp])
            gh = _dot_tn(o["xbh"], mub[cp])
            coef[cp] = dict(
                m_r=(o["xr_f"] + ry[:, :LANES]).astype(BF16),
                y1=ry[:, LANES:] + _dot(amat[cp]["rk"], o["vs"]),
                g=(eye * o["decay_end"] + gh[:, :LANES]).astype(BF16),
                h=gh[:, LANES:] + _dot_tn(o["xkh"], o["vs"]))
        yield
        ys = {}
        for (c, p) in probs:
            st = states[p].astype(BF16)
            cf = coef[c, p]
            yst = _dot(cf["m_r"], st) + cf["y1"]
            states[p] = _dot(cf["g"], st) + cf["h"]
            ys[c, p] = yst[:L] + yst[L:]
        yield
        yc = {cp: ys[cp] - _seg_sum(ys[cp], ones_bd) * (1.0 / N) for cp in probs}
        yield
        bonus = {cp: _seg_sum(opnd[cp]["rk"], ones_bd) * opnd[cp]["v"] for cp in probs}
        var = {cp: _seg_sum(yc[cp] * yc[cp], ones_bd) * (1.0 / N) for cp in probs}
        yield
        for (c, p) in probs:
            rows, sl = rows_of(c), lanes_of(p)
            out_rows = slice(hf * half + c * L, hf * half + (c + 1) * L)
            yn = yc[c, p] * lax.rsqrt(var[c, p] + RWKV_GN_EPS) * lnw_ref[:, sl] + lnb_ref[:, sl]
            o_ref[out_rows, sl] = ((yn + bonus[c, p]) * gates[hf][rows, sl]).astype(BF16)

    _interleave(front(0), front(1))
    _interleave(_roundrobin(back(0), back(1)), companion)
    for p in range(npair):
        s_ref[p] = states[p]


def _rwkv_specs(pin, tail, vec, mats, vfirst, vmix, layer, tb):
    S = pin.shape[0]
    W = RWKV_W
    lsel = lambda i: (layer, 0, 0)
    lsel1 = lambda i: (layer - 1, 0, 0)
    in_specs = [
        pl.BlockSpec((tb, W), lambda i: (i, P_WR // W)),
        pl.BlockSpec((tb, W), lambda i: (i, P_WK // W)),
        pl.BlockSpec((tb, W), lambda i: (i, P_WV // W)),
        pl.BlockSpec((tb, LANES), lambda i: (i, T_WL // LANES)),
        pl.BlockSpec((tb, LANES), lambda i: (i, T_AL // LANES)),
        pl.BlockSpec((tb, LANES), lambda i: (i, T_GL // LANES)),
    ]
    args = [pin] * 3 + [tail] * 3
    for name, width in (("mu_r", W), ("mu_k", W), ("mu_v", W), ("mu_wl", LANES), ("mu_al", LANES),
                        ("mu_gl", LANES), ("w0", W), ("a0", W)):
        in_specs.append(pl.BlockSpec((None, 1, width), lsel))
        args.append(vec[name])
    for name in ("w2", "a2", "g2"):
        in_specs.append(pl.BlockSpec((None, LANES, W), lsel))
        args.append(mats[name])
    for name in ("k_k", "k_a", "ln_w", "ln_b", "r_k"):
        in_specs.append(pl.BlockSpec((None, 1, W), lsel))
        args.append(vec[name])
    mix_v = vmix is not None
    if mix_v:
        v0, v1, v2 = vmix
        in_specs += [pl.BlockSpec((tb, W), lambda i: (i, 0)),
                     pl.BlockSpec((None, 1, W), lsel1),
                     pl.BlockSpec((None, W, LANES), lsel1),
                     pl.BlockSpec((None, LANES, W), lsel1)]
        args += [vfirst, v0, v1, v2]
    out_spec = pl.BlockSpec((tb, W), lambda i: (i, 0))
    out_specs = [out_spec] if mix_v else [out_spec, out_spec]
    out_shape = [jax.ShapeDtypeStruct((S, W), BF16)]
    if not mix_v:
        out_shape.append(jax.ShapeDtypeStruct((S, W), F32))
    scratch = ([pltpu.VMEM((SUBLANES, W), F32)] * 3 + [pltpu.VMEM((SUBLANES, LANES), F32)] * 3
               + [pltpu.VMEM((W // LANES, LANES, LANES), F32)])
    return in_specs, args, out_specs, out_shape, scratch


def _mixers_body(*refs, counts, mix_v, block_decay):
    refs = list(refs)
    take = lambda n: [refs.pop(0) for _ in range(n)]
    ins = [take(n) for n in counts["in"]]
    outs = [take(n) for n in counts["out"]]
    scr = [take(n) for n in counts["scratch"]]
    ret, ssd, mla, rwkv = (ins[m] + outs[m] + scr[m] for m in range(4))
    companion = _chain(_ssd_stream(*ssd), _ret_stream(*ret, block_decay=block_decay), _mla_stream(*mla))
    _rwkv_body(*rwkv, mix_v=mix_v, companion=companion)


def _mixers(pin, tail, cos_ret, sin_ret, gn_w, ssd_params, mla_params, cos_mla, sin_mla, vec, mats, vfirst, vmix,
            layer, tb=ROW_BLOCK):
    S = pin.shape[0]
    r_in, r_args, r_out, r_shape, r_scr, bd = _ret_specs(pin, cos_ret, sin_ret, gn_w, layer, tb)
    parts = [
        (r_in, r_args, r_out, r_shape, r_scr),
        _ssd_specs(pin, tail, *ssd_params, layer, tb),
        _mla_specs(pin, tail, *mla_params, cos_mla, sin_mla, layer, tb),
        _rwkv_specs(pin, tail, vec, mats, vfirst, vmix, layer, tb),
    ]
    counts = {"in": [len(p[0]) for p in parts], "out": [len(p[2]) for p in parts],
              "scratch": [len(p[4]) for p in parts]}
    outs = pl.pallas_call(
        functools.partial(_mixers_body, counts=counts, mix_v=vmix is not None, block_decay=bd),
        grid=(S // tb,),
        in_specs=[sp for p in parts for sp in p[0]],
        out_specs=[sp for p in parts for sp in p[2]],
        out_shape=[sh for p in parts for sh in p[3]],
        scratch_shapes=[sc for p in parts for sc in p[4]],
        compiler_params=_cparams(("arbitrary",)),
        name="mixers",
    )(*[a for p in parts for a in p[1]])
    o_ret, o_ssd, qt, k, vt, o_rwkv = outs[:6]
    return o_ret, o_ssd, (qt, k, vt), o_rwkv, (outs[6] if vmix is None else None)


def _merge_body(oa_ref, ob_ref, oc_ref, od_ref, g0_ref, g1_ref, g2_ref, g3_ref, wb_ref, out_ref):
    o_refs = (oa_ref, ob_ref, oc_ref, od_ref)
    g_refs = (g0_ref, g1_ref, g2_ref, g3_ref)
    acc = None
    for n in range(N_BRANCH):
        contrib = _sigmoid(g_refs[n][...].astype(F32)) * _dot(o_refs[n][...], wb_ref[n])
        acc = contrib if acc is None else acc + contrib
    out_ref[...] = acc.astype(BF16)


def _merge(o_a, o_b, o_c, o_d, pin, w_branch, layer, tm=512):
    S = pin.shape[0]
    D = D_MODEL
    ospec = pl.BlockSpec((tm, BRANCH_W), lambda i: (i, 0))
    gspec = lambda n: pl.BlockSpec((tm, D), lambda i: (i, n))
    return pl.pallas_call(
        _merge_body,
        grid=(S // tm,),
        in_specs=[ospec, ospec, ospec, ospec, gspec(0), gspec(1), gspec(2), gspec(3),
                  pl.BlockSpec((None, N_BRANCH, BRANCH_W, D), lambda i: (layer, 0, 0, 0))],
        out_specs=pl.BlockSpec((tm, D), lambda i: (i, 0)),
        out_shape=jax.ShapeDtypeStruct((S, D), BF16),
        compiler_params=_cparams(("arbitrary",)),
        name="merge",
    )(o_a, o_b, o_c, o_d, pin, pin, pin, pin, w_branch)


def _outproj_body(m_ref, w_ref, x_ref, o_ref):
    o_ref[...] = x_ref[...] + _dot(m_ref[...], w_ref[...])


def _outproj(merged, w_out, x, layer, tm=512, tn=2048):
    S, D = x.shape
    return pl.pallas_call(
        _outproj_body,
        grid=(S // tm, D // tn),
        in_specs=[
            pl.BlockSpec((tm, D), lambda i, j: (i, 0)),
            pl.BlockSpec((None, D, tn), lambda i, j: (layer, 0, j)),
            pl.BlockSpec((tm, tn), lambda i, j: (i, j)),
        ],
        out_specs=pl.BlockSpec((tm, tn), lambda i, j: (i, j)),
        out_shape=jax.ShapeDtypeStruct((S, D), F32),
        compiler_params=_cparams(("arbitrary", "arbitrary")),
        name="outproj",
    )(merged, w_out, x)


def _ffn_body(x_ref, nw_ref, wg_ref, wu_ref, wd_ref, out_ref, h_ref):
    @pl.when(pl.program_id(1) == 0)
    def _():
        x = x_ref[...]
        ms = jnp.mean(x * x, axis=-1, keepdims=True)
        h_ref[...] = (x * lax.rsqrt(ms + RMS_EPS) * nw_ref[...]).astype(BF16)
        out_ref[...] = x

    h = h_ref[...]
    act = (_silu(_dot(h, wg_ref[...])) * _dot(h, wu_ref[...])).astype(BF16)
    out_ref[...] += _dot(act, wd_ref[...])


def _ffn(x, norm_w, w_gu, w_down, layer, tm=1024, tf=512):
    S, D = x.shape
    nf = D_FF // tf
    return pl.pallas_call(
        _ffn_body,
        grid=(S // tm, nf),
        in_specs=[
            pl.BlockSpec((tm, D), lambda i, f: (i, 0)),
            pl.BlockSpec((None, 1, D), lambda i, f: (layer, 0, 0)),
            pl.BlockSpec((None, D, tf), lambda i, f: (layer, 0, f)),
            pl.BlockSpec((None, D, tf), lambda i, f: (layer, 0, f + nf)),
            pl.BlockSpec((None, tf, D), lambda i, f: (layer, f, 0)),
        ],
        out_specs=pl.BlockSpec((tm, D), lambda i, f: (i, 0)),
        out_shape=jax.ShapeDtypeStruct((S, D), F32),
        scratch_shapes=[pltpu.VMEM((tm, D), BF16)],
        compiler_params=_cparams(("arbitrary", "arbitrary")),
        name="ffn",
    )(x, norm_w, w_gu, w_gu, w_down)


def _pad_last(a, width):
    pad = width - a.shape[-1]
    if pad == 0:
        return a
    return jnp.pad(a, [(0, 0)] * (a.ndim - 1) + [(0, pad)])


def _pad_axis(a, axis, width):
    pad = width - a.shape[axis]
    if pad == 0:
        return a
    cfg = [(0, 0)] * a.ndim
    cfg[axis] = (0, pad)
    return jnp.pad(a, cfg)


_MAIN_PIECES = (
    (_O_GATE, P_GATE, N_BRANCH * D_MODEL),
    (_O_RQ, P_RQ, 512), (_O_RK, P_RK, 512), (_O_RV, P_RV, 512), (_O_RG, P_RG, 512),
    (_O_SXBC, P_SXBC, SSM_XBC), (_O_SZ, P_SZ, 512),
    (_O_CQ, P_CQ, MLA_Q_RANK),
    (_O_RW, P_WR, RWKV_W), (_O_RW + RWKV_W, P_WK, RWKV_W), (_O_RW + 2 * RWKV_W, P_WV, RWKV_W),
    (_O_CKV, P_CKV, MLA_KV_RANK),
)
_TAIL_PIECES = (
    (_O_SDT, T_DT, SSM_HEADS),
    (_O_KROPE, T_KROPE, MLA_ROPE),
    (_O_RW + 3 * RWKV_W, T_WL, RWKV_W_LORA),
    (_O_RW + 3 * RWKV_W + RWKV_W_LORA, T_AL, RWKV_A_LORA),
    (_O_RW + 3 * RWKV_W + RWKV_W_LORA + RWKV_A_LORA, T_GL, RWKV_G_LORA),
)
PACK_ROWS = 512


def _pack_tables(pieces, total, tile):
    src_of, valid_of = [], []
    for t in range(total // tile):
        lo = t * tile
        src, valid = 0, 0
        for s0, d0, width in pieces:
            if d0 <= lo < d0 + width:
                src, valid = s0 + (lo - d0), min(tile, d0 + width - lo)
        assert src % SUBLANES == 0
        src_of.append(src // SUBLANES)
        valid_of.append(valid)
    return jnp.asarray(src_of, jnp.int32), jnp.asarray(valid_of, jnp.int32)


def _pack_body(src_ref, valid_ref, w_ref, o_ref):
    w = w_ref[0]
    rows = lax.broadcasted_iota(jnp.int32, w.shape, 0)
    keep = rows < valid_ref[pl.program_id(1)]
    o_ref[...] = jnp.where(keep, w, 0.0).astype(BF16)


def _pack_rows(w_t, pieces, total, tile, name):
    depth, _, d = w_t.shape
    src_of, valid_of = _pack_tables(pieces, total, tile)
    grid_spec = pltpu.PrefetchScalarGridSpec(
        num_scalar_prefetch=2,
        grid=(depth, total // tile),
        in_specs=[pl.BlockSpec((pl.Element(1), pl.Element(tile), pl.Element(d)),
                               lambda l, t, src, valid: (l, src[t] * SUBLANES, 0))],
        out_specs=pl.BlockSpec((None, tile, d), lambda l, t, src, valid: (l, t, 0)),
    )
    return pl.pallas_call(
        _pack_body,
        grid_spec=grid_spec,
        out_shape=jax.ShapeDtypeStruct((depth, total, d), BF16),
        compiler_params=_cparams(("arbitrary", "arbitrary")),
        name=name,
    )(src_of, valid_of, w_t)


def _pack_w_in(w_in):
    w_t = jnp.swapaxes(w_in, 1, 2)
    w_main_t = _pack_rows(w_t, _MAIN_PIECES, N_MAIN, PACK_ROWS, "pack_w_main")
    w_tail_t = _pack_rows(w_t, _TAIL_PIECES, N_TAIL, LANES, "pack_w_tail")
    return w_main_t, w_tail_t


def _rope_tables(positions, dim):
    inv = 1.0 / (ROPE_THETA ** (jnp.arange(0, dim, 2, dtype=F32) / dim))
    ang = positions.astype(F32)[:, None] * inv
    return jnp.cos(ang), jnp.sin(ang)


def kernel(x, positions, norm1_w, w_in, ret_gn_w, ssm_conv_w, ssm_conv_b, ssm_dt_bias, ssm_a_log, ssm_d, ssm_norm_w, mla_q_a_norm_w, mla_w_qb, mla_kv_a_norm_w, mla_w_kvb, mla_q_norm_w, mla_k_norm_w, rwkv_mu, rwkv_w0, rwkv_w2, rwkv_a0, rwkv_a2, rwkv_g2, rwkv_v0, rwkv_v1, rwkv_v2, rwkv_k_k, rwkv_k_a, rwkv_r_k, rwkv_ln_w, rwkv_ln_b, w_branch, w_out, norm2_w, ffn_w_gu, ffn_w_down):
    B, S, D = x.shape
    assert B == 1 and D == D_MODEL and S % 1024 == 0
    xs = x[0]
    pos = positions[0]

    c_r, s_r = _rope_tables(pos, RET_DK)
    cos_ret = jnp.concatenate([c_r, c_r], axis=-1)
    sin_ret = jnp.concatenate([-s_r, s_r], axis=-1)
    c_m, s_m = _rope_tables(pos, MLA_ROPE)
    cos_mla = _pad_last(jnp.concatenate([c_m, c_m], axis=-1), LANES)
    sin_mla = _pad_last(jnp.concatenate([-s_m, s_m], axis=-1), LANES)

    row = lambda a: a[:, None, :]
    w_main, w_tail = _pack_w_in(w_in)
    norm1 = row(norm1_w)
    norm2 = row(norm2_w)
    gn_w = row(ret_gn_w)
    conv_b = row(ssm_conv_b)
    d_skip = row(ssm_d)
    ssm_nw = row(ssm_norm_w)
    dt_bias = row(_pad_last(ssm_dt_bias, LANES))
    a_neg = row(_pad_last(-jnp.exp(ssm_a_log.astype(F32)), LANES))
    qaw = row(mla_q_a_norm_w)
    kvaw = row(mla_kv_a_norm_w)
    wqb = _pad_last(mla_w_qb.reshape(DEPTH, MLA_Q_RANK, MLA_HEADS, MLA_QK), MLA_PAD)
    wqb = wqb.reshape(DEPTH, MLA_Q_RANK, MLA_HEADS * MLA_PAD).astype(BF16)
    wkvb = mla_w_kvb.astype(BF16)
    qnw = row(_pad_last(mla_q_norm_w, MLA_PAD))
    knw_n = row(mla_k_norm_w[:, :MLA_NOPE])
    knw_r = row(_pad_last(mla_k_norm_w[:, MLA_NOPE:], LANES))
    W = RWKV_W
    mu = rwkv_mu
    vec = {
        "mu_r": row(mu[:, :W]), "mu_k": row(mu[:, W:2 * W]), "mu_v": row(mu[:, 2 * W:3 * W]),
        "mu_wl": row(_pad_last(mu[:, 3 * W:3 * W + RWKV_W_LORA], LANES)),
        "mu_al": row(_pad_last(mu[:, 3 * W + RWKV_W_LORA:3 * W + RWKV_W_LORA + RWKV_A_LORA], LANES)),
        "mu_gl": row(mu[:, 3 * W + RWKV_W_LORA + RWKV_A_LORA:]),
        "w0": row(rwkv_w0), "a0": row(rwkv_a0), "k_k": row(rwkv_k_k), "k_a": row(rwkv_k_a),
        "ln_w": row(rwkv_ln_w), "ln_b": row(rwkv_ln_b), "r_k": row(rwkv_r_k.reshape(DEPTH, W)),
    }
    mats = {"w2": _pad_axis(rwkv_w2, 1, LANES), "a2": _pad_axis(rwkv_a2, 1, LANES), "g2": rwkv_g2}
    v0 = row(rwkv_v0)
    v1 = _pad_last(rwkv_v1, LANES)
    v2 = _pad_axis(rwkv_v2, 1, LANES)
    wb = w_branch.astype(BF16)
    wo = w_out.astype(BF16)
    wgu = ffn_w_gu.astype(BF16)
    wdn = ffn_w_down.astype(BF16)

    v_first = None
    for l in range(DEPTH):
        pin, tail = _inproj(xs, norm1, w_main, w_tail, l)
        vmix = None if l == 0 else (v0, v1, v2)
        o_a, o_b, qkv, o_d, vf = _mixers(
            pin, tail, cos_ret, sin_ret, gn_w, (dt_bias, a_neg, ssm_conv_w, conv_b, d_skip, ssm_nw),
            (qaw, wqb, kvaw, wkvb, qnw, knw_n, knw_r), cos_mla, sin_mla, vec, mats, v_first, vmix, l)
        o_c = _flash(*qkv)
        if l == 0:
            v_first = vf
        merged = _merge(o_a, o_b, o_c, o_d, pin, wb, l)
        xs = _outproj(merged, wo, xs, l)
        xs = _ffn(xs, norm2, wgu, wdn, l)
    return xs[None]
```

```python
import functools
import math

import numpy as np
import jax
import jax.numpy as jnp
from jax import lax
from jax.experimental import pallas as pl
from jax.experimental.pallas import tpu as pltpu

F32 = jnp.float32
BF16 = jnp.bfloat16

D_MODEL = 2048
DEPTH = 4
CHUNK = 64
N_BRANCH = 4
BRANCH_W = 512
RMS_EPS = 1e-6
GN_EPS = 1e-5
ROPE_THETA = 10000.0
RET_HEADS, RET_DK, RET_DV = 4, 128, 128
SSM_HEADS, SSM_HEADDIM, SSM_GROUPS, SSM_STATE, SSM_CONV = 8, 64, 2, 128, 4
SSM_XBC = SSM_HEADS * SSM_HEADDIM + 2 * SSM_GROUPS * SSM_STATE
MLA_HEADS, MLA_Q_RANK, MLA_KV_RANK, MLA_NOPE, MLA_ROPE, MLA_V = 4, 512, 256, 128, 64, 128
MLA_QK = MLA_NOPE + MLA_ROPE
RWKV_HEADS, RWKV_HEAD = 8, 64
RWKV_W = RWKV_HEADS * RWKV_HEAD
RWKV_W_LORA, RWKV_A_LORA, RWKV_V_LORA, RWKV_G_LORA = 64, 64, 32, 128
RWKV_GN_EPS = 64e-5
D_FF = 5632

LANES = 128
SUBLANES = 8
VMEM_LIMIT_BYTES = 56 * 1024 * 1024

_O_RQ, _O_RK, _O_RV, _O_RG = 0, 512, 1024, 1536
_O_SZ, _O_SXBC, _O_SDT = 2048, 2560, 3584
_O_CQ, _O_CKV, _O_KROPE = 3592, 4104, 4360
_O_RW = 4424
_O_GATE = 6216
P_GATE = 0
P_RQ, P_RK, P_RV, P_RG = 8192, 8704, 9216, 9728
P_SXBC, P_SZ = 10240, 11264
P_CQ = 11776
P_WR, P_WK, P_WV = 12288, 12800, 13312
P_CKV = 13824
N_MAIN = 14336
T_DT, T_KROPE, T_WL, T_AL, T_GL = 0, 128, 256, 384, 512
N_TAIL = 640

ROW_BLOCK = 256
RWKV_L = 64
LOG2E = 1.4426950408889634


def _cparams(sem):
    return pltpu.CompilerParams(dimension_semantics=sem, vmem_limit_bytes=VMEM_LIMIT_BYTES)


def _dot(a, b):
    return jnp.dot(a, b, preferred_element_type=F32)


def _dot_nt(a, b):
    return lax.dot_general(a, b, (((1,), (1,)), ((), ())), preferred_element_type=F32)


def _dot_tn(a, b):
    return lax.dot_general(a, b, (((0,), (0,)), ((), ())), preferred_element_type=F32)


def _split_bf16(x):
    hi = x.astype(BF16)
    lo = (x - hi.astype(F32)).astype(BF16)
    return hi, lo


def _mm3(a, b, dot=_dot):
    ah, al = _split_bf16(a)
    bh, bl = _split_bf16(b)
    return dot(ah, bh) + dot(ah, bl) + dot(al, bh)


def _mm1(a, b, dot=_dot):
    return dot(a.astype(BF16), b.astype(BF16))


def _mm_exact_lhs(a01, x, dot=_dot):
    xh, xl = _split_bf16(x)
    a = a01.astype(BF16)
    return dot(a, xh) + dot(a, xl)


def _seg_sum(x, ones_bd):
    return _dot(x.astype(BF16), ones_bd)


def _sigmoid(x):
    return 0.5 * jnp.tanh(0.5 * x) + 0.5


def _silu(x):
    return x * _sigmoid(x)


def _softplus(x):
    return jnp.maximum(x, 0.0) + jnp.log(1.0 + jnp.exp(-jnp.abs(x)))


def _shift_rows(x, carry8, s):
    xr = pltpu.roll(x, s, 0)
    pr = pltpu.roll(carry8, s, 0)
    row = lax.broadcasted_iota(jnp.int32, carry8.shape, 0)
    top = jnp.where(row < s, pr, xr[:SUBLANES])
    return jnp.concatenate([top, xr[SUBLANES:]], axis=0)


def _inproj_body(x_ref, nw_ref, w_ref, wt_ref, o_ref, ot_ref, h_ref):
    @pl.when(pl.program_id(1) == 0)
    def _():
        x = x_ref[...]
        ms = jnp.mean(x * x, axis=-1, keepdims=True)
        h = (x * lax.rsqrt(ms + RMS_EPS) * nw_ref[...]).astype(BF16)
        h_ref[...] = h
        ot_ref[...] = _dot_nt(h, wt_ref[...])

    o_ref[...] = _dot_nt(h_ref[...], w_ref[...]).astype(BF16)


def _inproj(x, norm_w, w_main, w_tail, layer, tm=1024, tn=1024):
    S, D = x.shape
    return pl.pallas_call(
        _inproj_body,
        grid=(S // tm, N_MAIN // tn),
        in_specs=[
            pl.BlockSpec((tm, D), lambda i, j: (i, 0)),
            pl.BlockSpec((None, 1, D), lambda i, j: (layer, 0, 0)),
            pl.BlockSpec((None, tn, D), lambda i, j: (layer, j, 0)),
            pl.BlockSpec((None, N_TAIL, D), lambda i, j: (layer, 0, 0)),
        ],
        out_specs=[pl.BlockSpec((tm, tn), lambda i, j: (i, j)),
                   pl.BlockSpec((tm, N_TAIL), lambda i, j: (i, 0))],
        out_shape=[jax.ShapeDtypeStruct((S, N_MAIN), BF16), jax.ShapeDtypeStruct((S, N_TAIL), F32)],
        scratch_shapes=[pltpu.VMEM((tm, D), BF16)],
        compiler_params=_cparams(("arbitrary", "arbitrary")),
        name="inproj",
    )(x, norm_w, w_main, w_tail)


def _roundrobin(*streams):
    active = list(streams)
    while active:
        for g in list(active):
            try:
                next(g)
            except StopIteration:
                active.remove(g)
            else:
                yield


def _chain(*streams):
    for g in streams:
        yield from g


def _interleave(*streams):
    for _ in _roundrobin(*streams):
        pass


def _ret_stream(q_ref, k_ref, v_ref, g_ref, cos_ref, sin_ref, m_ref, qd_ref, kd_ref, gnw_ref,
                o_ref, s_ref, *, block_decay):
    @pl.when(pl.program_id(0) == 0)
    def _():
        s_ref[...] = jnp.zeros_like(s_ref)

    cos = cos_ref[...]
    sin = sin_ref[...]
    heads = range(RET_HEADS)
    lanes = [slice(h * RET_DK, (h + 1) * RET_DK) for h in heads]
    q, k, vb = [], [], []
    for h in heads:
        qh = q_ref[:, lanes[h]].astype(F32)
        kh = k_ref[:, lanes[h]].astype(F32)
        q.append(qh * cos + pltpu.roll(qh, RET_DK // 2, 1) * sin)
        k.append((kh * cos + pltpu.roll(kh, RET_DK // 2, 1) * sin) * (RET_DK ** -0.5))
        vb.append(v_ref[:, lanes[h]])
        if h % 2:
            yield
    sc = [(_dot_nt(q[h].astype(BF16), k[h].astype(BF16)) * m_ref[h]).astype(BF16) for h in heads]
    yield
    st = [s_ref[h] for h in heads]
    o = [_dot(sc[h], vb[h]) + _dot((q[h] * qd_ref[h]).astype(BF16), st[h].astype(BF16)) for h in heads]
    yield
    for h in heads:
        s_ref[h] = block_decay[h] * st[h] + _dot_tn((k[h] * kd_ref[h]).astype(BF16), vb[h])
    yield
    oc = [o[h] - jnp.mean(o[h], axis=-1, keepdims=True) for h in heads]
    var = [jnp.mean(oc[h] * oc[h], axis=-1, keepdims=True) for h in heads]
    yield
    for h in heads:
        on = oc[h] * lax.rsqrt(var[h] + GN_EPS) * gnw_ref[:, lanes[h]]
        o_ref[:, lanes[h]] = (_silu(g_ref[:, lanes[h]].astype(F32)) * on).astype(BF16)
        if h % 2:
            yield


def _ret_tables(tb):
    lg = np.log1p(-np.exp2(-5.0 - np.arange(RET_HEADS, dtype=np.float64)))
    pos = np.arange(tb, dtype=np.float64)
    dist = np.abs(pos[:, None] - pos[None, :])
    visible = (pos[None, :] // CHUNK) <= (pos[:, None] // CHUNK)
    mask = np.where(visible[None], np.exp(lg[:, None, None] * dist[None]), 0.0)
    qd = np.exp(lg[:, None] * (pos[None, :] + 1.0))[:, :, None]
    kd = np.exp(lg[:, None] * (tb - 1.0 - pos[None, :]))[:, :, None]
    bd = tuple(float(v) for v in np.exp(lg * tb))
    return (jnp.asarray(mask, F32), jnp.asarray(qd, F32), jnp.asarray(kd, F32), bd)


def _ret_specs(pin, cos2, sin2, gn_w, layer, tb):
    mask, qd, kd, bd = _ret_tables(tb)
    w = RET_HEADS * RET_DK
    col = lambda off: (lambda i: (i, off // w))
    full3 = lambda i: (0, 0, 0)
    in_specs = [
        pl.BlockSpec((tb, w), col(P_RQ)),
        pl.BlockSpec((tb, w), col(P_RK)),
        pl.BlockSpec((tb, w), col(P_RV)),
        pl.BlockSpec((tb, w), col(P_RG)),
        pl.BlockSpec((tb, RET_DK), lambda i: (i, 0)),
        pl.BlockSpec((tb, RET_DK), lambda i: (i, 0)),
        pl.BlockSpec((RET_HEADS, tb, tb), full3),
        pl.BlockSpec((RET_HEADS, tb, 1), full3),
        pl.BlockSpec((RET_HEADS, tb, 1), full3),
        pl.BlockSpec((None, 1, w), lambda i: (layer, 0, 0)),
    ]
    args = [pin, pin, pin, pin, cos2, sin2, mask, qd, kd, gn_w]
    out_specs = [pl.BlockSpec((tb, w), lambda i: (i, 0))]
    out_shape = [jax.ShapeDtypeStruct((pin.shape[0], w), BF16)]
    scratch = [pltpu.VMEM((RET_HEADS, RET_DK, RET_DV), F32)]
    return in_specs, args, out_specs, out_shape, scratch, bd


def _ssd_stream(xbc_ref, z_ref, dt_ref, dtb_ref, aneg_ref, cw_ref, cb_ref, d_ref, nw_ref,
                o_ref, carry_ref, st_ref):
    tb = xbc_ref.shape[0]
    P, N = SSM_HEADDIM, SSM_STATE
    nx = SSM_HEADS * P

    @pl.when(pl.program_id(0) == 0)
    def _():
        carry_ref[...] = jnp.zeros_like(carry_ref)
        st_ref[...] = jnp.zeros_like(st_ref)

    x = xbc_ref[...].astype(F32)
    carry = carry_ref[...]
    acc = x * cw_ref[SSM_CONV - 1:SSM_CONV, :] + cb_ref[...]
    for s in range(1, SSM_CONV):
        acc = acc + _shift_rows(x, carry, s) * cw_ref[SSM_CONV - 1 - s:SSM_CONV - s, :]
        yield
    carry_ref[...] = x[tb - SUBLANES:, :]
    xbc = _silu(acc)
    yield

    dt = _softplus(dt_ref[...] + dtb_ref[...])
    adt = dt * aneg_ref[...]
    row = lax.broadcasted_iota(jnp.int32, (tb, tb), 0)
    colm = lax.broadcasted_iota(jnp.int32, (tb, tb), 1)
    causal = colm <= row
    tril = causal.astype(F32)
    a_col = jnp.dot(tril, adt, preferred_element_type=F32, precision=lax.Precision.HIGHEST)
    triu = (row <= colm).astype(F32)
    a_row = lax.dot_general(adt, triu, (((0,), (0,)), ((), ())),
                            preferred_element_type=F32, precision=lax.Precision.HIGHEST)
    a_end = a_col[tb - 1:tb, :]
    yield
    lane = lax.broadcasted_iota(jnp.int32, (tb, 2 * P), 1)
    first = lane < P

    groups = range(SSM_GROUPS)
    heads = range(SSM_HEADS)
    hpg = SSM_HEADS // SSM_GROUPS
    npair = SSM_HEADS // 2
    bm = [xbc[:, nx + g * N: nx + (g + 1) * N] for g in groups]
    cm = [xbc[:, nx + SSM_GROUPS * N + g * N: nx + SSM_GROUPS * N + (g + 1) * N] for g in groups]
    cb = [_dot_nt(cm[g].astype(BF16), bm[g].astype(BF16)) for g in groups]
    xp = [xbc[:, 2 * q * P:(2 * q + 2) * P] for q in range(npair)]
    xdt = [(xp[q] * jnp.where(first, dt[:, 2 * q:2 * q + 1], dt[:, 2 * q + 1:2 * q + 2])).astype(BF16)
           for q in range(npair)]
    st = [st_ref[q] for q in range(npair)]
    ac = [a_col[:, h:h + 1] for h in heads]
    yield
    wts = []
    for h in heads:
        dec = jnp.exp(jnp.where(causal, ac[h] - a_row[h:h + 1, :], -jnp.inf))
        wts.append((cb[h // hpg] * dec).astype(BF16))
        if h % 2:
            yield
    y = [_dot(wts[h], xdt[h // 2]) for h in heads]
    yield
    cdec = [(cm[h // hpg] * jnp.exp(ac[h])).astype(BF16) for h in heads]
    y = [y[h] + _dot(cdec[h], st[h // 2].astype(BF16)) for h in heads]
    yield
    bdec = [(bm[h // hpg] * jnp.exp(a_end[:, h:h + 1] - ac[h])).astype(BF16) for h in heads]
    snew = [jnp.exp(a_end[:, h:h + 1]) * st[h // 2] + _dot_tn(bdec[h], xdt[h // 2]) for h in heads]
    yield
    lane_s = lax.broadcasted_iota(jnp.int32, (N, 2 * P), 1)
    for q in range(npair):
        st_ref[q] = jnp.where(lane_s < P, snew[2 * q], snew[2 * q + 1])
    ypair = [jnp.where(first, y[2 * q], y[2 * q + 1])
             + xp[q] * jnp.where(first, d_ref[:, 2 * q:2 * q + 1], d_ref[:, 2 * q + 1:2 * q + 2])
             for q in range(npair)]
    yield
    for g in groups:
        yg = jnp.concatenate(ypair[g * hpg // 2:(g + 1) * hpg // 2], axis=1)
        gsl = slice(g * hpg * P, (g + 1) * hpg * P)
        yg = yg * _silu(z_ref[:, gsl].astype(F32))
        ms = jnp.mean(yg * yg, axis=-1, keepdims=True)
        o_ref[:, gsl] = (yg * lax.rsqrt(ms + RMS_EPS) * nw_ref[:, gsl]).astype(BF16)
        yield


def _ssd_specs(pin, tail, dt_bias, a_neg, conv_w, conv_b, d_skip, norm_w, layer, tb):
    nx = SSM_HEADS * SSM_HEADDIM
    H = SSM_HEADS
    lsel = lambda i: (layer, 0, 0)
    in_specs = [
        pl.BlockSpec((tb, SSM_XBC), lambda i: (i, P_SXBC // SSM_XBC)),
        pl.BlockSpec((tb, nx), lambda i: (i, P_SZ // nx)),
        pl.BlockSpec((tb, LANES), lambda i: (i, T_DT // LANES)),
        pl.BlockSpec((None, 1, LANES), lsel),
        pl.BlockSpec((None, 1, LANES), lsel),
        pl.BlockSpec((None, SSM_CONV, SSM_XBC), lsel),
        pl.BlockSpec((None, 1, SSM_XBC), lsel),
        pl.BlockSpec((None, 1, H), lsel),
        pl.BlockSpec((None, 1, nx), lsel),
    ]
    args = [pin, pin, tail, dt_bias, a_neg, conv_w, conv_b, d_skip, norm_w]
    out_specs = [pl.BlockSpec((tb, nx), lambda i: (i, 0))]
    out_shape = [jax.ShapeDtypeStruct((pin.shape[0], nx), BF16)]
    scratch = [pltpu.VMEM((SUBLANES, SSM_XBC), F32), pltpu.VMEM((H // 2, SSM_STATE, 2 * SSM_HEADDIM), F32)]
    return in_specs, args, out_specs, out_shape, scratch


MLA_PAD = 256
MLA_VX = MLA_V + 16


def _rope_pad(x, cosp, sinp):
    half = MLA_ROPE // 2
    return x * cosp + (pltpu.roll(x, half, 1) + pltpu.roll(x, LANES - half, 1)) * sinp


def _mla_stream(cq_ref, ckv_ref, kr_ref, qaw_ref, wqb_ref, kvaw_ref, wkvb_ref, qnw_ref,
                knw_n_ref, knw_r_ref, cos_ref, sin_ref, q_out, k_out, v_out):
    cosp = cos_ref[...]
    sinp = sin_ref[...]
    cq = cq_ref[...].astype(F32)
    ms = jnp.mean(cq * cq, axis=-1, keepdims=True)
    cqn = (cq * lax.rsqrt(ms + RMS_EPS) * qaw_ref[...]).astype(BF16)
    q = _dot(cqn, wqb_ref[...])
    yield
    ckv = ckv_ref[...].astype(F32)
    ms = jnp.mean(ckv * ckv, axis=-1, keepdims=True)
    ckvn = (ckv * lax.rsqrt(ms + RMS_EPS) * kvaw_ref[...]).astype(BF16)
    kv = _dot(ckvn, wkvb_ref[...])
    yield
    kr = kr_ref[...]
    ssr = jnp.sum(kr * kr, axis=-1, keepdims=True)
    scale = (MLA_QK ** -0.5) * LOG2E
    heads = range(MLA_HEADS)
    qh = [q[:, h * MLA_PAD:(h + 1) * MLA_PAD] for h in heads]
    kn = [kv[:, h * MLA_PAD: h * MLA_PAD + MLA_NOPE] for h in heads]
    qinv = [lax.rsqrt(jnp.sum(qh[h] * qh[h], axis=-1, keepdims=True) * (1.0 / MLA_QK) + RMS_EPS) for h in heads]
    kinv = [lax.rsqrt((jnp.sum(kn[h] * kn[h], axis=-1, keepdims=True) + ssr) * (1.0 / MLA_QK) + RMS_EPS)
            for h in heads]
    yield
    qh = [qh[h] * qinv[h] * qnw_ref[...] for h in heads]
    qr = [_rope_pad(qh[h][:, MLA_NOPE:], cosp, sinp) for h in heads]
    yield
    krh = [_rope_pad(kr * kinv[h] * knw_r_ref[...], cosp, sinp) for h in heads]
    ones = jnp.ones((MLA_VX - MLA_V, kv.shape[0]), F32)
    yield
    for h in heads:
        q_out[h] = (jnp.concatenate([qh[h][:, :MLA_NOPE], qr[h]], axis=1) * scale).T.astype(BF16)
        if h % 2:
            yield
    for h in heads:
        k_out[h] = jnp.concatenate([kn[h] * kinv[h] * knw_n_ref[...], krh[h]], axis=1).astype(BF16)
    yield
    for h in heads:
        vv = kv[:, h * MLA_PAD + MLA_NOPE:(h + 1) * MLA_PAD]
        v_out[h] = jnp.concatenate([vv.T, ones], axis=0).astype(BF16)
        if h % 2:
            yield


def _mla_specs(pin, tail, qaw, wqb, kvaw, wkvb, qnw, knw_n, knw_r, cosp, sinp, layer, tb):
    S = pin.shape[0]
    H = MLA_HEADS
    lsel = lambda i: (layer, 0, 0)
    in_specs = [
        pl.BlockSpec((tb, MLA_Q_RANK), lambda i: (i, P_CQ // MLA_Q_RANK)),
        pl.BlockSpec((tb, MLA_KV_RANK), lambda i: (i, P_CKV // MLA_KV_RANK)),
        pl.BlockSpec((tb, LANES), lambda i: (i, T_KROPE // LANES)),
        pl.BlockSpec((None, 1, MLA_Q_RANK), lsel),
        pl.BlockSpec((None, MLA_Q_RANK, H * MLA_PAD), lsel),
        pl.BlockSpec((None, 1, MLA_KV_RANK), lsel),
        pl.BlockSpec((None, MLA_KV_RANK, H * MLA_PAD), lsel),
        pl.BlockSpec((None, 1, MLA_PAD), lsel),
        pl.BlockSpec((None, 1, MLA_NOPE), lsel),
        pl.BlockSpec((None, 1, LANES), lsel),
        pl.BlockSpec((tb, LANES), lambda i: (i, 0)),
        pl.BlockSpec((tb, LANES), lambda i: (i, 0)),
    ]
    args = [pin, pin, tail, qaw, wqb, kvaw, wkvb, qnw, knw_n, knw_r, cosp, sinp]
    out_specs = [
        pl.BlockSpec((H, MLA_PAD, tb), lambda i: (0, 0, i)),
        pl.BlockSpec((H, tb, MLA_PAD), lambda i: (0, i, 0)),
        pl.BlockSpec((H, MLA_VX, tb), lambda i: (0, 0, i)),
    ]
    out_shape = [
        jax.ShapeDtypeStruct((H, MLA_PAD, S), BF16),
        jax.ShapeDtypeStruct((H, S, MLA_PAD), BF16),
        jax.ShapeDtypeStruct((H, MLA_VX, S), BF16),
    ]
    return in_specs, args, out_specs, out_shape, []


def _flash_body(it_ref, jt_ref, qt_ref, k_ref, vt_ref, o_ref, m_ref, acc_ref):
    t = pl.program_id(0)
    i = it_ref[t]
    j = jt_ref[t]
    tq = qt_ref.shape[2]
    tk = k_ref.shape[1]
    full = (j + 1) * tk <= i * tq
    last = j == ((i + 1) * tq - 1) // tk
    H = MLA_HEADS

    @pl.when(j == 0)
    def _():
        m_ref[...] = jnp.full_like(m_ref, -jnp.inf)
        acc_ref[...] = jnp.zeros_like(acc_ref)

    def step(diagonal):
        if diagonal:
            kc = (j * tk + lax.broadcasted_iota(jnp.int32, (tk, tq), 0)) // CHUNK
            qc = (i * tq + lax.broadcasted_iota(jnp.int32, (tk, tq), 1)) // CHUNK
            visible = kc <= qc

        def logits(h):
            s = _dot(k_ref[h], qt_ref[h])
            return jnp.where(visible, s, -jnp.inf) if diagonal else s

        def softmax(h, s):
            m_prev = m_ref[h]
            m_new = jnp.maximum(m_prev, jnp.max(s, axis=0, keepdims=True))
            alpha = jnp.exp2(m_prev - m_new)
            m_ref[h] = m_new
            return alpha, jnp.exp2((s - m_new).astype(BF16))

        def accumulate(h, alpha, p):
            acc_ref[h] = alpha * acc_ref[h] + _dot(vt_ref[h], p)

        s_next = logits(0)
        pending = None
        for h in range(H):
            s_cur = s_next
            if h + 1 < H:
                s_next = logits(h + 1)
            if pending is not None:
                accumulate(*pending)
            pending = (h,) + softmax(h, s_cur)
        accumulate(*pending)

    @pl.when(full)
    def _():
        step(False)

    @pl.when(jnp.logical_not(full))
    def _():
        step(True)

    @pl.when(last)
    def _():
        for h in range(H):
            acc = acc_ref[h]
            o_ref[:, h * MLA_V:(h + 1) * MLA_V] = (acc[:MLA_V] / acc[MLA_V:MLA_V + 1]).T.astype(BF16)


def _flash(qt, k, vt, tq=1024, tk=1024):
    H, S, _ = k.shape
    pairs = [(i, j) for i in range(S // tq) for j in range(((i + 1) * tq - 1) // tk + 1)]
    it = jnp.asarray([p[0] for p in pairs], jnp.int32)
    jt = jnp.asarray([p[1] for p in pairs], jnp.int32)
    grid_spec = pltpu.PrefetchScalarGridSpec(
        num_scalar_prefetch=2,
        grid=(len(pairs),),
        in_specs=[
            pl.BlockSpec((H, MLA_PAD, tq), lambda t, it, jt: (0, 0, it[t])),
            pl.BlockSpec((H, tk, MLA_PAD), lambda t, it, jt: (0, jt[t], 0)),
            pl.BlockSpec((H, MLA_VX, tk), lambda t, it, jt: (0, 0, jt[t])),
        ],
        out_specs=pl.BlockSpec((tq, H * MLA_V), lambda t, it, jt: (it[t], 0)),
        scratch_shapes=[pltpu.VMEM((H, 1, tq), F32), pltpu.VMEM((H, MLA_VX, tq), F32)],
    )
    return pl.pallas_call(
        _flash_body,
        grid_spec=grid_spec,
        out_shape=jax.ShapeDtypeStruct((S, H * MLA_V), BF16),
        compiler_params=_cparams(("arbitrary",)),
        name="mla_flash",
    )(it, jt, qt, k, vt)


def _rwkv_body(*refs, mix_v, companion):
    if mix_v:
        (r_ref, k_ref, v_ref, wl_ref, al_ref, gl_ref, mur, muk, muv, muwl, mual, mugl, w0, a0, w2, a2, g2,
         kk_ref, ka_ref, lnw_ref, lnb_ref, rk_ref, vf_ref, v0, v1, v2,
         o_ref, c_r, c_k, c_v, c_wl, c_al, c_gl, s_ref) = refs
    else:
        (r_ref, k_ref, v_ref, wl_ref, al_ref, gl_ref, mur, muk, muv, muwl, mual, mugl, w0, a0, w2, a2, g2,
         kk_ref, ka_ref, lnw_ref, lnb_ref, rk_ref,
         o_ref, vfirst_ref, c_r, c_k, c_v, c_wl, c_al, c_gl, s_ref) = refs
    tb = r_ref.shape[0]
    L = RWKV_L
    N = RWKV_HEAD
    L2 = 2 * L
    npair = RWKV_W // LANES
    half = tb // 2
    carries = (c_r, c_k, c_v, c_wl, c_al, c_gl)

    @pl.when(pl.program_id(0) == 0)
    def _():
        for c in carries:
            c[...] = jnp.zeros_like(c)
        s_ref[...] = jnp.zeros_like(s_ref)

    def mixed(x_ref, c_ref, mu_ref):
        x = x_ref[...].astype(F32)
        prev = _shift_rows(x, c_ref[...], 1)
        c_ref[...] = x[tb - SUBLANES:, :]
        return x + (prev - x) * mu_ref[...]

    r = mixed(r_ref, c_r, mur)
    k = mixed(k_ref, c_k, muk)
    v = mixed(v_ref, c_v, muv)
    wl = mixed(wl_ref, c_wl, muwl)
    al = mixed(al_ref, c_al, mual)
    gl = mixed(gl_ref, c_gl, mugl)
    if not mix_v:
        vfirst_ref[...] = v

    lane = lax.broadcasted_iota(jnp.int32, (L, LANES), 1)
    m0 = lane < N
    r2 = lax.broadcasted_iota(jnp.int32, (L2, L2), 0)
    c2 = lax.broadcasted_iota(jnp.int32, (L2, L2), 1)
    same = (r2 // L) == (c2 // L)
    strict = same & (c2 < r2)
    incl = same & (c2 <= r2)
    eye = (r2 == c2).astype(F32)
    ones_bd = ((r2 // N) == (c2 // N)).astype(BF16)
    rb = lax.broadcasted_iota(jnp.int32, (half, half), 0)
    cb = lax.broadcasted_iota(jnp.int32, (half, half), 1)
    tril_bd = ((rb // L) == (cb // L)) & (cb <= rb)

    def stack(x):
        return jnp.concatenate([jnp.where(m0, x, 0.0), jnp.where(m0, 0.0, x)], axis=0).astype(BF16)

    probs = [(c, p) for c in range(half // L) for p in range(npair)]
    rows_of = lambda c: slice(c * L, (c + 1) * L)
    lanes_of = lambda p: slice(p * LANES, (p + 1) * LANES)
    opnds = [{}, {}]
    gates = [None, None]
    states = [s_ref[p] for p in range(npair)]

    def front(hf):
        rs = slice(hf * half, (hf + 1) * half)
        w_raw = w0[...] + _mm3(jnp.tanh(wl[rs]), w2[...])
        lw = -jnp.exp(-_softplus(-w_raw) - 0.5)
        yield
        a_sig = _sigmoid(a0[...] + _mm3(al[rs], a2[...]))
        gates[hf] = _mm1(_sigmoid(gl[rs]), g2[...])
        yield
        vh = v[rs]
        if mix_v:
            lora = _mm1(_mm1(vh, v1[...]), v2[...])
            vh = vh + (vf_ref[rs, :] - vh) * _sigmoid(v0[...] + lora)
        yield
        kk = k[rs] * kk_ref[...]
        parts = []
        for p in range(npair):
            kp = kk[:, lanes_of(p)]
            n2 = _seg_sum(kp * kp, ones_bd)
            parts.append(kp / jnp.maximum(jnp.sqrt(n2), 1e-12))
            if p % 2:
                yield
        kk = jnp.concatenate(parts, axis=1)
        kh = k[rs] * (1.0 + (a_sig - 1.0) * ka_ref[...])
        ah = -kk
        bh = kk * a_sig
        rh = r[rs]
        cum_blk = _mm_exact_lhs(tril_bd, lw)
        yield
        for n, (c, p) in enumerate(probs):
            rows, sl = rows_of(c), lanes_of(p)
            cum = cum_blk[rows, sl]
            cum_end = cum[L - 1:L, :]
            rr = rh[rows, sl]
            kc = kh[rows, sl]
            vc = vh[rows, sl]
            e_neg = jnp.exp(-cum)
            e_end = jnp.exp(cum_end - cum)
            re = rr * jnp.exp(cum)
            xr_f = jnp.concatenate([jnp.where(m0, re, 0.0), jnp.where(m0, 0.0, re)], axis=0)
            opnds[hf][c, p] = dict(
                xr_f=xr_f, xa=stack(ah[rows, sl] * jnp.exp(cum - lw[rows, sl])), xr=xr_f.astype(BF16),
                xb=stack(bh[rows, sl] * e_neg), xk=stack(kc * e_neg), xbh=stack(bh[rows, sl] * e_end),
                xkh=stack(kc * e_end), vs=stack(vc), decay_end=jnp.exp(cum_end),
                rk=rr * kc * rk_ref[:, sl], v=vc)
            if n % 2:
                yield

    def back(hf):
        opnd = opnds[hf]
        amat = {}
        for cp in probs:
            o = opnd[cp]
            pmat = _dot_nt(jnp.concatenate([o["xa"], o["xr"]], axis=0),
                           jnp.concatenate([o["xb"], o["xk"]], axis=0))
            amat[cp] = dict(
                ab=jnp.where(strict, pmat[:L2, :L2], 0.0),
                ak=jnp.where(strict, pmat[:L2, L2:], 0.0).astype(BF16),
                rb=jnp.where(incl, pmat[L2:, :L2], 0.0).astype(BF16),
                rk=jnp.where(incl, pmat[L2:, L2:], 0.0).astype(BF16))
        yield
        tinv = {cp: eye + amat[cp]["ab"] for cp in probs}
        pw = {cp: amat[cp]["ab"] for cp in probs}
        for _ in range(int(math.log2(L)) - 1):
            for cp in probs:
                pwb = pw[cp].astype(BF16)
                pw[cp] = _dot(pwb, pwb)
            yield
            for cp in probs:
                tinv[cp] = tinv[cp] + _mm1(tinv[cp], pw[cp])
            yield
        w1 = {cp: _dot(amat[cp]["ak"], opnd[cp]["vs"]) for cp in probs}
        yield
        mub = {}
        for cp in probs:
            rhs = jnp.concatenate([opnd[cp]["xa"], w1[cp].astype(BF16)], axis=1)
            mub[cp] = _dot(tinv[cp].astype(BF16), rhs).astype(BF16)
        yield
        coef = {}
        for cp in probs:
            o = opnd[cp]
            ry = _dot(amat[cp]["rb"], mub[cp])
            gh = _dot_tn(o["xbh"], mub[cp])
            coef[cp] = dict(
                m_r=(o["xr_f"] + ry[:, :LANES]).astype(BF16),
                y1=ry[:, LANES:] + _dot(amat[cp]["rk"], o["vs"]),
                g=(eye * o["decay_end"] + gh[:, :LANES]).astype(BF16),
                h=gh[:, LANES:] + _dot_tn(o["xkh"], o["vs"]))
        yield
        ys = {}
        for (c, p) in probs:
            st = states[p].astype(BF16)
            cf = coef[c, p]
            yst = _dot(cf["m_r"], st) + cf["y1"]
            states[p] = _dot(cf["g"], st) + cf["h"]
            ys[c, p] = yst[:L] + yst[L:]
        yield
        yc = {cp: ys[cp] - _seg_sum(ys[cp], ones_bd) * (1.0 / N) for cp in probs}
        yield
        bonus = {cp: _seg_sum(opnd[cp]["rk"], ones_bd) * opnd[cp]["v"] for cp in probs}
        var = {cp: _seg_sum(yc[cp] * yc[cp], ones_bd) * (1.0 / N) for cp in probs}
        yield
        for (c, p) in probs:
            rows, sl = rows_of(c), lanes_of(p)
            out_rows = slice(hf * half + c * L, hf * half + (c + 1) * L)
            yn = yc[c, p] * lax.rsqrt(var[c, p] + RWKV_GN_EPS) * lnw_ref[:, sl] + lnb_ref[:, sl]
            o_ref[out_rows, sl] = ((yn + bonus[c, p]) * gates[hf][rows, sl]).astype(BF16)

    _interleave(front(0), front(1))
    _interleave(_roundrobin(back(0), back(1)), companion)
    for p in range(npair):
        s_ref[p] = states[p]


def _rwkv_specs(pin, tail, vec, mats, vfirst, vmix, layer, tb):
    S = pin.shape[0]
    W = RWKV_W
    lsel = lambda i: (layer, 0, 0)
    lsel1 = lambda i: (layer - 1, 0, 0)
    in_specs = [
        pl.BlockSpec((tb, W), lambda i: (i, P_WR // W)),
        pl.BlockSpec((tb, W), lambda i: (i, P_WK // W)),
        pl.BlockSpec((tb, W), lambda i: (i, P_WV // W)),
        pl.BlockSpec((tb, LANES), lambda i: (i, T_WL // LANES)),
        pl.BlockSpec((tb, LANES), lambda i: (i, T_AL // LANES)),
        pl.BlockSpec((tb, LANES), lambda i: (i, T_GL // LANES)),
    ]
    args = [pin] * 3 + [tail] * 3
    for name, width in (("mu_r", W), ("mu_k", W), ("mu_v", W), ("mu_wl", LANES), ("mu_al", LANES),
                        ("mu_gl", LANES), ("w0", W), ("a0", W)):
        in_specs.append(pl.BlockSpec((None, 1, width), lsel))
        args.append(vec[name])
    for name in ("w2", "a2", "g2"):
        in_specs.append(pl.BlockSpec((None, LANES, W), lsel))
        args.append(mats[name])
    for name in ("k_k", "k_a", "ln_w", "ln_b", "r_k"):
        in_specs.append(pl.BlockSpec((None, 1, W), lsel))
        args.append(vec[name])
    mix_v = vmix is not None
    if mix_v:
        v0, v1, v2 = vmix
        in_specs += [pl.BlockSpec((tb, W), lambda i: (i, 0)),
                     pl.BlockSpec((None, 1, W), lsel1),
                     pl.BlockSpec((None, W, LANES), lsel1),
                     pl.BlockSpec((None, LANES, W), lsel1)]
        args += [vfirst, v0, v1, v2]
    out_spec = pl.BlockSpec((tb, W), lambda i: (i, 0))
    out_specs = [out_spec] if mix_v else [out_spec, out_spec]
    out_shape = [jax.ShapeDtypeStruct((S, W), BF16)]
    if not mix_v:
        out_shape.append(jax.ShapeDtypeStruct((S, W), F32))
    scratch = ([pltpu.VMEM((SUBLANES, W), F32)] * 3 + [pltpu.VMEM((SUBLANES, LANES), F32)] * 3
               + [pltpu.VMEM((W // LANES, LANES, LANES), F32)])
    return in_specs, args, out_specs, out_shape, scratch


def _mixers_body(*refs, counts, mix_v, block_decay):
    refs = list(refs)
    take = lambda n: [refs.pop(0) for _ in range(n)]
    ins = [take(n) for n in counts["in"]]
    outs = [take(n) for n in counts["out"]]
    scr = [take(n) for n in counts["scratch"]]
    ret, ssd, mla, rwkv = (ins[m] + outs[m] + scr[m] for m in range(4))
    companion = _chain(_mla_stream(*mla), _ret_stream(*ret, block_decay=block_decay), _ssd_stream(*ssd))
    _rwkv_body(*rwkv, mix_v=mix_v, companion=companion)


def _mixers(pin, tail, cos_ret, sin_ret, gn_w, ssd_params, mla_params, cos_mla, sin_mla, vec, mats, vfirst, vmix,
            layer, tb=ROW_BLOCK):
    S = pin.shape[0]
    r_in, r_args, r_out, r_shape, r_scr, bd = _ret_specs(pin, cos_ret, sin_ret, gn_w, layer, tb)
    parts = [
        (r_in, r_args, r_out, r_shape, r_scr),
        _ssd_specs(pin, tail, *ssd_params, layer, tb),
        _mla_specs(pin, tail, *mla_params, cos_mla, sin_mla, layer, tb),
        _rwkv_specs(pin, tail, vec, mats, vfirst, vmix, layer, tb),
    ]
    counts = {"in": [len(p[0]) for p in parts], "out": [len(p[2]) for p in parts],
              "scratch": [len(p[4]) for p in parts]}
    outs = pl.pallas_call(
        functools.partial(_mixers_body, counts=counts, mix_v=vmix is not None, block_decay=bd),
        grid=(S // tb,),
        in_specs=[sp for p in parts for sp in p[0]],
        out_specs=[sp for p in parts for sp in p[2]],
        out_shape=[sh for p in parts for sh in p[3]],
        scratch_shapes=[sc for p in parts for sc in p[4]],
        compiler_params=_cparams(("arbitrary",)),
        name="mixers",
    )(*[a for p in parts for a in p[1]])
    o_ret, o_ssd, qt, k, vt, o_rwkv = outs[:6]
    return o_ret, o_ssd, (qt, k, vt), o_rwkv, (outs[6] if vmix is None else None)


def _merge_body(oa_ref, ob_ref, oc_ref, od_ref, g0_ref, g1_ref, g2_ref, g3_ref, wb_ref, out_ref):
    o_refs = (oa_ref, ob_ref, oc_ref, od_ref)
    g_refs = (g0_ref, g1_ref, g2_ref, g3_ref)
    acc = None
    for n in range(N_BRANCH):
        contrib = _sigmoid(g_refs[n][...].astype(F32)) * _dot(o_refs[n][...], wb_ref[n])
        acc = contrib if acc is None else acc + contrib
    out_ref[...] = acc.astype(BF16)


def _merge(o_a, o_b, o_c, o_d, pin, w_branch, layer, tm=512):
    S = pin.shape[0]
    D = D_MODEL
    ospec = pl.BlockSpec((tm, BRANCH_W), lambda i: (i, 0))
    gspec = lambda n: pl.BlockSpec((tm, D), lambda i: (i, n))
    return pl.pallas_call(
        _merge_body,
        grid=(S // tm,),
        in_specs=[ospec, ospec, ospec, ospec, gspec(0), gspec(1), gspec(2), gspec(3),
                  pl.BlockSpec((None, N_BRANCH, BRANCH_W, D), lambda i: (layer, 0, 0, 0))],
        out_specs=pl.BlockSpec((tm, D), lambda i: (i, 0)),
        out_shape=jax.ShapeDtypeStruct((S, D), BF16),
        compiler_params=_cparams(("arbitrary",)),
        name="merge",
    )(o_a, o_b, o_c, o_d, pin, pin, pin, pin, w_branch)


def _outproj_body(m_ref, w_ref, x_ref, o_ref):
    o_ref[...] = x_ref[...] + _dot(m_ref[...], w_ref[...])


def _outproj(merged, w_out, x, layer, tm=512, tn=2048):
    S, D = x.shape
    return pl.pallas_call(
        _outproj_body,
        grid=(S // tm, D // tn),
        in_specs=[
            pl.BlockSpec((tm, D), lambda i, j: (i, 0)),
            pl.BlockSpec((None, D, tn), lambda i, j: (layer, 0, j)),
            pl.BlockSpec((tm, tn), lambda i, j: (i, j)),
        ],
        out_specs=pl.BlockSpec((tm, tn), lambda i, j: (i, j)),
        out_shape=jax.ShapeDtypeStruct((S, D), F32),
        compiler_params=_cparams(("arbitrary", "arbitrary")),
        name="outproj",
    )(merged, w_out, x)


def _ffn_body(x_ref, nw_ref, wg_ref, wu_ref, wd_ref, out_ref, h_ref):
    @pl.when(pl.program_id(1) == 0)
    def _():
        x = x_ref[...]
        ms = jnp.mean(x * x, axis=-1, keepdims=True)
        h_ref[...] = (x * lax.rsqrt(ms + RMS_EPS) * nw_ref[...]).astype(BF16)
        out_ref[...] = x

    h = h_ref[...]
    act = (_silu(_dot(h, wg_ref[...])) * _dot(h, wu_ref[...])).astype(BF16)
    out_ref[...] += _dot(act, wd_ref[...])


def _ffn(x, norm_w, w_gu, w_down, layer, tm=1024, tf=512):
    S, D = x.shape
    nf = D_FF // tf
    return pl.pallas_call(
        _ffn_body,
        grid=(S // tm, nf),
        in_specs=[
            pl.BlockSpec((tm, D), lambda i, f: (i, 0)),
            pl.BlockSpec((None, 1, D), lambda i, f: (layer, 0, 0)),
            pl.BlockSpec((None, D, tf), lambda i, f: (layer, 0, f)),
            pl.BlockSpec((None, D, tf), lambda i, f: (layer, 0, f + nf)),
            pl.BlockSpec((None, tf, D), lambda i, f: (layer, f, 0)),
        ],
        out_specs=pl.BlockSpec((tm, D), lambda i, f: (i, 0)),
        out_shape=jax.ShapeDtypeStruct((S, D), F32),
        scratch_shapes=[pltpu.VMEM((tm, D), BF16)],
        compiler_params=_cparams(("arbitrary", "arbitrary")),
        name="ffn",
    )(x, norm_w, w_gu, w_gu, w_down)


def _pad_last(a, width):
    pad = width - a.shape[-1]
    if pad == 0:
        return a
    return jnp.pad(a, [(0, 0)] * (a.ndim - 1) + [(0, pad)])


def _pad_axis(a, axis, width):
    pad = width - a.shape[axis]
    if pad == 0:
        return a
    cfg = [(0, 0)] * a.ndim
    cfg[axis] = (0, pad)
    return jnp.pad(a, cfg)


_MAIN_PIECES = (
    (_O_GATE, P_GATE, N_BRANCH * D_MODEL),
    (_O_RQ, P_RQ, 512), (_O_RK, P_RK, 512), (_O_RV, P_RV, 512), (_O_RG, P_RG, 512),
    (_O_SXBC, P_SXBC, SSM_XBC), (_O_SZ, P_SZ, 512),
    (_O_CQ, P_CQ, MLA_Q_RANK),
    (_O_RW, P_WR, RWKV_W), (_O_RW + RWKV_W, P_WK, RWKV_W), (_O_RW + 2 * RWKV_W, P_WV, RWKV_W),
    (_O_CKV, P_CKV, MLA_KV_RANK),
)
_TAIL_PIECES = (
    (_O_SDT, T_DT, SSM_HEADS),
    (_O_KROPE, T_KROPE, MLA_ROPE),
    (_O_RW + 3 * RWKV_W, T_WL, RWKV_W_LORA),
    (_O_RW + 3 * RWKV_W + RWKV_W_LORA, T_AL, RWKV_A_LORA),
    (_O_RW + 3 * RWKV_W + RWKV_W_LORA + RWKV_A_LORA, T_GL, RWKV_G_LORA),
)
PACK_ROWS = 512


def _pack_tables(pieces, total, tile):
    src_of, valid_of = [], []
    for t in range(total // tile):
        lo = t * tile
        src, valid = 0, 0
        for s0, d0, width in pieces:
            if d0 <= lo < d0 + width:
                src, valid = s0 + (lo - d0), min(tile, d0 + width - lo)
        assert src % SUBLANES == 0
        src_of.append(src // SUBLANES)
        valid_of.append(valid)
    return jnp.asarray(src_of, jnp.int32), jnp.asarray(valid_of, jnp.int32)


def _pack_body(src_ref, valid_ref, w_ref, o_ref):
    w = w_ref[0]
    rows = lax.broadcasted_iota(jnp.int32, w.shape, 0)
    keep = rows < valid_ref[pl.program_id(1)]
    o_ref[...] = jnp.where(keep, w, 0.0).astype(BF16)


def _pack_rows(w_t, pieces, total, tile, name):
    depth, _, d = w_t.shape
    src_of, valid_of = _pack_tables(pieces, total, tile)
    grid_spec = pltpu.PrefetchScalarGridSpec(
        num_scalar_prefetch=2,
        grid=(depth, total // tile),
        in_specs=[pl.BlockSpec((pl.Element(1), pl.Element(tile), pl.Element(d)),
                               lambda l, t, src, valid: (l, src[t] * SUBLANES, 0))],
        out_specs=pl.BlockSpec((None, tile, d), lambda l, t, src, valid: (l, t, 0)),
    )
    return pl.pallas_call(
        _pack_body,
        grid_spec=grid_spec,
        out_shape=jax.ShapeDtypeStruct((depth, total, d), BF16),
        compiler_params=_cparams(("arbitrary", "arbitrary")),
        name=name,
    )(src_of, valid_of, w_t)


def _pack_w_in(w_in):
    w_t = jnp.swapaxes(w_in, 1, 2)
    w_main_t = _pack_rows(w_t, _MAIN_PIECES, N_MAIN, PACK_ROWS, "pack_w_main")
    w_tail_t = _pack_rows(w_t, _TAIL_PIECES, N_TAIL, LANES, "pack_w_tail")
    return w_main_t, w_tail_t


def _rope_tables(positions, dim):
    inv = 1.0 / (ROPE_THETA ** (jnp.arange(0, dim, 2, dtype=F32) / dim))
    ang = positions.astype(F32)[:, None] * inv
    return jnp.cos(ang), jnp.sin(ang)


def kernel(x, positions, norm1_w, w_in, ret_gn_w, ssm_conv_w, ssm_conv_b, ssm_dt_bias, ssm_a_log, ssm_d, ssm_norm_w, mla_q_a_norm_w, mla_w_qb, mla_kv_a_norm_w, mla_w_kvb, mla_q_norm_w, mla_k_norm_w, rwkv_mu, rwkv_w0, rwkv_w2, rwkv_a0, rwkv_a2, rwkv_g2, rwkv_v0, rwkv_v1, rwkv_v2, rwkv_k_k, rwkv_k_a, rwkv_r_k, rwkv_ln_w, rwkv_ln_b, w_branch, w_out, norm2_w, ffn_w_gu, ffn_w_down):
    B, S, D = x.shape
    assert B == 1 and D == D_MODEL and S % 1024 == 0
    xs = x[0]
    pos = positions[0]

    c_r, s_r = _rope_tables(pos, RET_DK)
    cos_ret = jnp.concatenate([c_r, c_r], axis=-1)
    sin_ret = jnp.concatenate([-s_r, s_r], axis=-1)
    c_m, s_m = _rope_tables(pos, MLA_ROPE)
    cos_mla = _pad_last(jnp.concatenate([c_m, c_m], axis=-1), LANES)
    sin_mla = _pad_last(jnp.concatenate([-s_m, s_m], axis=-1), LANES)

    row = lambda a: a[:, None, :]
    w_main, w_tail = _pack_w_in(w_in)
    norm1 = row(norm1_w)
    norm2 = row(norm2_w)
    gn_w = row(ret_gn_w)
    conv_b = row(ssm_conv_b)
    d_skip = row(ssm_d)
    ssm_nw = row(ssm_norm_w)
    dt_bias = row(_pad_last(ssm_dt_bias, LANES))
    a_neg = row(_pad_last(-jnp.exp(ssm_a_log.astype(F32)), LANES))
    qaw = row(mla_q_a_norm_w)
    kvaw = row(mla_kv_a_norm_w)
    wqb = _pad_last(mla_w_qb.reshape(DEPTH, MLA_Q_RANK, MLA_HEADS, MLA_QK), MLA_PAD)
    wqb = wqb.reshape(DEPTH, MLA_Q_RANK, MLA_HEADS * MLA_PAD).astype(BF16)
    wkvb = mla_w_kvb.astype(BF16)
    qnw = row(_pad_last(mla_q_norm_w, MLA_PAD))
    knw_n = row(mla_k_norm_w[:, :MLA_NOPE])
    knw_r = row(_pad_last(mla_k_norm_w[:, MLA_NOPE:], LANES))
    W = RWKV_W
    mu = rwkv_mu
    vec = {
        "mu_r": row(mu[:, :W]), "mu_k": row(mu[:, W:2 * W]), "mu_v": row(mu[:, 2 * W:3 * W]),
        "mu_wl": row(_pad_last(mu[:, 3 * W:3 * W + RWKV_W_LORA], LANES)),
        "mu_al": row(_pad_last(mu[:, 3 * W + RWKV_W_LORA:3 * W + RWKV_W_LORA + RWKV_A_LORA], LANES)),
        "mu_gl": row(mu[:, 3 * W + RWKV_W_LORA + RWKV_A_LORA:]),
        "w0": row(rwkv_w0), "a0": row(rwkv_a0), "k_k": row(rwkv_k_k), "k_a": row(rwkv_k_a),
        "ln_w": row(rwkv_ln_w), "ln_b": row(rwkv_ln_b), "r_k": row(rwkv_r_k.reshape(DEPTH, W)),
    }
    mats = {"w2": _pad_axis(rwkv_w2, 1, LANES), "a2": _pad_axis(rwkv_a2, 1, LANES), "g2": rwkv_g2}
    v0 = row(rwkv_v0)
    v1 = _pad_last(rwkv_v1, LANES)
    v2 = _pad_axis(rwkv_v2, 1, LANES)
    wb = w_branch.astype(BF16)
    wo = w_out.astype(BF16)
    wgu = ffn_w_gu.astype(BF16)
    wdn = ffn_w_down.astype(BF16)

    v_first = None
    for l in range(DEPTH):
        pin, tail = _inproj(xs, norm1, w_main, w_tail, l)
        vmix = None if l == 0 else (v0, v1, v2)
        o_a, o_b, qkv, o_d, vf = _mixers(
            pin, tail, cos_ret, sin_ret, gn_w, (dt_bias, a_neg, ssm_conv_w, conv_b, d_skip, ssm_nw),
            (qaw, wqb, kvaw, wkvb, qnw, knw_n, knw_r), cos_mla, sin_mla, vec, mats, v_first, vmix, l)
        o_c = _flash(*qkv)
        if l == 0:
            v_first = vf
        merged = _merge(o_a, o_b, o_c, o_d, pin, wb, l)
        xs = _outproj(merged, wo, xs, l)
        xs = _ffn(xs, norm2, wgu, wdn, l)
    return xs[None]
```

```python
import functools
import math

import numpy as np
import jax
import jax.numpy as jnp
from jax import lax
from jax.experimental import pallas as pl
from jax.experimental.pallas import tpu as pltpu

F32 = jnp.float32
BF16 = jnp.bfloat16

D_MODEL = 2048
DEPTH = 4
CHUNK = 64
N_BRANCH = 4
BRANCH_W = 512
RMS_EPS = 1e-6
GN_EPS = 1e-5
ROPE_THETA = 10000.0
RET_HEADS, RET_DK, RET_DV = 4, 128, 128
SSM_HEADS, SSM_HEADDIM, SSM_GROUPS, SSM_STATE, SSM_CONV = 8, 64, 2, 128, 4
SSM_XBC = SSM_HEADS * SSM_HEADDIM + 2 * SSM_GROUPS * SSM_STATE
MLA_HEADS, MLA_Q_RANK, MLA_KV_RANK, MLA_NOPE, MLA_ROPE, MLA_V = 4, 512, 256, 128, 64, 128
MLA_QK = MLA_NOPE + MLA_ROPE
RWKV_HEADS, RWKV_HEAD = 8, 64
RWKV_W = RWKV_HEADS * RWKV_HEAD
RWKV_W_LORA, RWKV_A_LORA, RWKV_V_LORA, RWKV_G_LORA = 64, 64, 32, 128
RWKV_GN_EPS = 64e-5
D_FF = 5632

LANES = 128
SUBLANES = 8
VMEM_LIMIT_BYTES = 56 * 1024 * 1024

_O_RQ, _O_RK, _O_RV, _O_RG = 0, 512, 1024, 1536
_O_SZ, _O_SXBC, _O_SDT = 2048, 2560, 3584
_O_CQ, _O_CKV, _O_KROPE = 3592, 4104, 4360
_O_RW = 4424
_O_GATE = 6216
P_GATE = 0
P_RQ, P_RK, P_RV, P_RG = 8192, 8704, 9216, 9728
P_SXBC, P_SZ = 10240, 11264
P_CQ = 11776
P_WR, P_WK, P_WV = 12288, 12800, 13312
P_CKV = 13824
N_MAIN = 14336
T_DT, T_KROPE, T_WL, T_AL, T_GL = 0, 128, 256, 384, 512
N_TAIL = 640

ROW_BLOCK = 256
RWKV_L = 64
LOG2E = 1.4426950408889634


def _cparams(sem):
    return pltpu.CompilerParams(dimension_semantics=sem, vmem_limit_bytes=VMEM_LIMIT_BYTES)


def _dot(a, b):
    return jnp.dot(a, b, preferred_element_type=F32)


def _dot_nt(a, b):
    return lax.dot_general(a, b, (((1,), (1,)), ((), ())), preferred_element_type=F32)


def _dot_tn(a, b):
    return lax.dot_general(a, b, (((0,), (0,)), ((), ())), preferred_element_type=F32)


def _split_bf16(x):
    hi = x.astype(BF16)
    lo = (x - hi.astype(F32)).astype(BF16)
    return hi, lo


def _mm3(a, b, dot=_dot):
    ah, al = _split_bf16(a)
    bh, bl = _split_bf16(b)
    return dot(ah, bh) + dot(ah, bl) + dot(al, bh)


def _mm1(a, b, dot=_dot):
    return dot(a.astype(BF16), b.astype(BF16))


def _mm_exact_lhs(a01, x, dot=_dot):
    xh, xl = _split_bf16(x)
    a = a01.astype(BF16)
    return dot(a, xh) + dot(a, xl)


def _seg_sum(x, ones_bd):
    return _dot(x.astype(BF16), ones_bd)


def _sigmoid(x):
    return 0.5 * jnp.tanh(0.5 * x) + 0.5


def _silu(x):
    return x * _sigmoid(x)


def _softplus(x):
    return jnp.maximum(x, 0.0) + jnp.log(1.0 + jnp.exp(-jnp.abs(x)))


def _shift_rows(x, carry8, s):
    xr = pltpu.roll(x, s, 0)
    pr = pltpu.roll(carry8, s, 0)
    row = lax.broadcasted_iota(jnp.int32, carry8.shape, 0)
    top = jnp.where(row < s, pr, xr[:SUBLANES])
    return jnp.concatenate([top, xr[SUBLANES:]], axis=0)


def _inproj_body(x_ref, nw_ref, w_ref, wt_ref, o_ref, ot_ref, h_ref):
    @pl.when(pl.program_id(1) == 0)
    def _():
        x = x_ref[...]
        ms = jnp.mean(x * x, axis=-1, keepdims=True)
        h = (x * lax.rsqrt(ms + RMS_EPS) * nw_ref[...]).astype(BF16)
        h_ref[...] = h
        ot_ref[...] = _dot_nt(h, wt_ref[...])

    o_ref[...] = _dot_nt(h_ref[...], w_ref[...]).astype(BF16)


def _inproj(x, norm_w, w_main, w_tail, layer, tm=1024, tn=1024):
    S, D = x.shape
    return pl.pallas_call(
        _inproj_body,
        grid=(S // tm, N_MAIN // tn),
        in_specs=[
            pl.BlockSpec((tm, D), lambda i, j: (i, 0)),
            pl.BlockSpec((None, 1, D), lambda i, j: (layer, 0, 0)),
            pl.BlockSpec((None, tn, D), lambda i, j: (layer, j, 0)),
            pl.BlockSpec((None, N_TAIL, D), lambda i, j: (layer, 0, 0)),
        ],
        out_specs=[pl.BlockSpec((tm, tn), lambda i, j: (i, j)),
                   pl.BlockSpec((tm, N_TAIL), lambda i, j: (i, 0))],
        out_shape=[jax.ShapeDtypeStruct((S, N_MAIN), BF16), jax.ShapeDtypeStruct((S, N_TAIL), F32)],
        scratch_shapes=[pltpu.VMEM((tm, D), BF16)],
        compiler_params=_cparams(("arbitrary", "arbitrary")),
        name="inproj",
    )(x, norm_w, w_main, w_tail)


def _roundrobin(*streams):
    active = list(streams)
    while active:
        for g in list(active):
            try:
                next(g)
            except StopIteration:
                active.remove(g)
            else:
                yield


def _chain(*streams):
    for g in streams:
        yield from g


def _interleave(*streams):
    for _ in _roundrobin(*streams):
        pass


def _ret_stream(q_ref, k_ref, v_ref, g_ref, cos_ref, sin_ref, m_ref, qd_ref, kd_ref, gnw_ref,
                o_ref, s_ref, *, block_decay):
    @pl.when(pl.program_id(0) == 0)
    def _():
        s_ref[...] = jnp.zeros_like(s_ref)

    cos = cos_ref[...]
    sin = sin_ref[...]
    heads = range(RET_HEADS)
    lanes = [slice(h * RET_DK, (h + 1) * RET_DK) for h in heads]
    q, k, vb = [], [], []
    for h in heads:
        qh = q_ref[:, lanes[h]].astype(F32)
        kh = k_ref[:, lanes[h]].astype(F32)
        q.append(qh * cos + pltpu.roll(qh, RET_DK // 2, 1) * sin)
        k.append((kh * cos + pltpu.roll(kh, RET_DK // 2, 1) * sin) * (RET_DK ** -0.5))
        vb.append(v_ref[:, lanes[h]])
        if h % 2:
            yield
    sc = [(_dot_nt(q[h].astype(BF16), k[h].astype(BF16)) * m_ref[h]).astype(BF16) for h in heads]
    yield
    st = [s_ref[h] for h in heads]
    o = [_dot(sc[h], vb[h]) + _dot((q[h] * qd_ref[h]).astype(BF16), st[h].astype(BF16)) for h in heads]
    yield
    for h in heads:
        s_ref[h] = block_decay[h] * st[h] + _dot_tn((k[h] * kd_ref[h]).astype(BF16), vb[h])
    yield
    oc = [o[h] - jnp.mean(o[h], axis=-1, keepdims=True) for h in heads]
    var = [jnp.mean(oc[h] * oc[h], axis=-1, keepdims=True) for h in heads]
    yield
    for h in heads:
        on = oc[h] * lax.rsqrt(var[h] + GN_EPS) * gnw_ref[:, lanes[h]]
        o_ref[:, lanes[h]] = (_silu(g_ref[:, lanes[h]].astype(F32)) * on).astype(BF16)
        if h % 2:
            yield


def _ret_tables(tb):
    lg = np.log1p(-np.exp2(-5.0 - np.arange(RET_HEADS, dtype=np.float64)))
    pos = np.arange(tb, dtype=np.float64)
    dist = np.abs(pos[:, None] - pos[None, :])
    visible = (pos[None, :] // CHUNK) <= (pos[:, None] // CHUNK)
    mask = np.where(visible[None], np.exp(lg[:, None, None] * dist[None]), 0.0)
    qd = np.exp(lg[:, None] * (pos[None, :] + 1.0))[:, :, None]
    kd = np.exp(lg[:, None] * (tb - 1.0 - pos[None, :]))[:, :, None]
    bd = tuple(float(v) for v in np.exp(lg * tb))
    return (jnp.asarray(mask, F32), jnp.asarray(qd, F32), jnp.asarray(kd, F32), bd)


def _ret_specs(pin, cos2, sin2, gn_w, layer, tb):
    mask, qd, kd, bd = _ret_tables(tb)
    w = RET_HEADS * RET_DK
    col = lambda off: (lambda i: (i, off // w))
    full3 = lambda i: (0, 0, 0)
    in_specs = [
        pl.BlockSpec((tb, w), col(P_RQ)),
        pl.BlockSpec((tb, w), col(P_RK)),
        pl.BlockSpec((tb, w), col(P_RV)),
        pl.BlockSpec((tb, w), col(P_RG)),
        pl.BlockSpec((tb, RET_DK), lambda i: (i, 0)),
        pl.BlockSpec((tb, RET_DK), lambda i: (i, 0)),
        pl.BlockSpec((RET_HEADS, tb, tb), full3),
        pl.BlockSpec((RET_HEADS, tb, 1), full3),
        pl.BlockSpec((RET_HEADS, tb, 1), full3),
        pl.BlockSpec((None, 1, w), lambda i: (layer, 0, 0)),
    ]
    args = [pin, pin, pin, pin, cos2, sin2, mask, qd, kd, gn_w]
    out_specs = [pl.BlockSpec((tb, w), lambda i: (i, 0))]
    out_shape = [jax.ShapeDtypeStruct((pin.shape[0], w), BF16)]
    scratch = [pltpu.VMEM((RET_HEADS, RET_DK, RET_DV), F32)]
    return in_specs, args, out_specs, out_shape, scratch, bd


def _ssd_stream(xbc_ref, z_ref, dt_ref, dtb_ref, aneg_ref, cw_ref, cb_ref, d_ref, nw_ref,
                o_ref, carry_ref, st_ref):
    tb = xbc_ref.shape[0]
    P, N = SSM_HEADDIM, SSM_STATE
    nx = SSM_HEADS * P

    @pl.when(pl.program_id(0) == 0)
    def _():
        carry_ref[...] = jnp.zeros_like(carry_ref)
        st_ref[...] = jnp.zeros_like(st_ref)

    x = xbc_ref[...].astype(F32)
    carry = carry_ref[...]
    acc = x * cw_ref[SSM_CONV - 1:SSM_CONV, :] + cb_ref[...]
    for s in range(1, SSM_CONV):
        acc = acc + _shift_rows(x, carry, s) * cw_ref[SSM_CONV - 1 - s:SSM_CONV - s, :]
        yield
    carry_ref[...] = x[tb - SUBLANES:, :]
    xbc = _silu(acc)
    yield

    dt = _softplus(dt_ref[...] + dtb_ref[...])
    adt = dt * aneg_ref[...]
    row = lax.broadcasted_iota(jnp.int32, (tb, tb), 0)
    colm = lax.broadcasted_iota(jnp.int32, (tb, tb), 1)
    causal = colm <= row
    tril = causal.astype(F32)
    a_col = jnp.dot(tril, adt, preferred_element_type=F32, precision=lax.Precision.HIGHEST)
    triu = (row <= colm).astype(F32)
    a_row = lax.dot_general(adt, triu, (((0,), (0,)), ((), ())),
                            preferred_element_type=F32, precision=lax.Precision.HIGHEST)
    a_end = a_col[tb - 1:tb, :]
    yield
    lane = lax.broadcasted_iota(jnp.int32, (tb, 2 * P), 1)
    first = lane < P

    groups = range(SSM_GROUPS)
    heads = range(SSM_HEADS)
    hpg = SSM_HEADS // SSM_GROUPS
    npair = SSM_HEADS // 2
    bm = [xbc[:, nx + g * N: nx + (g + 1) * N] for g in groups]
    cm = [xbc[:, nx + SSM_GROUPS * N + g * N: nx + SSM_GROUPS * N + (g + 1) * N] for g in groups]
    cb = [_dot_nt(cm[g].astype(BF16), bm[g].astype(BF16)) for g in groups]
    xp = [xbc[:, 2 * q * P:(2 * q + 2) * P] for q in range(npair)]
    xdt = [(xp[q] * jnp.where(first, dt[:, 2 * q:2 * q + 1], dt[:, 2 * q + 1:2 * q + 2])).astype(BF16)
           for q in range(npair)]
    st = [st_ref[q] for q in range(npair)]
    ac = [a_col[:, h:h + 1] for h in heads]
    yield
    wts = []
    for h in heads:
        dec = jnp.exp(jnp.where(causal, ac[h] - a_row[h:h + 1, :], -jnp.inf))
        wts.append((cb[h // hpg] * dec).astype(BF16))
        if h % 2:
            yield
    y = [_dot(wts[h], xdt[h // 2]) for h in heads]
    yield
    cdec = [(cm[h // hpg] * jnp.exp(ac[h])).astype(BF16) for h in heads]
    y = [y[h] + _dot(cdec[h], st[h // 2].astype(BF16)) for h in heads]
    yield
    bdec = [(bm[h // hpg] * jnp.exp(a_end[:, h:h + 1] - ac[h])).astype(BF16) for h in heads]
    snew = [jnp.exp(a_end[:, h:h + 1]) * st[h // 2] + _dot_tn(bdec[h], xdt[h // 2]) for h in heads]
    yield
    lane_s = lax.broadcasted_iota(jnp.int32, (N, 2 * P), 1)
    for q in range(npair):
        st_ref[q] = jnp.where(lane_s < P, snew[2 * q], snew[2 * q + 1])
    ypair = [jnp.where(first, y[2 * q], y[2 * q + 1])
             + xp[q] * jnp.where(first, d_ref[:, 2 * q:2 * q + 1], d_ref[:, 2 * q + 1:2 * q + 2])
             for q in range(npair)]
    yield
    for g in groups:
        yg = jnp.concatenate(ypair[g * hpg // 2:(g + 1) * hpg // 2], axis=1)
        gsl = slice(g * hpg * P, (g + 1) * hpg * P)
        yg = yg * _silu(z_ref[:, gsl].astype(F32))
        ms = jnp.mean(yg * yg, axis=-1, keepdims=True)
        o_ref[:, gsl] = (yg * lax.rsqrt(ms + RMS_EPS) * nw_ref[:, gsl]).astype(BF16)
        yield


def _ssd_specs(pin, tail, dt_bias, a_neg, conv_w, conv_b, d_skip, norm_w, layer, tb):
    nx = SSM_HEADS * SSM_HEADDIM
    H = SSM_HEADS
    lsel = lambda i: (layer, 0, 0)
    in_specs = [
        pl.BlockSpec((tb, SSM_XBC), lambda i: (i, P_SXBC // SSM_XBC)),
        pl.BlockSpec((tb, nx), lambda i: (i, P_SZ // nx)),
        pl.BlockSpec((tb, LANES), lambda i: (i, T_DT // LANES)),
        pl.BlockSpec((None, 1, LANES), lsel),
        pl.BlockSpec((None, 1, LANES), lsel),
        pl.BlockSpec((None, SSM_CONV, SSM_XBC), lsel),
        pl.BlockSpec((None, 1, SSM_XBC), lsel),
        pl.BlockSpec((None, 1, H), lsel),
        pl.BlockSpec((None, 1, nx), lsel),
    ]
    args = [pin, pin, tail, dt_bias, a_neg, conv_w, conv_b, d_skip, norm_w]
    out_specs = [pl.BlockSpec((tb, nx), lambda i: (i, 0))]
    out_shape = [jax.ShapeDtypeStruct((pin.shape[0], nx), BF16)]
    scratch = [pltpu.VMEM((SUBLANES, SSM_XBC), F32), pltpu.VMEM((H // 2, SSM_STATE, 2 * SSM_HEADDIM), F32)]
    return in_specs, args, out_specs, out_shape, scratch


MLA_PAD = 256
MLA_VX = MLA_V + 16


def _rope_pad(x, cosp, sinp):
    half = MLA_ROPE // 2
    return x * cosp + (pltpu.roll(x, half, 1) + pltpu.roll(x, LANES - half, 1)) * sinp


def _mla_stream(cq_ref, ckv_ref, kr_ref, qaw_ref, wqb_ref, kvaw_ref, wkvb_ref, qnw_ref,
                knw_n_ref, knw_r_ref, cos_ref, sin_ref, q_out, k_out, v_out):
    cosp = cos_ref[...]
    sinp = sin_ref[...]
    cq = cq_ref[...].astype(F32)
    ms = jnp.mean(cq * cq, axis=-1, keepdims=True)
    cqn = (cq * lax.rsqrt(ms + RMS_EPS) * qaw_ref[...]).astype(BF16)
    q = _dot(cqn, wqb_ref[...])
    yield
    ckv = ckv_ref[...].astype(F32)
    ms = jnp.mean(ckv * ckv, axis=-1, keepdims=True)
    ckvn = (ckv * lax.rsqrt(ms + RMS_EPS) * kvaw_ref[...]).astype(BF16)
    kv = _dot(ckvn, wkvb_ref[...])
    yield
    kr = kr_ref[...]
    ssr = jnp.sum(kr * kr, axis=-1, keepdims=True)
    scale = (MLA_QK ** -0.5) * LOG2E
    heads = range(MLA_HEADS)
    qh = [q[:, h * MLA_PAD:(h + 1) * MLA_PAD] for h in heads]
    kn = [kv[:, h * MLA_PAD: h * MLA_PAD + MLA_NOPE] for h in heads]
    qinv = [lax.rsqrt(jnp.sum(qh[h] * qh[h], axis=-1, keepdims=True) * (1.0 / MLA_QK) + RMS_EPS) for h in heads]
    kinv = [lax.rsqrt((jnp.sum(kn[h] * kn[h], axis=-1, keepdims=True) + ssr) * (1.0 / MLA_QK) + RMS_EPS)
            for h in heads]
    yield
    qh = [qh[h] * qinv[h] * qnw_ref[...] for h in heads]
    qr = [_rope_pad(qh[h][:, MLA_NOPE:], cosp, sinp) for h in heads]
    yield
    krh = [_rope_pad(kr * kinv[h] * knw_r_ref[...], cosp, sinp) for h in heads]
    ones = jnp.ones((MLA_VX - MLA_V, kv.shape[0]), F32)
    yield
    for h in heads:
        q_out[h] = (jnp.concatenate([qh[h][:, :MLA_NOPE], qr[h]], axis=1) * scale).T.astype(BF16)
        if h % 2:
            yield
    for h in heads:
        k_out[h] = jnp.concatenate([kn[h] * kinv[h] * knw_n_ref[...], krh[h]], axis=1).astype(BF16)
    yield
    for h in heads:
        vv = kv[:, h * MLA_PAD + MLA_NOPE:(h + 1) * MLA_PAD]
        v_out[h] = jnp.concatenate([vv.T, ones], axis=0).astype(BF16)
        if h % 2:
            yield


def _mla_specs(pin, tail, qaw, wqb, kvaw, wkvb, qnw, knw_n, knw_r, cosp, sinp, layer, tb):
    S = pin.shape[0]
    H = MLA_HEADS
    lsel = lambda i: (layer, 0, 0)
    in_specs = [
        pl.BlockSpec((tb, MLA_Q_RANK), lambda i: (i, P_CQ // MLA_Q_RANK)),
        pl.BlockSpec((tb, MLA_KV_RANK), lambda i: (i, P_CKV // MLA_KV_RANK)),
        pl.BlockSpec((tb, LANES), lambda i: (i, T_KROPE // LANES)),
        pl.BlockSpec((None, 1, MLA_Q_RANK), lsel),
        pl.BlockSpec((None, MLA_Q_RANK, H * MLA_PAD), lsel),
        pl.BlockSpec((None, 1, MLA_KV_RANK), lsel),
        pl.BlockSpec((None, MLA_KV_RANK, H * MLA_PAD), lsel),
        pl.BlockSpec((None, 1, MLA_PAD), lsel),
        pl.BlockSpec((None, 1, MLA_NOPE), lsel),
        pl.BlockSpec((None, 1, LANES), lsel),
        pl.BlockSpec((tb, LANES), lambda i: (i, 0)),
        pl.BlockSpec((tb, LANES), lambda i: (i, 0)),
    ]
    args = [pin, pin, tail, qaw, wqb, kvaw, wkvb, qnw, knw_n, knw_r, cosp, sinp]
    out_specs = [
        pl.BlockSpec((H, MLA_PAD, tb), lambda i: (0, 0, i)),
        pl.BlockSpec((H, tb, MLA_PAD), lambda i: (0, i, 0)),
        pl.BlockSpec((H, MLA_VX, tb), lambda i: (0, 0, i)),
    ]
    out_shape = [
        jax.ShapeDtypeStruct((H, MLA_PAD, S), BF16),
        jax.ShapeDtypeStruct((H, S, MLA_PAD), BF16),
        jax.ShapeDtypeStruct((H, MLA_VX, S), BF16),
    ]
    return in_specs, args, out_specs, out_shape, []


def _flash_body(it_ref, jt_ref, qt_ref, k_ref, vt_ref, o_ref, m_ref, acc_ref):
    t = pl.program_id(0)
    i = it_ref[t]
    j = jt_ref[t]
    tq = qt_ref.shape[2]
    tk = k_ref.shape[1]
    full = (j + 1) * tk <= i * tq
    last = j == ((i + 1) * tq - 1) // tk
    H = MLA_HEADS

    @pl.when(j == 0)
    def _():
        m_ref[...] = jnp.full_like(m_ref, -jnp.inf)
        acc_ref[...] = jnp.zeros_like(acc_ref)

    def step(diagonal):
        if diagonal:
            kc = (j * tk + lax.broadcasted_iota(jnp.int32, (tk, tq), 0)) // CHUNK
            qc = (i * tq + lax.broadcasted_iota(jnp.int32, (tk, tq), 1)) // CHUNK
            visible = kc <= qc

        def logits(h):
            s = _dot(k_ref[h], qt_ref[h])
            return jnp.where(visible, s, -jnp.inf) if diagonal else s

        def softmax(h, s):
            m_prev = m_ref[h]
            m_new = jnp.maximum(m_prev, jnp.max(s, axis=0, keepdims=True))
            alpha = jnp.exp2(m_prev - m_new)
            m_ref[h] = m_new
            return alpha, jnp.exp2((s - m_new).astype(BF16))

        def accumulate(h, alpha, p):
            acc_ref[h] = alpha * acc_ref[h] + _dot(vt_ref[h], p)

        s_next = logits(0)
        pending = None
        for h in range(H):
            s_cur = s_next
            if h + 1 < H:
                s_next = logits(h + 1)
            if pending is not None:
                accumulate(*pending)
            pending = (h,) + softmax(h, s_cur)
        accumulate(*pending)

    @pl.when(full)
    def _():
        step(False)

    @pl.when(jnp.logical_not(full))
    def _():
        step(True)

    @pl.when(last)
    def _():
        for h in range(H):
            acc = acc_ref[h]
            o_ref[:, h * MLA_V:(h + 1) * MLA_V] = (acc[:MLA_V] / acc[MLA_V:MLA_V + 1]).T.astype(BF16)


def _flash(qt, k, vt, tq=1024, tk=1024):
    H, S, _ = k.shape
    pairs = [(i, j) for i in range(S // tq) for j in range(((i + 1) * tq - 1) // tk + 1)]
    it = jnp.asarray([p[0] for p in pairs], jnp.int32)
    jt = jnp.asarray([p[1] for p in pairs], jnp.int32)
    grid_spec = pltpu.PrefetchScalarGridSpec(
        num_scalar_prefetch=2,
        grid=(len(pairs),),
        in_specs=[
            pl.BlockSpec((H, MLA_PAD, tq), lambda t, it, jt: (0, 0, it[t])),
            pl.BlockSpec((H, tk, MLA_PAD), lambda t, it, jt: (0, jt[t], 0)),
            pl.BlockSpec((H, MLA_VX, tk), lambda t, it, jt: (0, 0, jt[t])),
        ],
        out_specs=pl.BlockSpec((tq, H * MLA_V), lambda t, it, jt: (it[t], 0)),
        scratch_shapes=[pltpu.VMEM((H, 1, tq), F32), pltpu.VMEM((H, MLA_VX, tq), F32)],
    )
    return pl.pallas_call(
        _flash_body,
        grid_spec=grid_spec,
        out_shape=jax.ShapeDtypeStruct((S, H * MLA_V), BF16),
        compiler_params=_cparams(("arbitrary",)),
        name="mla_flash",
    )(it, jt, qt, k, vt)


def _rwkv_body(*refs, mix_v, companion):
    if mix_v:
        (r_ref, k_ref, v_ref, wl_ref, al_ref, gl_ref, mur, muk, muv, muwl, mual, mugl, w0, a0, w2, a2, g2,
         kk_ref, ka_ref, lnw_ref, lnb_ref, rk_ref, vf_ref, v0, v1, v2,
         o_ref, c_r, c_k, c_v, c_wl, c_al, c_gl, s_ref) = refs
    else:
        (r_ref, k_ref, v_ref, wl_ref, al_ref, gl_ref, mur, muk, muv, muwl, mual, mugl, w0, a0, w2, a2, g2,
         kk_ref, ka_ref, lnw_ref, lnb_ref, rk_ref,
         o_ref, vfirst_ref, c_r, c_k, c_v, c_wl, c_al, c_gl, s_ref) = refs
    tb = r_ref.shape[0]
    L = RWKV_L
    N = RWKV_HEAD
    L2 = 2 * L
    npair = RWKV_W // LANES
    half = tb // 2
    carries = (c_r, c_k, c_v, c_wl, c_al, c_gl)

    @pl.when(pl.program_id(0) == 0)
    def _():
        for c in carries:
            c[...] = jnp.zeros_like(c)
        s_ref[...] = jnp.zeros_like(s_ref)

    def mixed(x_ref, c_ref, mu_ref):
        x = x_ref[...].astype(F32)
        prev = _shift_rows(x, c_ref[...], 1)
        c_ref[...] = x[tb - SUBLANES:, :]
        return x + (prev - x) * mu_ref[...]

    r = mixed(r_ref, c_r, mur)
    k = mixed(k_ref, c_k, muk)
    v = mixed(v_ref, c_v, muv)
    wl = mixed(wl_ref, c_wl, muwl)
    al = mixed(al_ref, c_al, mual)
    gl = mixed(gl_ref, c_gl, mugl)
    if not mix_v:
        vfirst_ref[...] = v

    lane = lax.broadcasted_iota(jnp.int32, (L, LANES), 1)
    m0 = lane < N
    r2 = lax.broadcasted_iota(jnp.int32, (L2, L2), 0)
    c2 = lax.broadcasted_iota(jnp.int32, (L2, L2), 1)
    same = (r2 // L) == (c2 // L)
    strict = same & (c2 < r2)
    incl = same & (c2 <= r2)
    eye = (r2 == c2).astype(F32)
    ones_bd = ((r2 // N) == (c2 // N)).astype(BF16)
    rb = lax.broadcasted_iota(jnp.int32, (half, half), 0)
    cb = lax.broadcasted_iota(jnp.int32, (half, half), 1)
    tril_bd = ((rb // L) == (cb // L)) & (cb <= rb)

    def stack(x):
        return jnp.concatenate([jnp.where(m0, x, 0.0), jnp.where(m0, 0.0, x)], axis=0).astype(BF16)

    probs = [(c, p) for c in range(half // L) for p in range(npair)]
    rows_of = lambda c: slice(c * L, (c + 1) * L)
    lanes_of = lambda p: slice(p * LANES, (p + 1) * LANES)
    opnds = [{}, {}]
    gates = [None, None]
    states = [s_ref[p] for p in range(npair)]

    def front(hf):
        rs = slice(hf * half, (hf + 1) * half)
        w_raw = w0[...] + _mm3(jnp.tanh(wl[rs]), w2[...])
        lw = -jnp.exp(-_softplus(-w_raw) - 0.5)
        yield
        a_sig = _sigmoid(a0[...] + _mm3(al[rs], a2[...]))
        gates[hf] = _mm1(_sigmoid(gl[rs]), g2[...])
        yield
        vh = v[rs]
        if mix_v:
            lora = _mm1(_mm1(vh, v1[...]), v2[...])
            vh = vh + (vf_ref[rs, :] - vh) * _sigmoid(v0[...] + lora)
        yield
        kk = k[rs] * kk_ref[...]
        parts = []
        for p in range(npair):
            kp = kk[:, lanes_of(p)]
            n2 = _seg_sum(kp * kp, ones_bd)
            parts.append(kp / jnp.maximum(jnp.sqrt(n2), 1e-12))
            if p % 2:
                yield
        kk = jnp.concatenate(parts, axis=1)
        kh = k[rs] * (1.0 + (a_sig - 1.0) * ka_ref[...])
        ah = -kk
        bh = kk * a_sig
        rh = r[rs]
        cum_blk = _mm_exact_lhs(tril_bd, lw)
        yield
        for n, (c, p) in enumerate(probs):
            rows, sl = rows_of(c), lanes_of(p)
            cum = cum_blk[rows, sl]
            cum_end = cum[L - 1:L, :]
            rr = rh[rows, sl]
            kc = kh[rows, sl]
            vc = vh[rows, sl]
            e_neg = jnp.exp(-cum)
            e_end = jnp.exp(cum_end - cum)
            re = rr * jnp.exp(cum)
            xr_f = jnp.concatenate([jnp.where(m0, re, 0.0), jnp.where(m0, 0.0, re)], axis=0)
            opnds[hf][c, p] = dict(
                xr_f=xr_f, xa=stack(ah[rows, sl] * jnp.exp(cum - lw[rows, sl])), xr=xr_f.astype(BF16),
                xb=stack(bh[rows, sl] * e_neg), xk=stack(kc * e_neg), xbh=stack(bh[rows, sl] * e_end),
                xkh=stack(kc * e_end), vs=stack(vc), decay_end=jnp.exp(cum_end),
                rk=rr * kc * rk_ref[:, sl], v=vc)
            if n % 2:
                yield

    def back(hf):
        opnd = opnds[hf]
        amat = {}
        for cp in probs:
            o = opnd[cp]
            pmat = _dot_nt(jnp.concatenate([o["xa"], o["xr"]], axis=0),
                           jnp.concatenate([o["xb"], o["xk"]], axis=0))
            amat[cp] = dict(
                ab=jnp.where(strict, pmat[:L2, :L2], 0.0),
                ak=jnp.where(strict, pmat[:L2, L2:], 0.0).astype(BF16),
                rb=jnp.where(incl, pmat[L2:, :L2], 0.0).astype(BF16),
                rk=jnp.where(incl, pmat[L2:, L2:], 0.0).astype(BF16))
        yield
        tinv = {cp: eye + amat[cp]["ab"] for cp in probs}
        pw = {cp: amat[cp]["ab"] for cp in probs}
        for _ in range(int(math.log2(L)) - 1):
            for cp in probs:
                pwb = pw[cp].astype(BF16)
                pw[cp] = _dot(pwb, pwb)
            yield
            for cp in probs:
                tinv[cp] = tinv[cp] + _mm1(tinv[cp], pw[cp])
            yield
        w1 = {cp: _dot(amat[cp]["ak"], opnd[cp]["vs"]) for cp in probs}
        yield
        mub = {}
        for cp in probs:
            rhs = jnp.concatenate([opnd[cp]["xa"], w1[cp].astype(BF16)], axis=1)
            mub[cp] = _dot(tinv[cp].astype(BF16), rhs).astype(BF16)
        yield
        coef = {}
        for cp in probs:
            o = opnd[cp]
            ry = _dot(amat[cp]["rb"], mub[cp])
            gh = _dot_tn(o["xbh"], mub[cp])
            coef[cp] = dict(
                m_r=(o["xr_f"] + ry[:, :LANES]).astype(BF16),
                y1=ry[:, LANES:] + _dot(amat[cp]["rk"], o["vs"]),
                g=(eye * o["decay_end"] + gh[:, :LANES]).astype(BF16),
                h=gh[:, LANES:] + _dot_tn(o["xkh"], o["vs"]))
        yield
        ys = {}
        for (c, p) in probs:
            st = states[p].astype(BF16)
            cf = coef[c, p]
            yst = _dot(cf["m_r"], st) + cf["y1"]
            states[p] = _dot(cf["g"], st) + cf["h"]
            ys[c, p] = yst[:L] + yst[L:]
        yield
        yc = {cp: ys[cp] - _seg_sum(ys[cp], ones_bd) * (1.0 / N) for cp in probs}
        yield
        bonus = {cp: _seg_sum(opnd[cp]["rk"], ones_bd) * opnd[cp]["v"] for cp in probs}
        var = {cp: _seg_sum(yc[cp] * yc[cp], ones_bd) * (1.0 / N) for cp in probs}
        yield
        for (c, p) in probs:
            rows, sl = rows_of(c), lanes_of(p)
            out_rows = slice(hf * half + c * L, hf * half + (c + 1) * L)
            yn = yc[c, p] * lax.rsqrt(var[c, p] + RWKV_GN_EPS) * lnw_ref[:, sl] + lnb_ref[:, sl]
            o_ref[out_rows, sl] = ((yn + bonus[c, p]) * gates[hf][rows, sl]).astype(BF16)

    _interleave(front(0), front(1))
    _interleave(_roundrobin(back(0), back(1)), companion)
    for p in range(npair):
        s_ref[p] = states[p]


def _rwkv_specs(pin, tail, vec, mats, vfirst, vmix, layer, tb):
    S = pin.shape[0]
    W = RWKV_W
    lsel = lambda i: (layer, 0, 0)
    lsel1 = lambda i: (layer - 1, 0, 0)
    in_specs = [
        pl.BlockSpec((tb, W), lambda i: (i, P_WR // W)),
        pl.BlockSpec((tb, W), lambda i: (i, P_WK // W)),
        pl.BlockSpec((tb, W), lambda i: (i, P_WV // W)),
        pl.BlockSpec((tb, LANES), lambda i: (i, T_WL // LANES)),
        pl.BlockSpec((tb, LANES), lambda i: (i, T_AL // LANES)),
        pl.BlockSpec((tb, LANES), lambda i: (i, T_GL // LANES)),
    ]
    args = [pin] * 3 + [tail] * 3
    for name, width in (("mu_r", W), ("mu_k", W), ("mu_v", W), ("mu_wl", LANES), ("mu_al", LANES),
                        ("mu_gl", LANES), ("w0", W), ("a0", W)):
        in_specs.append(pl.BlockSpec((None, 1, width), lsel))
        args.append(vec[name])
    for name in ("w2", "a2", "g2"):
        in_specs.append(pl.BlockSpec((None, LANES, W), lsel))
        args.append(mats[name])
    for name in ("k_k", "k_a", "ln_w", "ln_b", "r_k"):
        in_specs.append(pl.BlockSpec((None, 1, W), lsel))
        args.append(vec[name])
    mix_v = vmix is not None
    if mix_v:
        v0, v1, v2 = vmix
        in_specs += [pl.BlockSpec((tb, W), lambda i: (i, 0)),
                     pl.BlockSpec((None, 1, W), lsel1),
                     pl.BlockSpec((None, W, LANES), lsel1),
                     pl.BlockSpec((None, LANES, W), lsel1)]
        args += [vfirst, v0, v1, v2]
    out_spec = pl.BlockSpec((tb, W), lambda i: (i, 0))
    out_specs = [out_spec] if mix_v else [out_spec, out_spec]
    out_shape = [jax.ShapeDtypeStruct((S, W), BF16)]
    if not mix_v:
        out_shape.append(jax.ShapeDtypeStruct((S, W), F32))
    scratch = ([pltpu.VMEM((SUBLANES, W), F32)] * 3 + [pltpu.VMEM((SUBLANES, LANES), F32)] * 3
               + [pltpu.VMEM((W // LANES, LANES, LANES), F32)])
    return in_specs, args, out_specs, out_shape, scratch


def _mixers_body(*refs, counts, mix_v, block_decay):
    refs = list(refs)
    take = lambda n: [refs.pop(0) for _ in range(n)]
    ins = [take(n) for n in counts["in"]]
    outs = [take(n) for n in counts["out"]]
    scr = [take(n) for n in counts["scratch"]]
    ret, ssd, mla, rwkv = (ins[m] + outs[m] + scr[m] for m in range(4))
    companion = _chain(_mla_stream(*mla), _ret_stream(*ret, block_decay=block_decay), _ssd_stream(*ssd))
    _rwkv_body(*rwkv, mix_v=mix_v, companion=companion)


def _mixers(pin, tail, cos_ret, sin_ret, gn_w, ssd_params, mla_params, cos_mla, sin_mla, vec, mats, vfirst, vmix,
            layer, tb=ROW_BLOCK):
    S = pin.shape[0]
    r_in, r_args, r_out, r_shape, r_scr, bd = _ret_specs(pin, cos_ret, sin_ret, gn_w, layer, tb)
    parts = [
        (r_in, r_args, r_out, r_shape, r_scr),
        _ssd_specs(pin, tail, *ssd_params, layer, tb),
        _mla_specs(pin, tail, *mla_params, cos_mla, sin_mla, layer, tb),
        _rwkv_specs(pin, tail, vec, mats, vfirst, vmix, layer, tb),
    ]
    counts = {"in": [len(p[0]) for p in parts], "out": [len(p[2]) for p in parts],
              "scratch": [len(p[4]) for p in parts]}
    outs = pl.pallas_call(
        functools.partial(_mixers_body, counts=counts, mix_v=vmix is not None, block_decay=bd),
        grid=(S // tb,),
        in_specs=[sp for p in parts for sp in p[0]],
        out_specs=[sp for p in parts for sp in p[2]],
        out_shape=[sh for p in parts for sh in p[3]],
        scratch_shapes=[sc for p in parts for sc in p[4]],
        compiler_params=_cparams(("arbitrary",)),
        name="mixers",
    )(*[a for p in parts for a in p[1]])
    o_ret, o_ssd, qt, k, vt, o_rwkv = outs[:6]
    return o_ret, o_ssd, (qt, k, vt), o_rwkv, (outs[6] if vmix is None else None)


def _merge_out_body(oa_ref, ob_ref, oc_ref, od_ref, g0_ref, g1_ref, g2_ref, g3_ref, wb_ref, wo_ref, x_ref,
                    out_ref):
    o_refs = (oa_ref, ob_ref, oc_ref, od_ref)
    g_refs = (g0_ref, g1_ref, g2_ref, g3_ref)
    acc = None
    for n in range(N_BRANCH):
        contrib = _sigmoid(g_refs[n][...].astype(F32)) * _dot(o_refs[n][...], wb_ref[n])
        acc = contrib if acc is None else acc + contrib
    out_ref[...] = x_ref[...] + _dot(acc.astype(BF16), wo_ref[...])


def _merge_out(o_a, o_b, o_c, o_d, pin, w_branch, w_out, x, layer, tm=256):
    S, D = x.shape
    ospec = pl.BlockSpec((tm, BRANCH_W), lambda i: (i, 0))
    gspec = lambda n: pl.BlockSpec((tm, D), lambda i: (i, n))
    resident = pl.Buffered(1)
    return pl.pallas_call(
        _merge_out_body,
        grid=(S // tm,),
        in_specs=[ospec, ospec, ospec, ospec, gspec(0), gspec(1), gspec(2), gspec(3),
                  pl.BlockSpec((None, N_BRANCH, BRANCH_W, D), lambda i: (layer, 0, 0, 0), pipeline_mode=resident),
                  pl.BlockSpec((None, D, D), lambda i: (layer, 0, 0), pipeline_mode=resident),
                  pl.BlockSpec((tm, D), lambda i: (i, 0))],
        out_specs=pl.BlockSpec((tm, D), lambda i: (i, 0)),
        out_shape=jax.ShapeDtypeStruct((S, D), F32),
        compiler_params=_cparams(("arbitrary",)),
        name="merge_out",
    )(o_a, o_b, o_c, o_d, pin, pin, pin, pin, w_branch, w_out, x)


def _ffn_body(x_ref, nw_ref, wg_ref, wu_ref, wd_ref, out_ref, h_ref):
    @pl.when(pl.program_id(1) == 0)
    def _():
        x = x_ref[...]
        ms = jnp.mean(x * x, axis=-1, keepdims=True)
        h_ref[...] = (x * lax.rsqrt(ms + RMS_EPS) * nw_ref[...]).astype(BF16)
        out_ref[...] = x

    h = h_ref[...]
    act = (_silu(_dot(h, wg_ref[...])) * _dot(h, wu_ref[...])).astype(BF16)
    out_ref[...] += _dot(act, wd_ref[...])


def _ffn(x, norm_w, w_gu, w_down, layer, tm=1024, tf=512):
    S, D = x.shape
    nf = D_FF // tf
    return pl.pallas_call(
        _ffn_body,
        grid=(S // tm, nf),
        in_specs=[
            pl.BlockSpec((tm, D), lambda i, f: (i, 0)),
            pl.BlockSpec((None, 1, D), lambda i, f: (layer, 0, 0)),
            pl.BlockSpec((None, D, tf), lambda i, f: (layer, 0, f)),
            pl.BlockSpec((None, D, tf), lambda i, f: (layer, 0, f + nf)),
            pl.BlockSpec((None, tf, D), lambda i, f: (layer, f, 0)),
        ],
        out_specs=pl.BlockSpec((tm, D), lambda i, f: (i, 0)),
        out_shape=jax.ShapeDtypeStruct((S, D), F32),
        scratch_shapes=[pltpu.VMEM((tm, D), BF16)],
        compiler_params=_cparams(("arbitrary", "arbitrary")),
        name="ffn",
    )(x, norm_w, w_gu, w_gu, w_down)


def _pad_last(a, width):
    pad = width - a.shape[-1]
    if pad == 0:
        return a
    return jnp.pad(a, [(0, 0)] * (a.ndim - 1) + [(0, pad)])


def _pad_axis(a, axis, width):
    pad = width - a.shape[axis]
    if pad == 0:
        return a
    cfg = [(0, 0)] * a.ndim
    cfg[axis] = (0, pad)
    return jnp.pad(a, cfg)


_MAIN_PIECES = (
    (_O_GATE, P_GATE, N_BRANCH * D_MODEL),
    (_O_RQ, P_RQ, 512), (_O_RK, P_RK, 512), (_O_RV, P_RV, 512), (_O_RG, P_RG, 512),
    (_O_SXBC, P_SXBC, SSM_XBC), (_O_SZ, P_SZ, 512),
    (_O_CQ, P_CQ, MLA_Q_RANK),
    (_O_RW, P_WR, RWKV_W), (_O_RW + RWKV_W, P_WK, RWKV_W), (_O_RW + 2 * RWKV_W, P_WV, RWKV_W),
    (_O_CKV, P_CKV, MLA_KV_RANK),
)
_TAIL_PIECES = (
    (_O_SDT, T_DT, SSM_HEADS),
    (_O_KROPE, T_KROPE, MLA_ROPE),
    (_O_RW + 3 * RWKV_W, T_WL, RWKV_W_LORA),
    (_O_RW + 3 * RWKV_W + RWKV_W_LORA, T_AL, RWKV_A_LORA),
    (_O_RW + 3 * RWKV_W + RWKV_W_LORA + RWKV_A_LORA, T_GL, RWKV_G_LORA),
)
PACK_ROWS = 512


def _pack_tables(pieces, total, tile):
    src_of, valid_of = [], []
    for t in range(total // tile):
        lo = t * tile
        src, valid = 0, 0
        for s0, d0, width in pieces:
            if d0 <= lo < d0 + width:
                src, valid = s0 + (lo - d0), min(tile, d0 + width - lo)
        assert src % SUBLANES == 0
        src_of.append(src // SUBLANES)
        valid_of.append(valid)
    return jnp.asarray(src_of, jnp.int32), jnp.asarray(valid_of, jnp.int32)


def _pack_body(src_ref, valid_ref, w_ref, o_ref):
    w = w_ref[0]
    rows = lax.broadcasted_iota(jnp.int32, w.shape, 0)
    keep = rows < valid_ref[pl.program_id(1)]
    o_ref[...] = jnp.where(keep, w, 0.0).astype(BF16)


def _pack_rows(w_t, pieces, total, tile, name):
    depth, _, d = w_t.shape
    src_of, valid_of = _pack_tables(pieces, total, tile)
    grid_spec = pltpu.PrefetchScalarGridSpec(
        num_scalar_prefetch=2,
        grid=(depth, total // tile),
        in_specs=[pl.BlockSpec((pl.Element(1), pl.Element(tile), pl.Element(d)),
                               lambda l, t, src, valid: (l, src[t] * SUBLANES, 0))],
        out_specs=pl.BlockSpec((None, tile, d), lambda l, t, src, valid: (l, t, 0)),
    )
    return pl.pallas_call(
        _pack_body,
        grid_spec=grid_spec,
        out_shape=jax.ShapeDtypeStruct((depth, total, d), BF16),
        compiler_params=_cparams(("arbitrary", "arbitrary")),
        name=name,
    )(src_of, valid_of, w_t)


def _pack_w_in(w_in):
    w_t = jnp.swapaxes(w_in, 1, 2)
    w_main_t = _pack_rows(w_t, _MAIN_PIECES, N_MAIN, PACK_ROWS, "pack_w_main")
    w_tail_t = _pack_rows(w_t, _TAIL_PIECES, N_TAIL, LANES, "pack_w_tail")
    return w_main_t, w_tail_t


def _rope_tables(positions, dim):
    inv = 1.0 / (ROPE_THETA ** (jnp.arange(0, dim, 2, dtype=F32) / dim))
    ang = positions.astype(F32)[:, None] * inv
    return jnp.cos(ang), jnp.sin(ang)


def kernel(x, positions, norm1_w, w_in, ret_gn_w, ssm_conv_w, ssm_conv_b, ssm_dt_bias, ssm_a_log, ssm_d, ssm_norm_w, mla_q_a_norm_w, mla_w_qb, mla_kv_a_norm_w, mla_w_kvb, mla_q_norm_w, mla_k_norm_w, rwkv_mu, rwkv_w0, rwkv_w2, rwkv_a0, rwkv_a2, rwkv_g2, rwkv_v0, rwkv_v1, rwkv_v2, rwkv_k_k, rwkv_k_a, rwkv_r_k, rwkv_ln_w, rwkv_ln_b, w_branch, w_out, norm2_w, ffn_w_gu, ffn_w_down):
    B, S, D = x.shape
    assert B == 1 and D == D_MODEL and S % 1024 == 0
    xs = x[0]
    pos = positions[0]

    c_r, s_r = _rope_tables(pos, RET_DK)
    cos_ret = jnp.concatenate([c_r, c_r], axis=-1)
    sin_ret = jnp.concatenate([-s_r, s_r], axis=-1)
    c_m, s_m = _rope_tables(pos, MLA_ROPE)
    cos_mla = _pad_last(jnp.concatenate([c_m, c_m], axis=-1), LANES)
    sin_mla = _pad_last(jnp.concatenate([-s_m, s_m], axis=-1), LANES)

    row = lambda a: a[:, None, :]
    w_main, w_tail = _pack_w_in(w_in)
    norm1 = row(norm1_w)
    norm2 = row(norm2_w)
    gn_w = row(ret_gn_w)
    conv_b = row(ssm_conv_b)
    d_skip = row(ssm_d)
    ssm_nw = row(ssm_norm_w)
    dt_bias = row(_pad_last(ssm_dt_bias, LANES))
    a_neg = row(_pad_last(-jnp.exp(ssm_a_log.astype(F32)), LANES))
    qaw = row(mla_q_a_norm_w)
    kvaw = row(mla_kv_a_norm_w)
    wqb = _pad_last(mla_w_qb.reshape(DEPTH, MLA_Q_RANK, MLA_HEADS, MLA_QK), MLA_PAD)
    wqb = wqb.reshape(DEPTH, MLA_Q_RANK, MLA_HEADS * MLA_PAD).astype(BF16)
    wkvb = mla_w_kvb.astype(BF16)
    qnw = row(_pad_last(mla_q_norm_w, MLA_PAD))
    knw_n = row(mla_k_norm_w[:, :MLA_NOPE])
    knw_r = row(_pad_last(mla_k_norm_w[:, MLA_NOPE:], LANES))
    W = RWKV_W
    mu = rwkv_mu
    vec = {
        "mu_r": row(mu[:, :W]), "mu_k": row(mu[:, W:2 * W]), "mu_v": row(mu[:, 2 * W:3 * W]),
        "mu_wl": row(_pad_last(mu[:, 3 * W:3 * W + RWKV_W_LORA], LANES)),
        "mu_al": row(_pad_last(mu[:, 3 * W + RWKV_W_LORA:3 * W + RWKV_W_LORA + RWKV_A_LORA], LANES)),
        "mu_gl": row(mu[:, 3 * W + RWKV_W_LORA + RWKV_A_LORA:]),
        "w0": row(rwkv_w0), "a0": row(rwkv_a0), "k_k": row(rwkv_k_k), "k_a": row(rwkv_k_a),
        "ln_w": row(rwkv_ln_w), "ln_b": row(rwkv_ln_b), "r_k": row(rwkv_r_k.reshape(DEPTH, W)),
    }
    mats = {"w2": _pad_axis(rwkv_w2, 1, LANES), "a2": _pad_axis(rwkv_a2, 1, LANES), "g2": rwkv_g2}
    v0 = row(rwkv_v0)
    v1 = _pad_last(rwkv_v1, LANES)
    v2 = _pad_axis(rwkv_v2, 1, LANES)
    wb = w_branch.astype(BF16)
    wo = w_out.astype(BF16)
    wgu = ffn_w_gu.astype(BF16)
    wdn = ffn_w_down.astype(BF16)

    v_first = None
    for l in range(DEPTH):
        pin, tail = _inproj(xs, norm1, w_main, w_tail, l)
        vmix = None if l == 0 else (v0, v1, v2)
        o_a, o_b, qkv, o_d, vf = _mixers(
            pin, tail, cos_ret, sin_ret, gn_w, (dt_bias, a_neg, ssm_conv_w, conv_b, d_skip, ssm_nw),
            (qaw, wqb, kvaw, wkvb, qnw, knw_n, knw_r), cos_mla, sin_mla, vec, mats, v_first, vmix, l)
        o_c = _flash(*qkv)
        if l == 0:
            v_first = vf
        xs = _merge_out(o_a, o_b, o_c, o_d, pin, wb, wo, xs, l)
        xs = _ffn(xs, norm2, wgu, wdn, l)
    return xs[None]
```
